```python
import math
import jax
import jax.numpy as jnp
from jax import lax
import numpy as np

D_MODEL = 1024
BATCH = 8
SEQ = 2048
DEPTH = 4
DEC_BATCH = 128
DEC_SEQ = 4
PAST_LEN = 16384
PAGE_SIZE = 128

N_HEADS = 4
HEAD_DIM = D_MODEL // N_HEADS
D_INNER = N_HEADS * HEAD_DIM
CHUNK = 128
POOL_WINDOWS = (2, 4, 8, 16)
N_POOL_GROUPS = len(POOL_WINDOWS)
POOL_GROUP_DIM = D_MODEL // N_POOL_GROUPS
POOL_BUF = max(POOL_WINDOWS) - 1
N_MEM = 256
N_XHEADS = 4
XHEAD_DIM = D_MODEL // N_XHEADS
D_FF = ((8 * D_MODEL // 3 + 127) // 128) * 128
N_MLSTM_LAYERS = (DEPTH + 1) // 2
N_POOL_LAYERS = DEPTH // 2
EPS = 1e-6

kernel_name = 'hybrid_mlstm_pool_macaron_memxattn_step'

F32 = jnp.float32


def rmsnorm(x, g):
    xf = x.astype(F32)
    y = xf * lax.rsqrt(jnp.mean(xf * xf, axis=-1, keepdims=True) + EPS) * g.astype(F32)
    return y.astype(x.dtype)


def swiglu(h, w_up, w_down):
    gate, up = jnp.split(h @ w_up, 2, axis=-1)
    return (jax.nn.silu(gate) * up) @ w_down


def mlstm_scan(q, k, v, ig, lf, C, n, m):
    B, H, T, _ = q.shape
    L = CHUNK if T % CHUNK == 0 else T
    nc = T // L
    mask = jnp.tril(jnp.ones((L, L), dtype=bool))

    def to_chunks(a):
        return jnp.moveaxis(a.reshape((B, H, nc, L) + a.shape[3:]), 2, 0)

    def step(carry, inp):
        C, n, m = carry
        qc, kc, vc, ic, fc = inp
        b = jnp.cumsum(fc, axis=-1)
        a = b + m[..., None]
        dlog = b[..., :, None] - b[..., None, :] + ic[..., None, :]
        dlog = jnp.where(mask, dlog, -jnp.inf)
        mt = jnp.maximum(a, jnp.max(dlog, axis=-1))
        dw = jnp.exp(dlog - mt[..., None])
        s = jnp.einsum('bhtd,bhsd->bhts', qc, kc) * dw
        inter = jnp.exp(a - mt)
        num = inter[..., None] * jnp.einsum('bhtd,bhde->bhte', qc, C) + jnp.einsum('bhts,bhse->bhte', s, vc)
        den = inter * jnp.einsum('bhtd,bhd->bht', qc, n) + jnp.sum(s, axis=-1)
        h = num / jnp.maximum(jnp.abs(den), jnp.exp(-mt))[..., None]
        m_new = mt[..., -1]
        decay = jnp.exp(b[..., -1] + m - m_new)
        wk = jnp.exp(b[..., -1:] - b + ic - m_new[..., None])
        C_new = decay[..., None, None] * C + jnp.einsum('bhs,bhsd,bhse->bhde', wk, kc, vc)
        n_new = decay[..., None] * n + jnp.einsum('bhs,bhsd->bhd', wk, kc)
        return (C_new, n_new, m_new), h

    (C, n, m), hs = lax.scan(step, (C, n, m), tuple(to_chunks(a) for a in (q, k, v, ig, lf)))
    hs = jnp.moveaxis(hs, 0, 2).reshape(B, H, T, HEAD_DIM)
    return hs, C, n, m


def mlstm_mixer(h, w_in, b_i, b_f, g_head, w_out, C, n, m):
    B, T, _ = h.shape
    p = (h @ w_in).astype(F32)

    def heads(a):
        return a.reshape(B, T, N_HEADS, HEAD_DIM).transpose(0, 2, 1, 3)

    q = heads(p[..., :D_INNER])
    k = heads(p[..., D_INNER:2 * D_INNER]) * (HEAD_DIM ** -0.5)
    v = heads(p[..., 2 * D_INNER:3 * D_INNER])
    o = p[..., 3 * D_INNER:4 * D_INNER]
    ig = (p[..., 4 * D_INNER:4 * D_INNER + N_HEADS] + b_i.astype(F32)).transpose(0, 2, 1)
    lf = jax.nn.log_sigmoid(p[..., 4 * D_INNER + N_HEADS:] + b_f.astype(F32)).transpose(0, 2, 1)
    hs, C, n, m = mlstm_scan(q, k, v, ig, lf, C.astype(F32), n.astype(F32), m.astype(F32))
    hn = hs * lax.rsqrt(jnp.mean(hs * hs, axis=-1, keepdims=True) + EPS) * g_head.astype(F32)[:, None, :]
    hn = hn.transpose(0, 2, 1, 3).reshape(B, T, D_INNER)
    y = jax.nn.sigmoid(o) * hn
    return (y @ w_out.astype(F32)).astype(h.dtype), C, n, m


def pool_mixer(h, w_in, w_grp, scale, w_out, buf, start_pos):
    T = h.shape[1]
    u = h @ w_in
    ext = jnp.concatenate([buf.astype(u.dtype), u], axis=1)
    ef = ext.astype(F32)
    cs = jnp.concatenate([jnp.zeros_like(ef[:, :1]), jnp.cumsum(ef, axis=1)], axis=1)
    hi = cs[:, POOL_BUF + 1:]
    cnt_pos = start_pos + 1 + jnp.arange(T)
    outs = []
    for g, w in enumerate(POOL_WINDOWS):
        sl = slice(g * POOL_GROUP_DIM, (g + 1) * POOL_GROUP_DIM)
        lo = cs[:, POOL_BUF + 1 - w:POOL_BUF + 1 - w + T, sl]
        cnt = jnp.minimum(cnt_pos, w).astype(F32)[None, :, None]
        d = (hi[..., sl] - lo) / cnt - ef[:, POOL_BUF:, sl]
        outs.append(d @ w_grp[g].astype(F32))
    y = jnp.concatenate(outs, axis=-1) * scale.astype(F32)
    return (y @ w_out.astype(F32)).astype(h.dtype), ext[:, -POOL_BUF:]


def mem_kv(mem, w_kv):
    B, M, _ = mem.shape
    kv = (mem @ w_kv).reshape(B, M, 2, N_XHEADS, XHEAD_DIM)
    return kv[:, :, 0], kv[:, :, 1]


def cross_attn(h, w_q, k, v, w_o):
    B, T, _ = h.shape
    q = (h @ w_q).reshape(B, T, N_XHEADS, XHEAD_DIM)
    s = jnp.einsum('bthd,bmhd->bhtm', q.astype(F32), k.astype(F32)) * (XHEAD_DIM ** -0.5)
    p = jax.nn.softmax(s, axis=-1)
    o = jnp.einsum('bhtm,bmhd->bthd', p, v.astype(F32)).reshape(B, T, N_XHEADS * XHEAD_DIM)
    return (o @ w_o.astype(F32)).astype(h.dtype)


def setup_inputs(seed: int = 0) -> dict:
    key = jax.random.key(seed)
    ks = iter(jax.random.split(key, 32))

    def nrm(shape, scale=1.0):
        return jax.random.normal(next(ks), shape, F32) * scale

    D = D_MODEL
    return {
        'x_prompt': nrm((BATCH, SEQ, D)),
        'x_sample': nrm((DEC_BATCH, DEC_SEQ, D)),
        'mem_prompt': nrm((BATCH, N_MEM, D)),
        'cache_mem_k': nrm((DEPTH, DEC_BATCH, N_MEM, N_XHEADS, XHEAD_DIM)),
        'cache_mem_v': nrm((DEPTH, DEC_BATCH, N_MEM, N_XHEADS, XHEAD_DIM)),
        'state_mlstm_C': nrm((N_MLSTM_LAYERS, DEC_BATCH, N_HEADS, HEAD_DIM, HEAD_DIM), 0.1),
        'state_mlstm_n': nrm((N_MLSTM_LAYERS, DEC_BATCH, N_HEADS, HEAD_DIM), 0.1),
        'state_mlstm_m': nrm((N_MLSTM_LAYERS, DEC_BATCH, N_HEADS), 0.5),
        'state_pool_buf': nrm((N_POOL_LAYERS, DEC_BATCH, POOL_BUF, D)),
        'norm_g': 1.0 + nrm((DEPTH, 4, D), 0.05),
        'final_g': 1.0 + nrm((D,), 0.05),
        'ffn_w_up': nrm((DEPTH, 2, D, 2 * D_FF), D ** -0.5),
        'ffn_w_down': nrm((DEPTH, 2, D_FF, D), D_FF ** -0.5),
        'mlstm_w_in': nrm((N_MLSTM_LAYERS, D, 4 * D_INNER + 2 * N_HEADS), D ** -0.5),
        'mlstm_b_i': nrm((N_MLSTM_LAYERS, N_HEADS), 0.1),
        'mlstm_b_f': 3.0 + nrm((N_MLSTM_LAYERS, N_HEADS), 0.5),
        'mlstm_g_head': 1.0 + nrm((N_MLSTM_LAYERS, N_HEADS, HEAD_DIM), 0.05),
        'mlstm_w_out': nrm((N_MLSTM_LAYERS, D_INNER, D), D_INNER ** -0.5),
        'pool_w_in': nrm((N_POOL_LAYERS, D, D), D ** -0.5),
        'pool_w_grp': nrm((N_POOL_LAYERS, N_POOL_GROUPS, POOL_GROUP_DIM, POOL_GROUP_DIM), POOL_GROUP_DIM ** -0.5),
        'pool_scale': 1.0 + nrm((N_POOL_LAYERS, D), 0.1),
        'pool_w_out': nrm((N_POOL_LAYERS, D, D), D ** -0.5),
        'xattn_w_q': nrm((DEPTH, D, D), D ** -0.5),
        'xattn_w_kv': nrm((DEPTH, D, 2 * D), D ** -0.5),
        'xattn_w_o': nrm((DEPTH, D, D), D ** -0.5),
    }


def reference(x_prompt, x_sample, mem_prompt, cache_mem_k, cache_mem_v, state_mlstm_C, state_mlstm_n,
              state_mlstm_m, state_pool_buf, norm_g, final_g, ffn_w_up, ffn_w_down, mlstm_w_in, mlstm_b_i,
              mlstm_b_f, mlstm_g_head, mlstm_w_out, pool_w_in, pool_w_grp, pool_scale, pool_w_out,
              xattn_w_q, xattn_w_kv, xattn_w_o):
    xp, xs = x_prompt, x_sample
    Bp = xp.shape[0]
    mk_p, mv_p = [], []
    Cp_l, np_l, mp_l, Cs_l, ns_l, ms_l = [], [], [], [], [], []
    bp_l, bs_l = [], []
    for l in range(DEPTH):
        g = norm_g[l]
        j = l // 2
        xp = xp + 0.5 * swiglu(rmsnorm(xp, g[0]), ffn_w_up[l, 0], ffn_w_down[l, 0])
        xs = xs + 0.5 * swiglu(rmsnorm(xs, g[0]), ffn_w_up[l, 0], ffn_w_down[l, 0])
        if l % 2 == 0:
            yp, C1, n1, m1 = mlstm_mixer(rmsnorm(xp, g[1]), mlstm_w_in[j], mlstm_b_i[j], mlstm_b_f[j],
                                         mlstm_g_head[j], mlstm_w_out[j],
                                         jnp.zeros((Bp, N_HEADS, HEAD_DIM, HEAD_DIM), F32),
                                         jnp.zeros((Bp, N_HEADS, HEAD_DIM), F32),
                                         jnp.zeros((Bp, N_HEADS), F32))
            ys, C2, n2, m2 = mlstm_mixer(rmsnorm(xs, g[1]), mlstm_w_in[j], mlstm_b_i[j], mlstm_b_f[j],
                                         mlstm_g_head[j], mlstm_w_out[j],
                                         state_mlstm_C[j], state_mlstm_n[j], state_mlstm_m[j])
            Cp_l.append(C1.astype(x_prompt.dtype)); np_l.append(n1.astype(x_prompt.dtype)); mp_l.append(m1.astype(x_prompt.dtype))
            Cs_l.append(C2.astype(state_mlstm_C.dtype)); ns_l.append(n2.astype(state_mlstm_n.dtype)); ms_l.append(m2.astype(state_mlstm_m.dtype))
        else:
            yp, b1 = pool_mixer(rmsnorm(xp, g[1]), pool_w_in[j], pool_w_grp[j], pool_scale[j], pool_w_out[j],
                                jnp.zeros((Bp, POOL_BUF, D_MODEL), xp.dtype), 0)
            ys, b2 = pool_mixer(rmsnorm(xs, g[1]), pool_w_in[j], pool_w_grp[j], pool_scale[j], pool_w_out[j],
                                state_pool_buf[j], PAST_LEN)
            bp_l.append(b1); bs_l.append(b2.astype(state_pool_buf.dtype))
        xp = xp + yp
        xs = xs + ys
        kp, vp = mem_kv(mem_prompt, xattn_w_kv[l])
        xp = xp + cross_attn(rmsnorm(xp, g[2]), xattn_w_q[l], kp, vp, xattn_w_o[l])
        xs = xs + cross_attn(rmsnorm(xs, g[2]), xattn_w_q[l], cache_mem_k[l], cache_mem_v[l], xattn_w_o[l])
        mk_p.append(kp); mv_p.append(vp)
        xp = xp + 0.5 * swiglu(rmsnorm(xp, g[3]), ffn_w_up[l, 1], ffn_w_down[l, 1])
        xs = xs + 0.5 * swiglu(rmsnorm(xs, g[3]), ffn_w_up[l, 1], ffn_w_down[l, 1])
    y_prompt = rmsnorm(xp, final_g)
    y_sample = rmsnorm(xs, final_g)
    return (y_prompt, y_sample, jnp.stack(mk_p), jnp.stack(mv_p),
            jnp.stack(Cp_l), jnp.stack(np_l), jnp.stack(mp_l),
            jnp.stack(Cs_l), jnp.stack(ns_l), jnp.stack(ms_l),
            jnp.stack(bp_l), jnp.stack(bs_l))
```

```python
import functools

import jax
import jax.numpy as jnp
from jax import lax
from jax.experimental import pallas as pl
from jax.experimental.pallas import tpu as pltpu

F32 = jnp.float32
BF16 = jnp.bfloat16

D_MODEL = 1024
BATCH = 8
SEQ = 2048
DEPTH = 4
DEC_BATCH = 128
DEC_SEQ = 4
PAST_LEN = 16384
N_HEADS = 4
HEAD_DIM = D_MODEL // N_HEADS
CHUNK = 128
POOL_WINDOWS = (2, 4, 8, 16)
POOL_GROUP_DIM = D_MODEL // len(POOL_WINDOWS)
POOL_BUF = max(POOL_WINDOWS) - 1
POOL_HALO = 16
N_MEM = 256
N_XHEADS = 4
XHEAD_DIM = D_MODEL // N_XHEADS
D_FF = ((8 * D_MODEL // 3 + 127) // 128) * 128
EPS = 1e-6

N_PROMPT = BATCH * SEQ
N_SAMPLE = DEC_BATCH * DEC_SEQ
N_ROWS = N_PROMPT + N_SAMPLE

LANES = 128
GATE_ROWS = 8
ROW_TILE = 512
FFN_CHUNK = 1408
VMEM_LIMIT = 56 * 1024 * 1024

_NT = (((1,), (1,)), ((), ()))
_TN = (((0,), (0,)), ((), ()))


def _params(*sem):
    return pltpu.CompilerParams(dimension_semantics=sem, vmem_limit_bytes=VMEM_LIMIT)


def _resident(block_shape, index_map):
    return pl.BlockSpec(block_shape, index_map, pipeline_mode=pl.Buffered(1))


def _rms(x, g):
    return x * lax.rsqrt(jnp.mean(x * x, axis=-1, keepdims=True) + EPS) * g


def _log_sigmoid(x):
    return jnp.minimum(x, 0.0) - jnp.log1p(jnp.exp(-jnp.abs(x)))


def _dot(a, b):
    return jnp.dot(a, b, preferred_element_type=F32)


def _ffn_kernel(x_ref, g_ref, wup_ref, wdn_ref, *rest, final):
    if final:
        fg_ref, o_ref = rest
    else:
        (o_ref,) = rest
    x = x_ref[...]
    xn = _rms(x, g_ref[...]).astype(BF16)
    acc = None
    for c in range(D_FF // FFN_CHUNK):
        lo, hi = c * FFN_CHUNK, (c + 1) * FFN_CHUNK
        gate = _dot(xn, wup_ref[:, lo:hi])
        up = _dot(xn, wup_ref[:, D_FF + lo:D_FF + hi])
        act = (gate * jax.nn.sigmoid(gate) * up).astype(BF16)
        part = _dot(act, wdn_ref[lo:hi, :])
        acc = part if acc is None else acc + part
    y = x + 0.5 * acc
    if final:
        y = _rms(y, fg_ref[...])
    o_ref[...] = y


def _ffn(x, g, wup_all, wdn_all, l, i, final_g=None):
    final = final_g is not None
    row = pl.BlockSpec((ROW_TILE, D_MODEL), lambda r: (r, 0))
    vec = pl.BlockSpec((1, D_MODEL), lambda r: (0, 0))
    in_specs = [row, vec,
                _resident((None, None, D_MODEL, 2 * D_FF), lambda r: (l, i, 0, 0)),
                _resident((None, None, D_FF, D_MODEL), lambda r: (l, i, 0, 0))]
    args = [x, g, wup_all, wdn_all]
    if final:
        in_specs.append(vec)
        args.append(final_g)
    return pl.pallas_call(
        functools.partial(_ffn_kernel, final=final),
        grid=(N_ROWS // ROW_TILE,),
        in_specs=in_specs,
        out_specs=row,
        out_shape=jax.ShapeDtypeStruct((N_ROWS, D_MODEL), F32),
        compiler_params=_params("parallel"),
        name="ffn_final" if final else "ffn",
    )(*args)


def _mlstm_proj_kernel(x_ref, g_ref, w_ref, wg_ref, p_ref, gate_ref):
    xn = _rms(x_ref[...], g_ref[...]).astype(BF16)
    p_ref[...] = _dot(xn, w_ref[...])
    gate_ref[...] = _dot(xn, wg_ref[...])


def _mlstm_proj(x, g, w_all, wg_all, j):
    row = lambda n: pl.BlockSpec((ROW_TILE, n), lambda r: (r, 0))
    return pl.pallas_call(
        _mlstm_proj_kernel,
        grid=(N_ROWS // ROW_TILE,),
        in_specs=[row(D_MODEL), pl.BlockSpec((1, D_MODEL), lambda r: (0, 0)),
                  _resident((None, D_MODEL, 4 * D_MODEL), lambda r: (j, 0, 0)),
                  _resident((None, D_MODEL, LANES), lambda r: (j, 0, 0))],
        out_specs=[row(4 * D_MODEL), row(LANES)],
        out_shape=[jax.ShapeDtypeStruct((N_ROWS, 4 * D_MODEL), F32),
                   jax.ShapeDtypeStruct((N_ROWS, LANES), F32)],
        compiler_params=_params("parallel"),
        name="mlstm_proj",
    )(x, g, w_all, wg_all)


def _gate_terms(gate_col, gate_row, bias_col, bias_row, length):
    r = lax.broadcasted_iota(jnp.int32, (length, length), 0)
    c = lax.broadcasted_iota(jnp.int32, (length, length), 1)
    mask = r >= c
    lower = mask.astype(F32)
    upper = (r <= c).astype(F32)
    zc = gate_col + bias_col
    lane = lax.broadcasted_iota(jnp.int32, zc.shape, 1)
    log_col = jnp.where(lane < N_HEADS, zc, _log_sigmoid(zc))
    cum_col = jnp.dot(lower, log_col, precision=lax.Precision.HIGHEST,
                      preferred_element_type=F32)
    zr = gate_row + bias_row
    sub = lax.broadcasted_iota(jnp.int32, zr.shape, 0)
    log_row = jnp.where(sub < N_HEADS, zr, _log_sigmoid(zr))
    cum_row = jnp.dot(log_row, upper, precision=lax.Precision.HIGHEST,
                      preferred_element_type=F32)
    return mask, log_col, cum_col, log_row, cum_row


def _mlstm_head(q, k, v, b_col, b_row, i_col, i_row, mask, c_state, n_state, m_state):
    length = q.shape[0]
    a = b_col + m_state
    dlog = jnp.where(mask, b_col - b_row + i_row, -jnp.inf)
    mt = jnp.maximum(a, jnp.max(dlog, axis=1, keepdims=True))
    dw = jnp.exp(dlog - mt)
    qb, kb, vb = q.astype(BF16), k.astype(BF16), v.astype(BF16)
    s = lax.dot_general(qb, kb, _NT, preferred_element_type=F32) * dw
    inter = jnp.exp(a - mt)
    num = inter * _dot(qb, c_state.astype(BF16)) + _dot(s.astype(BF16), vb)
    den = inter * jnp.sum(q * n_state, axis=1, keepdims=True) + jnp.sum(s, axis=1, keepdims=True)
    h = num / jnp.maximum(jnp.abs(den), jnp.exp(-mt))
    m_new = mt[length - 1:length, :]
    b_last = b_col[length - 1:length, :]
    decay = jnp.exp(b_last + m_state - m_new)
    kw = k * jnp.exp(b_last - b_col + i_col - m_new)
    c_new = decay * c_state + lax.dot_general(kw.astype(BF16), vb, _TN, preferred_element_type=F32)
    n_new = decay * n_state + jnp.sum(kw, axis=0, keepdims=True)
    return h, c_new, n_new, m_new


def _head_out(h, o, g_head):
    hn = h * lax.rsqrt(jnp.mean(h * h, axis=1, keepdims=True) + EPS) * g_head
    return jax.nn.sigmoid(o) * hn


def _head_slices(p, h):
    lo, hi = h * HEAD_DIM, (h + 1) * HEAD_DIM
    q = p[:, lo:hi]
    k = p[:, D_MODEL + lo:D_MODEL + hi] * (HEAD_DIM ** -0.5)
    v = p[:, 2 * D_MODEL + lo:2 * D_MODEL + hi]
    o = p[:, 3 * D_MODEL + lo:3 * D_MODEL + hi]
    return q, k, v, o


def _mlstm_prompt_kernel(p_ref, gc_ref, gr_ref, x_ref, bc_ref, br_ref, gh_ref, wo_ref,
                         xo_ref, co_ref, no_ref, mo_ref, c_sc, n_sc, m_sc, y_sc):
    chunk = pl.program_id(1)

    @pl.when(chunk == 0)
    def _():
        c_sc[...] = jnp.zeros_like(c_sc)
        n_sc[...] = jnp.zeros_like(n_sc)
        m_sc[...] = jnp.zeros_like(m_sc)

    mask, log_col, cum_col, log_row, cum_row = _gate_terms(
        gc_ref[...], gr_ref[...], bc_ref[...], br_ref[...], CHUNK)
    for h in range(N_HEADS):
        q, k, v, o = _head_slices(p_ref, h)
        f = N_HEADS + h
        hh, c_new, n_new, m_new = _mlstm_head(
            q, k, v, cum_col[:, f:f + 1], cum_row[f:f + 1, :], log_col[:, h:h + 1],
            log_row[h:h + 1, :], mask, c_sc[h], n_sc[h:h + 1, :], m_sc[h:h + 1, 0:1])
        c_sc[h] = c_new
        n_sc[h:h + 1, :] = n_new
        m_sc[h:h + 1, :] = jnp.broadcast_to(m_new, (1, LANES))
        lo, hi = h * HEAD_DIM, (h + 1) * HEAD_DIM
        y_sc[:, lo:hi] = _head_out(hh, o, gh_ref[:, lo:hi]).astype(BF16)
    xo_ref[...] = x_ref[...] + _dot(y_sc[...], wo_ref[...])

    @pl.when(chunk == pl.num_programs(1) - 1)
    def _():
        co_ref[...] = c_sc[...]
        no_ref[...] = n_sc[0:N_HEADS, :]
        mo_ref[...] = m_sc[...]


def _mlstm_prompt(x, p, gate, gate_t, bias_col, bias_row, g_head, wo_all, j):
    nc = SEQ // CHUNK
    tok = lambda n: pl.BlockSpec((CHUNK, n), lambda b, c: (b * nc + c, 0))
    const = lambda shape: pl.BlockSpec(shape, lambda b, c: (0,) * len(shape))
    return pl.pallas_call(
        _mlstm_prompt_kernel,
        grid=(BATCH, nc),
        in_specs=[tok(4 * D_MODEL), tok(LANES),
                  pl.BlockSpec((GATE_ROWS, CHUNK), lambda b, c: (0, b * nc + c)),
                  tok(D_MODEL), const((1, LANES)), const((GATE_ROWS, 1)), const((1, D_MODEL)),
                  _resident((None, D_MODEL, D_MODEL), lambda b, c: (j, 0, 0))],
        out_specs=[tok(D_MODEL),
                   pl.BlockSpec((None, N_HEADS, HEAD_DIM, HEAD_DIM), lambda b, c: (b, 0, 0, 0)),
                   pl.BlockSpec((None, N_HEADS, HEAD_DIM), lambda b, c: (b, 0, 0)),
                   pl.BlockSpec((None, GATE_ROWS, LANES), lambda b, c: (b, 0, 0))],
        out_shape=[jax.ShapeDtypeStruct((N_ROWS, D_MODEL), F32),
                   jax.ShapeDtypeStruct((BATCH, N_HEADS, HEAD_DIM, HEAD_DIM), F32),
                   jax.ShapeDtypeStruct((BATCH, N_HEADS, HEAD_DIM), F32),
                   jax.ShapeDtypeStruct((BATCH, GATE_ROWS, LANES), F32)],
        scratch_shapes=[pltpu.VMEM((N_HEADS, HEAD_DIM, HEAD_DIM), F32),
                        pltpu.VMEM((GATE_ROWS, HEAD_DIM), F32),
                        pltpu.VMEM((GATE_ROWS, LANES), F32),
                        pltpu.VMEM((CHUNK, D_MODEL), BF16)],
        input_output_aliases={3: 0},
        compiler_params=_params("arbitrary", "arbitrary"),
        name="mlstm_prompt",
    )(p, gate, gate_t, x, bias_col, bias_row, g_head, wo_all)


MLSTM_SAMPLE_BATCH = 4


def _mlstm_sample_kernel(p_ref, gc_ref, gr_ref, bc_ref, br_ref, gh_ref, ci_ref, ni_ref, mi_ref,
                         y_ref, co_ref, no_ref, mo_ref):
    def body(b, carry):
        mask, log_col, cum_col, log_row, cum_row = _gate_terms(
            gc_ref[b], gr_ref[b], bc_ref[...], br_ref[...], DEC_SEQ)
        mo_ref[b] = jnp.zeros((GATE_ROWS, LANES), F32)
        for h in range(N_HEADS):
            q, k, v, o = _head_slices(p_ref.at[b], h)
            f = N_HEADS + h
            hh, c_new, n_new, m_new = _mlstm_head(
                q, k, v, cum_col[:, f:f + 1], cum_row[f:f + 1, :], log_col[:, h:h + 1],
                log_row[h:h + 1, :], mask, ci_ref[b, h], ni_ref[b, h:h + 1, :],
                mi_ref[b, h:h + 1, 0:1])
            co_ref[b, h] = c_new
            no_ref[b, h:h + 1, :] = n_new
            mo_ref[b, h:h + 1, :] = jnp.broadcast_to(m_new, (1, LANES))
            lo, hi = h * HEAD_DIM, (h + 1) * HEAD_DIM
            y_ref[b, :, lo:hi] = _head_out(hh, o, gh_ref[:, lo:hi])
        return carry

    lax.fori_loop(0, MLSTM_SAMPLE_BATCH, body, 0)


def _mlstm_sample(p3, gate3, gate_t3, bias_col, bias_row, g_head, c_in, n_in, m_in):
    bs = MLSTM_SAMPLE_BATCH
    first = N_PROMPT // DEC_SEQ // bs
    const = lambda shape: pl.BlockSpec(shape, lambda i: (0,) * len(shape))
    state_specs = [pl.BlockSpec((bs, N_HEADS, HEAD_DIM, HEAD_DIM), lambda i: (i, 0, 0, 0)),
                   pl.BlockSpec((bs, N_HEADS, HEAD_DIM), lambda i: (i, 0, 0)),
                   pl.BlockSpec((bs, GATE_ROWS, LANES), lambda i: (i, 0, 0))]
    return pl.pallas_call(
        _mlstm_sample_kernel,
        grid=(DEC_BATCH // bs,),
        in_specs=[pl.BlockSpec((bs, DEC_SEQ, 4 * D_MODEL), lambda i: (first + i, 0, 0)),
                  pl.BlockSpec((bs, DEC_SEQ, LANES), lambda i: (first + i, 0, 0)),
                  pl.BlockSpec((bs, GATE_ROWS, DEC_SEQ), lambda i: (i, 0, 0)),
                  const((1, LANES)), const((GATE_ROWS, 1)), const((1, D_MODEL))] + state_specs,
        out_specs=[pl.BlockSpec((bs, DEC_SEQ, D_MODEL), lambda i: (i, 0, 0))] + state_specs,
        out_shape=[jax.ShapeDtypeStruct((DEC_BATCH, DEC_SEQ, D_MODEL), F32),
                   jax.ShapeDtypeStruct((DEC_BATCH, N_HEADS, HEAD_DIM, HEAD_DIM), F32),
                   jax.ShapeDtypeStruct((DEC_BATCH, N_HEADS, HEAD_DIM), F32),
                   jax.ShapeDtypeStruct((DEC_BATCH, GATE_ROWS, LANES), F32)],
        compiler_params=_params("parallel"),
        name="mlstm_sample",
    )(p3, gate3, gate_t3, bias_col, bias_row, g_head, c_in, n_in, m_in)


def _norm_matmul_kernel(x_ref, g_ref, w_ref, o_ref):
    o_ref[...] = _dot(_rms(x_ref[...], g_ref[...]).astype(BF16), w_ref[...])


def _norm_matmul_sample(x, g, w_all, l):
    first = N_PROMPT // ROW_TILE
    return pl.pallas_call(
        _norm_matmul_kernel,
        grid=(N_SAMPLE // ROW_TILE,),
        in_specs=[pl.BlockSpec((ROW_TILE, D_MODEL), lambda r: (first + r, 0)),
                  pl.BlockSpec((1, D_MODEL), lambda r: (0, 0)),
                  _resident((None, D_MODEL, D_MODEL), lambda r: (l, 0, 0))],
        out_specs=pl.BlockSpec((ROW_TILE, D_MODEL), lambda r: (r, 0)),
        out_shape=jax.ShapeDtypeStruct((N_SAMPLE, D_MODEL), F32),
        compiler_params=_params("parallel"),
        name="norm_matmul_sample",
    )(x, g, w_all)


def _matmul_residual_kernel(x_ref, y_ref, w_ref, o_ref):
    o_ref[...] = x_ref[...] + _dot(y_ref[...].astype(BF16), w_ref[...])


def _matmul_residual_sample(x, y, w_all, l):
    first = N_PROMPT // ROW_TILE
    xrow = pl.BlockSpec((ROW_TILE, D_MODEL), lambda r: (first + r, 0))
    return pl.pallas_call(
        _matmul_residual_kernel,
        grid=(N_SAMPLE // ROW_TILE,),
        in_specs=[xrow, pl.BlockSpec((ROW_TILE, D_MODEL), lambda r: (r, 0)),
                  _resident((None, D_MODEL, D_MODEL), lambda r: (l, 0, 0))],
        out_specs=xrow,
        out_shape=jax.ShapeDtypeStruct((N_ROWS, D_MODEL), F32),
        input_output_aliases={0: 0},
        compiler_params=_params("parallel"),
        name="matmul_residual_sample",
    )(x, y, w_all)


def _mem_kv_kernel(m_ref, w_ref, k_ref, v_ref):
    kv = _dot(m_ref[...].astype(BF16), w_ref[...])
    k_ref[...] = kv[:, :D_MODEL]
    v_ref[...] = kv[:, D_MODEL:]


def _mem_kv(mem, wkv_all):
    rows = BATCH * N_MEM
    out = pl.BlockSpec((None, ROW_TILE, D_MODEL), lambda l, r: (l, r, 0))
    return pl.pallas_call(
        _mem_kv_kernel,
        grid=(DEPTH, rows // ROW_TILE),
        in_specs=[pl.BlockSpec((ROW_TILE, D_MODEL), lambda l, r: (r, 0)),
                  pl.BlockSpec((None, D_MODEL, 2 * D_MODEL), lambda l, r: (l, 0, 0))],
        out_specs=[out, out],
        out_shape=[jax.ShapeDtypeStruct((DEPTH, rows, D_MODEL), F32)] * 2,
        compiler_params=_params("parallel", "parallel"),
        name="mem_kv",
    )(mem, wkv_all)


def _attend(q, k, v):
    s = lax.dot_general(q.astype(BF16), k.astype(BF16), _NT,
                        preferred_element_type=F32) * (XHEAD_DIM ** -0.5)
    e = jnp.exp(s - jnp.max(s, axis=1, keepdims=True))
    p = e / jnp.sum(e, axis=1, keepdims=True)
    return _dot(p.astype(BF16), v.astype(BF16))


def _xattn_prompt_kernel(x_ref, g_ref, wq_ref, k_ref, v_ref, wo_ref, o_ref, a_sc):
    x = x_ref[...]
    q = _dot(_rms(x, g_ref[...]).astype(BF16), wq_ref[...])
    for h in range(N_XHEADS):
        lo, hi = h * XHEAD_DIM, (h + 1) * XHEAD_DIM
        a_sc[:, lo:hi] = _attend(q[:, lo:hi], k_ref[:, lo:hi], v_ref[:, lo:hi]).astype(BF16)
    o_ref[...] = x + _dot(a_sc[...], wo_ref[...])


def _xattn_prompt(x, g, wq_all, k_all, v_all, wo_all, l):
    nt = SEQ // ROW_TILE
    xrow = pl.BlockSpec((ROW_TILE, D_MODEL), lambda b, t: (b * nt + t, 0))
    kv = pl.BlockSpec((None, N_MEM, D_MODEL), lambda b, t: (l, b, 0))
    w = _resident((None, D_MODEL, D_MODEL), lambda b, t: (l, 0, 0))
    return pl.pallas_call(
        _xattn_prompt_kernel,
        grid=(BATCH, nt),
        in_specs=[xrow, pl.BlockSpec((1, D_MODEL), lambda b, t: (0, 0)), w, kv, kv, w],
        out_specs=xrow,
        out_shape=jax.ShapeDtypeStruct((N_ROWS, D_MODEL), F32),
        scratch_shapes=[pltpu.VMEM((ROW_TILE, D_MODEL), BF16)],
        input_output_aliases={0: 0},
        compiler_params=_params("parallel", "parallel"),
        name="xattn_prompt",
    )(x, g, wq_all, k_all, v_all, wo_all)


XATTN_SAMPLE_BATCH = 8


def _xattn_sample_kernel(q_ref, k_ref, v_ref, o_ref):
    def body(b, carry):
        for h in range(N_XHEADS):
            lo, hi = h * XHEAD_DIM, (h + 1) * XHEAD_DIM
            o_ref[b, :, lo:hi] = _attend(q_ref[b, :, lo:hi], k_ref[b, :, lo:hi], v_ref[b, :, lo:hi])
        return carry

    lax.fori_loop(0, XATTN_SAMPLE_BATCH, body, 0)


def _xattn_sample(q3, k_cache, v_cache, l):
    bs = XATTN_SAMPLE_BATCH
    tok = pl.BlockSpec((bs, DEC_SEQ, D_MODEL), lambda i: (i, 0, 0))
    kv = pl.BlockSpec((None, bs, N_MEM, D_MODEL), lambda i: (l, i, 0, 0))
    return pl.pallas_call(
        _xattn_sample_kernel,
        grid=(DEC_BATCH // bs,),
        in_specs=[tok, kv, kv],
        out_specs=tok,
        out_shape=jax.ShapeDtypeStruct((DEC_BATCH, DEC_SEQ, D_MODEL), F32),
        compiler_params=_params("parallel"),
        name="xattn_sample",
    )(q3, k_cache, v_cache)


def _pool_prompt_kernel(x_ref, g_ref, win_ref, wgrp_ref, scale_ref, wout_ref,
                        o_ref, buf_ref, ext_sc, z_sc):
    tile = pl.program_id(1)

    @pl.when(tile == 0)
    def _():
        ext_sc[0:POOL_HALO, :] = jnp.zeros((POOL_HALO, D_MODEL), F32)

    x = x_ref[...]
    u = _dot(_rms(x, g_ref[...]).astype(BF16), win_ref[...])
    ext_sc[POOL_HALO:, :] = u
    pos = tile * ROW_TILE + lax.broadcasted_iota(jnp.int32, (ROW_TILE, 1), 0)
    for g, w in enumerate(POOL_WINDOWS):
        lo, hi = g * POOL_GROUP_DIM, (g + 1) * POOL_GROUP_DIM
        win = u[:, lo:hi]
        for back in range(1, w):
            win = win + ext_sc[POOL_HALO - back:POOL_HALO - back + ROW_TILE, lo:hi]
        cnt = jnp.minimum(pos + 1, w).astype(F32)
        d = win / cnt - u[:, lo:hi]
        z_sc[:, lo:hi] = (_dot(d.astype(BF16), wgrp_ref[g]) * scale_ref[:, lo:hi]).astype(BF16)
    o_ref[...] = x + _dot(z_sc[...], wout_ref[...])
    tail = ext_sc[ROW_TILE:ROW_TILE + POOL_HALO, :]
    ext_sc[0:POOL_HALO, :] = tail

    @pl.when(tile == pl.num_programs(1) - 1)
    def _():
        buf_ref[...] = tail


def _pool_prompt(x, g, win_all, wgrp_all, scale, wout_all, j):
    nt = SEQ // ROW_TILE
    xrow = pl.BlockSpec((ROW_TILE, D_MODEL), lambda b, t: (b * nt + t, 0))
    vec = pl.BlockSpec((1, D_MODEL), lambda b, t: (0, 0))
    w = _resident((None, D_MODEL, D_MODEL), lambda b, t: (j, 0, 0))
    wgrp = _resident((None, len(POOL_WINDOWS), POOL_GROUP_DIM, POOL_GROUP_DIM),
                     lambda b, t: (j, 0, 0, 0))
    return pl.pallas_call(
        _pool_prompt_kernel,
        grid=(BATCH, nt),
        in_specs=[xrow, vec, w, wgrp, vec, w],
        out_specs=[xrow, pl.BlockSpec((None, POOL_HALO, D_MODEL), lambda b, t: (b, 0, 0))],
        out_shape=[jax.ShapeDtypeStruct((N_ROWS, D_MODEL), F32),
                   jax.ShapeDtypeStruct((BATCH, POOL_HALO, D_MODEL), F32)],
        scratch_shapes=[pltpu.VMEM((POOL_HALO + ROW_TILE, D_MODEL), F32),
                        pltpu.VMEM((ROW_TILE, D_MODEL), BF16)],
        input_output_aliases={0: 0},
        compiler_params=_params("arbitrary", "arbitrary"),
        name="pool_prompt",
    )(x, g, win_all, wgrp_all, scale, wout_all)


def _pool_sample_kernel(x_ref, buf_ref, g_ref, win_ref, wgrp_ref, scale_ref, z_ref, nbuf_ref):
    u = _dot(_rms(x_ref[...], g_ref[...]).astype(BF16), win_ref[...])

    def ext(row, lo, hi):
        if row < POOL_BUF:
            return buf_ref[row, :, lo:hi]
        t = row - POOL_BUF
        return u[t * DEC_BATCH:(t + 1) * DEC_BATCH, lo:hi]

    for g, w in enumerate(POOL_WINDOWS):
        lo, hi = g * POOL_GROUP_DIM, (g + 1) * POOL_GROUP_DIM
        ds = []
        for t in range(DEC_SEQ):
            win = ext(POOL_BUF + t, lo, hi)
            for back in range(1, w):
                win = win + ext(POOL_BUF + t - back, lo, hi)
            cnt = float(min(PAST_LEN + 1 + t, w))
            ds.append(win / cnt - ext(POOL_BUF + t, lo, hi))
        d = jnp.concatenate(ds, axis=0).astype(BF16)
        z_ref[:, lo:hi] = _dot(d, wgrp_ref[g]) * scale_ref[:, lo:hi]
    for row in range(POOL_BUF):
        nbuf_ref[row] = ext(row + DEC_SEQ, 0, D_MODEL)


def _pool_sample(xs_t, buf_t, g, win_all, wgrp_all, scale, j):
    vec = pl.BlockSpec((1, D_MODEL), lambda i: (0, 0))
    return pl.pallas_call(
        _pool_sample_kernel,
        grid=(1,),
        in_specs=[pl.BlockSpec((N_SAMPLE, D_MODEL), lambda i: (0, 0)),
                  pl.BlockSpec((POOL_BUF, DEC_BATCH, D_MODEL), lambda i: (0, 0, 0)),
                  vec,
                  pl.BlockSpec((None, D_MODEL, D_MODEL), lambda i: (j, 0, 0)),
                  pl.BlockSpec((None, len(POOL_WINDOWS), POOL_GROUP_DIM, POOL_GROUP_DIM),
                               lambda i: (j, 0, 0, 0)),
                  vec],
        out_specs=[pl.BlockSpec((N_SAMPLE, D_MODEL), lambda i: (0, 0)),
                   pl.BlockSpec((POOL_BUF, DEC_BATCH, D_MODEL), lambda i: (0, 0, 0))],
        out_shape=[jax.ShapeDtypeStruct((N_SAMPLE, D_MODEL), F32),
                   jax.ShapeDtypeStruct((POOL_BUF, DEC_BATCH, D_MODEL), F32)],
        compiler_params=_params("arbitrary"),
        name="pool_sample",
    )(xs_t, buf_t, g, win_all, wgrp_all, scale)


def _swap_token_seq(a, lead):
    n = a.shape[0] // lead
    return a.reshape(lead, n, a.shape[1]).transpose(1, 0, 2).reshape(a.shape)


def kernel(x_prompt, x_sample, mem_prompt, cache_mem_k, cache_mem_v, state_mlstm_C, state_mlstm_n,
           state_mlstm_m, state_pool_buf, norm_g, final_g, ffn_w_up, ffn_w_down, mlstm_w_in,
           mlstm_b_i, mlstm_b_f, mlstm_g_head, mlstm_w_out, pool_w_in, pool_w_grp, pool_scale,
           pool_w_out, xattn_w_q, xattn_w_kv, xattn_w_o):
    n_mlstm = mlstm_w_in.shape[0]
    wup = ffn_w_up.astype(BF16)
    wdn = ffn_w_down.astype(BF16)
    w_in = mlstm_w_in[:, :, :4 * D_MODEL].astype(BF16)
    w_gate = jnp.pad(mlstm_w_in[:, :, 4 * D_MODEL:],
                     ((0, 0), (0, 0), (0, LANES - 2 * N_HEADS))).astype(BF16)
    w_mout = mlstm_w_out.astype(BF16)
    gate_bias = jnp.concatenate([mlstm_b_i, mlstm_b_f], axis=1).astype(F32)
    p_win = pool_w_in.astype(BF16)
    p_wgrp = pool_w_grp.astype(BF16)
    p_wout = pool_w_out.astype(BF16)
    wq = xattn_w_q.astype(BF16)
    wkv = xattn_w_kv.astype(BF16)
    wo = xattn_w_o.astype(BF16)

    x = jnp.concatenate([x_prompt.reshape(N_PROMPT, D_MODEL),
                         x_sample.reshape(N_SAMPLE, D_MODEL)], axis=0)
    mem_k, mem_v = _mem_kv(mem_prompt.reshape(BATCH * N_MEM, D_MODEL), wkv)
    k_cache = cache_mem_k.reshape(DEPTH, DEC_BATCH, N_MEM, D_MODEL)
    v_cache = cache_mem_v.reshape(DEPTH, DEC_BATCH, N_MEM, D_MODEL)

    c_p, n_p, m_p, c_s, n_s, m_s, buf_p, buf_s = [], [], [], [], [], [], [], []
    for l in range(DEPTH):
        g = norm_g[l].reshape(4, 1, D_MODEL)
        j = l // 2
        x = _ffn(x, g[0], wup, wdn, l, 0)
        if l % 2 == 0:
            p, gate = _mlstm_proj(x, g[1], w_in, w_gate, j)
            gates = gate[:, :GATE_ROWS]
            bias_col = jnp.pad(gate_bias[j], (0, LANES - GATE_ROWS)).reshape(1, LANES)
            bias_row = gate_bias[j].reshape(GATE_ROWS, 1)
            g_head = mlstm_g_head[j].reshape(1, D_MODEL)
            x, c1, n1, m1 = _mlstm_prompt(x, p, gate, gates.T, bias_col, bias_row, g_head, w_mout, j)
            gate_t3 = gates[N_PROMPT:].reshape(DEC_BATCH, DEC_SEQ, GATE_ROWS).transpose(0, 2, 1)
            m_in = jnp.broadcast_to(
                jnp.pad(state_mlstm_m[j], ((0, 0), (0, GATE_ROWS - N_HEADS)))[:, :, None],
                (DEC_BATCH, GATE_ROWS, LANES))
            y3, c2, n2, m2 = _mlstm_sample(
                p.reshape(N_ROWS // DEC_SEQ, DEC_SEQ, 4 * D_MODEL),
                gate.reshape(N_ROWS // DEC_SEQ, DEC_SEQ, LANES), gate_t3,
                bias_col, bias_row, g_head, state_mlstm_C[j], state_mlstm_n[j], m_in)
            x = _matmul_residual_sample(x, y3.reshape(N_SAMPLE, D_MODEL), w_mout, j)
            c_p.append(c1); n_p.append(n1); m_p.append(m1[:, :N_HEADS, 0])
            c_s.append(c2); n_s.append(n2); m_s.append(m2[:, :N_HEADS, 0])
        else:
            scale = pool_scale[j].reshape(1, D_MODEL)
            x, b1 = _pool_prompt(x, g[1], p_win, p_wgrp, scale, p_wout, j)
            xs_t = _swap_token_seq(x[N_PROMPT:], DEC_BATCH)
            z_t, b2 = _pool_sample(xs_t, state_pool_buf[j].transpose(1, 0, 2), g[1],
                                   p_win, p_wgrp, scale, j)
            x = _matmul_residual_sample(x, _swap_token_seq(z_t, DEC_SEQ), p_wout, j)
            buf_p.append(b1[:, POOL_HALO - POOL_BUF:])
            buf_s.append(b2.transpose(1, 0, 2))
        x = _xattn_prompt(x, g[2], wq, mem_k, mem_v, wo, l)
        q_s = _norm_matmul_sample(x, g[2], wq, l)
        a3 = _xattn_sample(q_s.reshape(DEC_BATCH, DEC_SEQ, D_MODEL), k_cache, v_cache, l)
        x = _matmul_residual_sample(x, a3.reshape(N_SAMPLE, D_MODEL), wo, l)
        x = _ffn(x, g[3], wup, wdn, l, 1,
                 final_g=final_g.reshape(1, D_MODEL) if l == DEPTH - 1 else None)

    xh_shape = (DEPTH, BATCH, N_MEM, N_XHEADS, XHEAD_DIM)
    return (x[:N_PROMPT].reshape(BATCH, SEQ, D_MODEL),
            x[N_PROMPT:].reshape(DEC_BATCH, DEC_SEQ, D_MODEL),
            mem_k.reshape(xh_shape), mem_v.reshape(xh_shape),
            jnp.stack(c_p), jnp.stack(n_p), jnp.stack(m_p),
            jnp.stack(c_s), jnp.stack(n_s), jnp.stack(m_s),
            jnp.stack(buf_p), jnp.stack(buf_s))
```

```python
import functools

import jax
import jax.numpy as jnp
from jax import lax
from jax.experimental import pallas as pl
from jax.experimental.pallas import tpu as pltpu

F32 = jnp.float32
BF16 = jnp.bfloat16

D_MODEL = 1024
BATCH = 8
SEQ = 2048
DEPTH = 4
DEC_BATCH = 128
DEC_SEQ = 4
PAST_LEN = 16384
N_HEADS = 4
HEAD_DIM = D_MODEL // N_HEADS
CHUNK = 128
POOL_WINDOWS = (2, 4, 8, 16)
POOL_GROUP_DIM = D_MODEL // len(POOL_WINDOWS)
POOL_BUF = max(POOL_WINDOWS) - 1
POOL_HALO = 16
N_MEM = 256
N_XHEADS = 4
XHEAD_DIM = D_MODEL // N_XHEADS
D_FF = ((8 * D_MODEL // 3 + 127) // 128) * 128
EPS = 1e-6

N_PROMPT = BATCH * SEQ
N_SAMPLE = DEC_BATCH * DEC_SEQ
N_ROWS = N_PROMPT + N_SAMPLE

LANES = 128
GATE_ROWS = 8
ROW_TILE = 512
PROMPT_TILES = N_PROMPT // ROW_TILE
assert N_SAMPLE == ROW_TILE
FFN_CHUNK = 1408
VMEM_LIMIT = 56 * 1024 * 1024

_NT = (((1,), (1,)), ((), ()))
_TN = (((0,), (0,)), ((), ()))


def _params(*sem):
    return pltpu.CompilerParams(dimension_semantics=sem, vmem_limit_bytes=VMEM_LIMIT)


def _resident(block_shape, index_map):
    return pl.BlockSpec(block_shape, index_map, pipeline_mode=pl.Buffered(1))


def _rms(x, g):
    return x * lax.rsqrt(jnp.mean(x * x, axis=-1, keepdims=True) + EPS) * g


def _log_sigmoid(x):
    return jnp.minimum(x, 0.0) - jnp.log1p(jnp.exp(-jnp.abs(x)))


def _dot(a, b):
    return jnp.dot(a, b, preferred_element_type=F32)


def _ffn_kernel(*refs, first, final):
    is_prompt = pl.program_id(0) < PROMPT_TILES
    if first:
        xp_ref, xs_ref, *refs = refs
        x = jnp.where(is_prompt, xp_ref[...], xs_ref[...])
    else:
        x_ref, *refs = refs
        x = x_ref[...]
    g_ref, wup_ref, wdn_ref, *rest = refs
    xn = _rms(x, g_ref[...]).astype(BF16)
    acc = None
    for c in range(D_FF // FFN_CHUNK):
        lo, hi = c * FFN_CHUNK, (c + 1) * FFN_CHUNK
        gate = _dot(xn, wup_ref[:, lo:hi])
        up = _dot(xn, wup_ref[:, D_FF + lo:D_FF + hi])
        act = (gate * jax.nn.sigmoid(gate) * up).astype(BF16)
        part = _dot(act, wdn_ref[lo:hi, :])
        acc = part if acc is None else acc + part
    y = x + 0.5 * acc
    if not final:
        (o_ref,) = rest
        o_ref[...] = y
        return
    fg_ref, op_ref, os_ref = rest
    y = _rms(y, fg_ref[...])

    @pl.when(is_prompt)
    def _():
        op_ref[...] = y

    @pl.when(jnp.logical_not(is_prompt))
    def _():
        os_ref[...] = y


def _ffn(xs, g, wup_all, wdn_all, l, i, final_g=None):
    first, final = len(xs) == 2, final_g is not None
    row = pl.BlockSpec((ROW_TILE, D_MODEL), lambda r: (r, 0))
    prompt_row = pl.BlockSpec((ROW_TILE, D_MODEL), lambda r: (jnp.minimum(r, PROMPT_TILES - 1), 0))
    sample_row = pl.BlockSpec((ROW_TILE, D_MODEL), lambda r: (0, 0))
    vec = pl.BlockSpec((1, D_MODEL), lambda r: (0, 0))
    in_specs = ([prompt_row, sample_row] if first else [row]) + [
        vec,
        _resident((None, None, D_MODEL, 2 * D_FF), lambda r: (l, i, 0, 0)),
        _resident((None, None, D_FF, D_MODEL), lambda r: (l, i, 0, 0))]
    args = list(xs) + [g, wup_all, wdn_all]
    if final:
        in_specs.append(vec)
        args.append(final_g)
        out_specs = [prompt_row, sample_row]
        out_shape = [jax.ShapeDtypeStruct((N_PROMPT, D_MODEL), F32),
                     jax.ShapeDtypeStruct((N_SAMPLE, D_MODEL), F32)]
    else:
        out_specs = row
        out_shape = jax.ShapeDtypeStruct((N_ROWS, D_MODEL), F32)
    return pl.pallas_call(
        functools.partial(_ffn_kernel, first=first, final=final),
        grid=(N_ROWS // ROW_TILE,),
        in_specs=in_specs,
        out_specs=out_specs,
        out_shape=out_shape,
        compiler_params=_params("arbitrary"),
        name="ffn_final" if final else ("ffn_first" if first else "ffn"),
    )(*args)


def _mlstm_proj_kernel(x_ref, g_ref, w_ref, wg_ref, p_ref, gate_ref):
    xn = _rms(x_ref[...], g_ref[...]).astype(BF16)
    p_ref[...] = _dot(xn, w_ref[...])
    gate_ref[...] = _dot(xn, wg_ref[...])


def _mlstm_proj(x, g, w_all, wg_all, j):
    row = lambda n: pl.BlockSpec((ROW_TILE, n), lambda r: (r, 0))
    return pl.pallas_call(
        _mlstm_proj_kernel,
        grid=(N_ROWS // ROW_TILE,),
        in_specs=[row(D_MODEL), pl.BlockSpec((1, D_MODEL), lambda r: (0, 0)),
                  _resident((None, D_MODEL, 4 * D_MODEL), lambda r: (j, 0, 0)),
                  _resident((None, D_MODEL, LANES), lambda r: (j, 0, 0))],
        out_specs=[row(4 * D_MODEL), row(LANES)],
        out_shape=[jax.ShapeDtypeStruct((N_ROWS, 4 * D_MODEL), F32),
                   jax.ShapeDtypeStruct((N_ROWS, LANES), F32)],
        compiler_params=_params("parallel"),
        name="mlstm_proj",
    )(x, g, w_all, wg_all)


def _gate_terms(gate_col, gate_row, bias_col, bias_row, length):
    r = lax.broadcasted_iota(jnp.int32, (length, length), 0)
    c = lax.broadcasted_iota(jnp.int32, (length, length), 1)
    mask = r >= c
    lower = mask.astype(F32)
    upper = (r <= c).astype(F32)
    zc = gate_col + bias_col
    lane = lax.broadcasted_iota(jnp.int32, zc.shape, 1)
    log_col = jnp.where(lane < N_HEADS, zc, _log_sigmoid(zc))
    cum_col = jnp.dot(lower, log_col, precision=lax.Precision.HIGHEST,
                      preferred_element_type=F32)
    zr = gate_row + bias_row
    sub = lax.broadcasted_iota(jnp.int32, zr.shape, 0)
    log_row = jnp.where(sub < N_HEADS, zr, _log_sigmoid(zr))
    cum_row = jnp.dot(log_row, upper, precision=lax.Precision.HIGHEST,
                      preferred_element_type=F32)
    return mask, log_col, cum_col, log_row, cum_row


def _mlstm_head(q, k, v, b_col, b_row, i_col, i_row, mask, c_state, n_state, m_state):
    length = q.shape[0]
    a = b_col + m_state
    dlog = jnp.where(mask, b_col - b_row + i_row, -jnp.inf)
    mt = jnp.maximum(a, jnp.max(dlog, axis=1, keepdims=True))
    dw = jnp.exp(dlog - mt)
    qb, kb, vb = q.astype(BF16), k.astype(BF16), v.astype(BF16)
    s = lax.dot_general(qb, kb, _NT, preferred_element_type=F32) * dw
    inter = jnp.exp(a - mt)
    num = inter * _dot(qb, c_state.astype(BF16)) + _dot(s.astype(BF16), vb)
    den = inter * jnp.sum(q * n_state, axis=1, keepdims=True) + jnp.sum(s, axis=1, keepdims=True)
    h = num / jnp.maximum(jnp.abs(den), jnp.exp(-mt))
    m_new = mt[length - 1:length, :]
    b_last = b_col[length - 1:length, :]
    decay = jnp.exp(b_last + m_state - m_new)
    kw = k * jnp.exp(b_last - b_col + i_col - m_new)
    c_new = decay * c_state + lax.dot_general(kw.astype(BF16), vb, _TN, preferred_element_type=F32)
    n_new = decay * n_state + jnp.sum(kw, axis=0, keepdims=True)
    return h, c_new, n_new, m_new


def _head_out(h, o, g_head):
    hn = h * lax.rsqrt(jnp.mean(h * h, axis=1, keepdims=True) + EPS) * g_head
    return jax.nn.sigmoid(o) * hn


def _head_slices(p, h):
    lo, hi = h * HEAD_DIM, (h + 1) * HEAD_DIM
    q = p[:, lo:hi]
    k = p[:, D_MODEL + lo:D_MODEL + hi] * (HEAD_DIM ** -0.5)
    v = p[:, 2 * D_MODEL + lo:2 * D_MODEL + hi]
    o = p[:, 3 * D_MODEL + lo:3 * D_MODEL + hi]
    return q, k, v, o


def _own_layer(ref, layer, first):
    if not first:
        return ref
    for other in range(ref.shape[0]):
        if other != layer:
            ref[other] = jnp.zeros(ref.shape[1:], ref.dtype)
    return ref.at[layer]


def _mlstm_prompt_kernel(p_ref, gc_ref, gr_ref, x_ref, bc_ref, br_ref, gh_ref, wo_ref, *rest,
                         layer, first):
    xo_ref, co_ref, no_ref, mo_ref, c_sc, n_sc, m_sc, y_sc = rest[-8:]
    chunk = pl.program_id(1)

    @pl.when(chunk == 0)
    def _():
        c_sc[...] = jnp.zeros_like(c_sc)
        n_sc[...] = jnp.zeros_like(n_sc)
        m_sc[...] = jnp.zeros_like(m_sc)

    mask, log_col, cum_col, log_row, cum_row = _gate_terms(
        gc_ref[...], gr_ref[...], bc_ref[...], br_ref[...], CHUNK)
    for h in range(N_HEADS):
        q, k, v, o = _head_slices(p_ref, h)
        f = N_HEADS + h
        hh, c_new, n_new, m_new = _mlstm_head(
            q, k, v, cum_col[:, f:f + 1], cum_row[f:f + 1, :], log_col[:, h:h + 1],
            log_row[h:h + 1, :], mask, c_sc[h], n_sc[h:h + 1, :], m_sc[h:h + 1, 0:1])
        c_sc[h] = c_new
        n_sc[h:h + 1, :] = n_new
        m_sc[h:h + 1, :] = jnp.broadcast_to(m_new, (1, LANES))
        lo, hi = h * HEAD_DIM, (h + 1) * HEAD_DIM
        y_sc[:, lo:hi] = _head_out(hh, o, gh_ref[:, lo:hi]).astype(BF16)
    xo_ref[...] = x_ref[...] + _dot(y_sc[...], wo_ref[...])

    @pl.when(chunk == pl.num_programs(1) - 1)
    def _():
        _own_layer(co_ref, layer, first)[...] = c_sc[...]
        no_ref[...] = n_sc[0:N_HEADS, :]
        mo_ref[...] = m_sc[...]


def _mlstm_prompt(x, p, gate, gate_t, bias_col, bias_row, g_head, wo_all, j, n_layers, c_prev):
    first = c_prev is None
    state_block = (N_HEADS, HEAD_DIM, HEAD_DIM)
    if first:
        c_spec = pl.BlockSpec((n_layers, None) + state_block, lambda b, c: (0, b, 0, 0, 0))
    else:
        c_spec = pl.BlockSpec((None, None) + state_block, lambda b, c: (j, b, 0, 0, 0))
    nc = SEQ // CHUNK
    tok = lambda n: pl.BlockSpec((CHUNK, n), lambda b, c: (b * nc + c, 0))
    const = lambda shape: pl.BlockSpec(shape, lambda b, c: (0,) * len(shape))
    in_specs = [tok(4 * D_MODEL), tok(LANES),
                pl.BlockSpec((GATE_ROWS, CHUNK), lambda b, c: (0, b * nc + c)),
                tok(D_MODEL), const((1, LANES)), const((GATE_ROWS, 1)), const((1, D_MODEL)),
                _resident((None, D_MODEL, D_MODEL), lambda b, c: (j, 0, 0))]
    args = [p, gate, gate_t, x, bias_col, bias_row, g_head, wo_all]
    aliases = {3: 0}
    if c_prev is not None:
        in_specs.append(pl.BlockSpec(memory_space=pl.ANY))
        args.append(c_prev)
        aliases[len(args) - 1] = 1
    return pl.pallas_call(
        functools.partial(_mlstm_prompt_kernel, layer=j, first=first),
        grid=(BATCH, nc),
        in_specs=in_specs,
        out_specs=[tok(D_MODEL), c_spec,
                   pl.BlockSpec((None, N_HEADS, HEAD_DIM), lambda b, c: (b, 0, 0)),
                   pl.BlockSpec((None, GATE_ROWS, LANES), lambda b, c: (b, 0, 0))],
        out_shape=[jax.ShapeDtypeStruct((N_ROWS, D_MODEL), F32),
                   jax.ShapeDtypeStruct((n_layers, BATCH, N_HEADS, HEAD_DIM, HEAD_DIM), F32),
                   jax.ShapeDtypeStruct((BATCH, N_HEADS, HEAD_DIM), F32),
                   jax.ShapeDtypeStruct((BATCH, GATE_ROWS, LANES), F32)],
        scratch_shapes=[pltpu.VMEM((N_HEADS, HEAD_DIM, HEAD_DIM), F32),
                        pltpu.VMEM((GATE_ROWS, HEAD_DIM), F32),
                        pltpu.VMEM((GATE_ROWS, LANES), F32),
                        pltpu.VMEM((CHUNK, D_MODEL), BF16)],
        input_output_aliases=aliases,
        compiler_params=_params("arbitrary", "arbitrary"),
        name="mlstm_prompt",
    )(*args)


MLSTM_SAMPLE_BATCH = 4


def _mlstm_sample_kernel(p_ref, gc_ref, gr_ref, bc_ref, br_ref, gh_ref, ci_ref, ni_ref, mi_ref, *rest,
                         layer, first):
    y_ref, co_ref, no_ref, mo_ref = rest[-4:]
    co_ref = _own_layer(co_ref, layer, first)

    def body(b, carry):
        mask, log_col, cum_col, log_row, cum_row = _gate_terms(
            gc_ref[b], gr_ref[b], bc_ref[...], br_ref[...], DEC_SEQ)
        mo_ref[b] = jnp.zeros((GATE_ROWS, LANES), F32)
        for h in range(N_HEADS):
            q, k, v, o = _head_slices(p_ref.at[b], h)
            f = N_HEADS + h
            hh, c_new, n_new, m_new = _mlstm_head(
                q, k, v, cum_col[:, f:f + 1], cum_row[f:f + 1, :], log_col[:, h:h + 1],
                log_row[h:h + 1, :], mask, ci_ref[b, h], ni_ref[b, h:h + 1, :],
                mi_ref[b, h:h + 1, 0:1])
            co_ref[b, h] = c_new
            no_ref[b, h:h + 1, :] = n_new
            mo_ref[b, h:h + 1, :] = jnp.broadcast_to(m_new, (1, LANES))
            lo, hi = h * HEAD_DIM, (h + 1) * HEAD_DIM
            y_ref[b, :, lo:hi] = _head_out(hh, o, gh_ref[:, lo:hi])
        return carry

    lax.fori_loop(0, MLSTM_SAMPLE_BATCH, body, 0)


def _mlstm_sample(p3, gate3, gate_t3, bias_col, bias_row, g_head, c_all, n_in, m_in, j, c_prev):
    bs = MLSTM_SAMPLE_BATCH
    first = c_prev is None
    const = lambda shape: pl.BlockSpec(shape, lambda i: (0,) * len(shape))
    tok = lambda n: pl.BlockSpec((bs, DEC_SEQ, n), lambda i: (i, 0, 0))
    state_block = (bs, N_HEADS, HEAD_DIM, HEAD_DIM)
    c_spec = pl.BlockSpec((None,) + state_block, lambda i: (j, i, 0, 0, 0))
    c_out = pl.BlockSpec((c_all.shape[0],) + state_block, lambda i: (0, i, 0, 0, 0)) if first else c_spec
    n_spec = pl.BlockSpec((bs, N_HEADS, HEAD_DIM), lambda i: (i, 0, 0))
    m_spec = pl.BlockSpec((bs, GATE_ROWS, LANES), lambda i: (i, 0, 0))
    in_specs = [tok(4 * D_MODEL), tok(LANES),
                pl.BlockSpec((bs, GATE_ROWS, DEC_SEQ), lambda i: (i, 0, 0)),
                const((1, LANES)), const((GATE_ROWS, 1)), const((1, D_MODEL)),
                c_spec, n_spec, m_spec]
    args = [p3, gate3, gate_t3, bias_col, bias_row, g_head, c_all, n_in, m_in]
    aliases = {}
    if c_prev is not None:
        in_specs.append(pl.BlockSpec(memory_space=pl.ANY))
        args.append(c_prev)
        aliases[len(args) - 1] = 1
    return pl.pallas_call(
        functools.partial(_mlstm_sample_kernel, layer=j, first=first),
        grid=(DEC_BATCH // bs,),
        in_specs=in_specs,
        out_specs=[tok(D_MODEL), c_out, n_spec, m_spec],
        out_shape=[jax.ShapeDtypeStruct((DEC_BATCH, DEC_SEQ, D_MODEL), F32),
                   jax.ShapeDtypeStruct(c_all.shape, F32),
                   jax.ShapeDtypeStruct((DEC_BATCH, N_HEADS, HEAD_DIM), F32),
                   jax.ShapeDtypeStruct((DEC_BATCH, GATE_ROWS, LANES), F32)],
        input_output_aliases=aliases,
        compiler_params=_params("arbitrary"),
        name="mlstm_sample",
    )(*args)


def _norm_matmul_kernel(x_ref, g_ref, w_ref, o_ref):
    o_ref[...] = _dot(_rms(x_ref[...], g_ref[...]).astype(BF16), w_ref[...])


def _norm_matmul_sample(x, g, w_all, l):
    first = N_PROMPT // ROW_TILE
    return pl.pallas_call(
        _norm_matmul_kernel,
        grid=(N_SAMPLE // ROW_TILE,),
        in_specs=[pl.BlockSpec((ROW_TILE, D_MODEL), lambda r: (first + r, 0)),
                  pl.BlockSpec((1, D_MODEL), lambda r: (0, 0)),
                  _resident((None, D_MODEL, D_MODEL), lambda r: (l, 0, 0))],
        out_specs=pl.BlockSpec((ROW_TILE, D_MODEL), lambda r: (r, 0)),
        out_shape=jax.ShapeDtypeStruct((N_SAMPLE, D_MODEL), F32),
        compiler_params=_params("parallel"),
        name="norm_matmul_sample",
    )(x, g, w_all)


def _matmul_residual_kernel(x_ref, y_ref, w_ref, o_ref):
    o_ref[...] = x_ref[...] + _dot(y_ref[...].astype(BF16), w_ref[...])


def _matmul_residual_sample(x, y, w_all, l):
    first = N_PROMPT // ROW_TILE
    xrow = pl.BlockSpec((ROW_TILE, D_MODEL), lambda r: (first + r, 0))
    return pl.pallas_call(
        _matmul_residual_kernel,
        grid=(N_SAMPLE // ROW_TILE,),
        in_specs=[xrow, pl.BlockSpec((ROW_TILE, D_MODEL), lambda r: (r, 0)),
                  _resident((None, D_MODEL, D_MODEL), lambda r: (l, 0, 0))],
        out_specs=xrow,
        out_shape=jax.ShapeDtypeStruct((N_ROWS, D_MODEL), F32),
        input_output_aliases={0: 0},
        compiler_params=_params("parallel"),
        name="matmul_residual_sample",
    )(x, y, w_all)


MEM_KV_BATCH = 2


def _mem_kv_kernel(m_ref, w_ref, k_ref, v_ref, kb_ref, vb_ref):
    kv = _dot(m_ref[...].astype(BF16), w_ref[...])
    kb_ref[...] = kv[:, :D_MODEL].astype(BF16)
    vb_ref[...] = kv[:, D_MODEL:].astype(BF16)
    for b in range(MEM_KV_BATCH):
        rows = slice(b * N_MEM, (b + 1) * N_MEM)
        for h in range(N_XHEADS):
            lo, hi = h * XHEAD_DIM, (h + 1) * XHEAD_DIM
            k_ref[b, :, h, :] = kv[rows, lo:hi]
            v_ref[b, :, h, :] = kv[rows, D_MODEL + lo:D_MODEL + hi]


def _mem_kv(mem, wkv_all):
    bs = MEM_KV_BATCH
    out = pl.BlockSpec((None, bs, N_MEM, N_XHEADS, XHEAD_DIM), lambda l, r: (l, r, 0, 0, 0))
    shape = jax.ShapeDtypeStruct((DEPTH, BATCH, N_MEM, N_XHEADS, XHEAD_DIM), F32)
    out_b = pl.BlockSpec((None, bs * N_MEM, D_MODEL), lambda l, r: (l, r, 0))
    shape_b = jax.ShapeDtypeStruct((DEPTH, BATCH * N_MEM, D_MODEL), BF16)
    return pl.pallas_call(
        _mem_kv_kernel,
        grid=(DEPTH, BATCH // bs),
        in_specs=[pl.BlockSpec((bs * N_MEM, D_MODEL), lambda l, r: (r, 0)),
                  pl.BlockSpec((None, D_MODEL, 2 * D_MODEL), lambda l, r: (l, 0, 0))],
        out_specs=[out, out, out_b, out_b],
        out_shape=[shape, shape, shape_b, shape_b],
        compiler_params=_params("arbitrary", "arbitrary"),
        name="mem_kv",
    )(mem, wkv_all)


def _softmax_rows(s):
    e = jnp.exp(s - jnp.max(s, axis=1, keepdims=True))
    return e / jnp.sum(e, axis=1, keepdims=True)


def _xattn_prompt_kernel(x_ref, g_ref, wq_ref, k_ref, v_ref, wo_ref, o_ref, a_sc):
    x = x_ref[...]
    q = _dot(_rms(x, g_ref[...]).astype(BF16), wq_ref[...])
    for h in range(N_XHEADS):
        lo, hi = h * XHEAD_DIM, (h + 1) * XHEAD_DIM
        s = lax.dot_general(q[:, lo:hi].astype(BF16), k_ref[:, lo:hi], _NT,
                            preferred_element_type=F32) * (XHEAD_DIM ** -0.5)
        a_sc[:, lo:hi] = _dot(_softmax_rows(s).astype(BF16), v_ref[:, lo:hi]).astype(BF16)
    o_ref[...] = x + _dot(a_sc[...], wo_ref[...])


def _xattn_prompt(x, g, wq_all, k_all, v_all, wo_all, l):
    nt = SEQ // ROW_TILE
    xrow = pl.BlockSpec((ROW_TILE, D_MODEL), lambda b, t: (b * nt + t, 0))
    kv = pl.BlockSpec((None, N_MEM, D_MODEL), lambda b, t: (l, b, 0))
    w = _resident((None, D_MODEL, D_MODEL), lambda b, t: (l, 0, 0))
    return pl.pallas_call(
        _xattn_prompt_kernel,
        grid=(BATCH, nt),
        in_specs=[xrow, pl.BlockSpec((1, D_MODEL), lambda b, t: (0, 0)), w, kv, kv, w],
        out_specs=xrow,
        out_shape=jax.ShapeDtypeStruct((N_ROWS, D_MODEL), F32),
        scratch_shapes=[pltpu.VMEM((ROW_TILE, D_MODEL), BF16)],
        input_output_aliases={0: 0},
        compiler_params=_params("parallel", "parallel"),
        name="xattn_prompt",
    )(x, g, wq_all, k_all, v_all, wo_all)


XATTN_SAMPLE_BATCH = 4


def _xattn_sample_kernel(q_ref, k_ref, v_ref, o_ref):
    n_q, n_kv = N_XHEADS * DEC_SEQ, N_MEM * N_XHEADS
    q_head = lax.broadcasted_iota(jnp.int32, (n_q, n_kv), 0) // DEC_SEQ
    kv_head = lax.broadcasted_iota(jnp.int32, (n_q, n_kv), 1) % N_XHEADS
    same_head = q_head == kv_head

    def body(b, carry):
        k = k_ref[b].reshape(n_kv, XHEAD_DIM).astype(BF16)
        v = v_ref[b].reshape(n_kv, XHEAD_DIM).astype(BF16)
        s = lax.dot_general(q_ref[b].astype(BF16), k, _NT,
                            preferred_element_type=F32) * (XHEAD_DIM ** -0.5)
        p = _softmax_rows(jnp.where(same_head, s, -jnp.inf))
        o_ref[b] = _dot(p.astype(BF16), v)
        return carry

    lax.fori_loop(0, XATTN_SAMPLE_BATCH, body, 0)


def _xattn_sample(q, k_cache, v_cache, l):
    bs = XATTN_SAMPLE_BATCH
    tok = pl.BlockSpec((bs, N_XHEADS * DEC_SEQ, XHEAD_DIM), lambda i: (i, 0, 0))
    kv = pl.BlockSpec((None, bs, N_MEM, N_XHEADS, XHEAD_DIM), lambda i: (l, i, 0, 0, 0))
    return pl.pallas_call(
        _xattn_sample_kernel,
        grid=(DEC_BATCH // bs,),
        in_specs=[tok, kv, kv],
        out_specs=tok,
        out_shape=jax.ShapeDtypeStruct((DEC_BATCH, N_XHEADS * DEC_SEQ, XHEAD_DIM), F32),
        compiler_params=_params("parallel"),
        name="xattn_sample",
    )(q, k_cache, v_cache)


def _heads_major(a):
    a = a.reshape(DEC_BATCH, DEC_SEQ, N_XHEADS, XHEAD_DIM)
    return a.transpose(0, 2, 1, 3).reshape(DEC_BATCH, N_XHEADS * DEC_SEQ, XHEAD_DIM)


def _tokens_major(a):
    a = a.reshape(DEC_BATCH, N_XHEADS, DEC_SEQ, XHEAD_DIM)
    return a.transpose(0, 2, 1, 3).reshape(N_SAMPLE, D_MODEL)


def _pool_prompt_kernel(x_ref, g_ref, win_ref, wgrp_ref, scale_ref, wout_ref,
                        o_ref, buf_ref, ext_sc, z_sc):
    tile = pl.program_id(1)

    @pl.when(tile == 0)
    def _():
        ext_sc[0:POOL_HALO, :] = jnp.zeros((POOL_HALO, D_MODEL), F32)

    x = x_ref[...]
    u = _dot(_rms(x, g_ref[...]).astype(BF16), win_ref[...])
    ext_sc[POOL_HALO:, :] = u
    pos = tile * ROW_TILE + lax.broadcasted_iota(jnp.int32, (ROW_TILE, 1), 0)
    for g, w in enumerate(POOL_WINDOWS):
        lo, hi = g * POOL_GROUP_DIM, (g + 1) * POOL_GROUP_DIM
        win = u[:, lo:hi]
        for back in range(1, w):
            win = win + ext_sc[POOL_HALO - back:POOL_HALO - back + ROW_TILE, lo:hi]
        cnt = jnp.minimum(pos + 1, w).astype(F32)
        d = win / cnt - u[:, lo:hi]
        z_sc[:, lo:hi] = (_dot(d.astype(BF16), wgrp_ref[g]) * scale_ref[:, lo:hi]).astype(BF16)
    o_ref[...] = x + _dot(z_sc[...], wout_ref[...])
    tail = ext_sc[ROW_TILE:ROW_TILE + POOL_HALO, :]
    ext_sc[0:POOL_HALO, :] = tail

    @pl.when(tile == pl.num_programs(1) - 1)
    def _():
        buf_ref[...] = tail


def _pool_prompt(x, g, win_all, wgrp_all, scale, wout_all, j):
    nt = SEQ // ROW_TILE
    xrow = pl.BlockSpec((ROW_TILE, D_MODEL), lambda b, t: (b * nt + t, 0))
    vec = pl.BlockSpec((1, D_MODEL), lambda b, t: (0, 0))
    w = _resident((None, D_MODEL, D_MODEL), lambda b, t: (j, 0, 0))
    wgrp = _resident((None, len(POOL_WINDOWS), POOL_GROUP_DIM, POOL_GROUP_DIM),
                     lambda b, t: (j, 0, 0, 0))
    return pl.pallas_call(
        _pool_prompt_kernel,
        grid=(BATCH, nt),
        in_specs=[xrow, vec, w, wgrp, vec, w],
        out_specs=[xrow, pl.BlockSpec((None, POOL_HALO, D_MODEL), lambda b, t: (b, 0, 0))],
        out_shape=[jax.ShapeDtypeStruct((N_ROWS, D_MODEL), F32),
                   jax.ShapeDtypeStruct((BATCH, POOL_HALO, D_MODEL), F32)],
        scratch_shapes=[pltpu.VMEM((POOL_HALO + ROW_TILE, D_MODEL), F32),
                        pltpu.VMEM((ROW_TILE, D_MODEL), BF16)],
        input_output_aliases={0: 0},
        compiler_params=_params("arbitrary", "arbitrary"),
        name="pool_prompt",
    )(x, g, win_all, wgrp_all, scale, wout_all)


def _pool_sample_kernel(x_ref, buf_ref, g_ref, win_ref, wgrp_ref, scale_ref, *rest, layer, first):
    z_ref, nbuf_ref = rest[-2:]
    nbuf_ref = _own_layer(nbuf_ref, layer, first)
    u = _dot(_rms(x_ref[...], g_ref[...]).astype(BF16), win_ref[...])

    def ext(row, lo, hi):
        if row < POOL_BUF:
            return buf_ref[:, row, lo:hi]
        t = row - POOL_BUF
        return u[t * DEC_BATCH:(t + 1) * DEC_BATCH, lo:hi]

    for g, w in enumerate(POOL_WINDOWS):
        lo, hi = g * POOL_GROUP_DIM, (g + 1) * POOL_GROUP_DIM
        ds = []
        for t in range(DEC_SEQ):
            win = ext(POOL_BUF + t, lo, hi)
            for back in range(1, w):
                win = win + ext(POOL_BUF + t - back, lo, hi)
            cnt = float(min(PAST_LEN + 1 + t, w))
            ds.append(win / cnt - ext(POOL_BUF + t, lo, hi))
        d = jnp.concatenate(ds, axis=0).astype(BF16)
        z_ref[:, lo:hi] = _dot(d, wgrp_ref[g]) * scale_ref[:, lo:hi]
    for row in range(POOL_BUF):
        nbuf_ref[:, row, :] = ext(row + DEC_SEQ, 0, D_MODEL)


def _pool_sample(xs_t, buf_all, g, win_all, wgrp_all, scale, j, buf_prev):
    first = buf_prev is None
    buf_block = (DEC_BATCH, POOL_BUF, D_MODEL)
    once = lambda shape, imap: pl.BlockSpec(shape, imap, pipeline_mode=pl.Buffered(1))
    if first:
        nbuf = once((buf_all.shape[0],) + buf_block, lambda i: (0, 0, 0, 0))
    else:
        nbuf = once((None,) + buf_block, lambda i: (j, 0, 0, 0))
    vec = pl.BlockSpec((1, D_MODEL), lambda i: (0, 0))
    buf = once((None, DEC_BATCH, POOL_BUF, D_MODEL), lambda i: (j, 0, 0, 0))
    in_specs = [pl.BlockSpec((N_SAMPLE, D_MODEL), lambda i: (0, 0)), buf, vec,
                once((None, D_MODEL, D_MODEL), lambda i: (j, 0, 0)),
                once((None, len(POOL_WINDOWS), POOL_GROUP_DIM, POOL_GROUP_DIM), lambda i: (j, 0, 0, 0)),
                vec]
    args = [xs_t, buf_all, g, win_all, wgrp_all, scale]
    aliases = {}
    if buf_prev is not None:
        in_specs.append(pl.BlockSpec(memory_space=pl.ANY))
        args.append(buf_prev)
        aliases[len(args) - 1] = 1
    return pl.pallas_call(
        functools.partial(_pool_sample_kernel, layer=j, first=first),
        grid=(1,),
        in_specs=in_specs,
        out_specs=[pl.BlockSpec((N_SAMPLE, D_MODEL), lambda i: (0, 0)), nbuf],
        out_shape=[jax.ShapeDtypeStruct((N_SAMPLE, D_MODEL), F32),
                   jax.ShapeDtypeStruct(buf_all.shape, F32)],
        input_output_aliases=aliases,
        compiler_params=_params("arbitrary"),
        name="pool_sample",
    )(*args)


def _swap_token_seq(a, lead):
    n = a.shape[0] // lead
    return a.reshape(lead, n, a.shape[1]).transpose(1, 0, 2).reshape(a.shape)


def kernel(x_prompt, x_sample, mem_prompt, cache_mem_k, cache_mem_v, state_mlstm_C, state_mlstm_n,
           state_mlstm_m, state_pool_buf, norm_g, final_g, ffn_w_up, ffn_w_down, mlstm_w_in,
           mlstm_b_i, mlstm_b_f, mlstm_g_head, mlstm_w_out, pool_w_in, pool_w_grp, pool_scale,
           pool_w_out, xattn_w_q, xattn_w_kv, xattn_w_o):
    n_mlstm = mlstm_w_in.shape[0]
    wup = ffn_w_up.astype(BF16)
    wdn = ffn_w_down.astype(BF16)
    w_in = mlstm_w_in.astype(BF16)
    w_gate = jnp.pad(mlstm_w_in[:, :, 4 * D_MODEL:],
                     ((0, 0), (0, 0), (0, LANES - 2 * N_HEADS))).astype(BF16)
    w_mout = mlstm_w_out.astype(BF16)
    gate_bias = jnp.concatenate([mlstm_b_i, mlstm_b_f], axis=1).astype(F32)
    p_win = pool_w_in.astype(BF16)
    p_wgrp = pool_w_grp.astype(BF16)
    p_wout = pool_w_out.astype(BF16)
    wq = xattn_w_q.astype(BF16)
    wkv = xattn_w_kv.astype(BF16)
    wo = xattn_w_o.astype(BF16)

    mem_k, mem_v, mem_kb, mem_vb = _mem_kv(mem_prompt.reshape(BATCH * N_MEM, D_MODEL), wkv)

    x = [x_prompt.reshape(N_PROMPT, D_MODEL), x_sample.reshape(N_SAMPLE, D_MODEL)]
    c_p = c_s = buf_s = None
    n_p, m_p, n_s, m_s, buf_p = [], [], [], [], []
    for l in range(DEPTH):
        g = norm_g[l].reshape(4, 1, D_MODEL)
        j = l // 2
        x = _ffn(x if l == 0 else [x], g[0], wup, wdn, l, 0)
        if l % 2 == 0:
            p, gate = _mlstm_proj(x, g[1], w_in, w_gate, j)
            gates = gate[:, :GATE_ROWS]
            bias_col = jnp.pad(gate_bias[j], (0, LANES - GATE_ROWS)).reshape(1, LANES)
            bias_row = gate_bias[j].reshape(GATE_ROWS, 1)
            g_head = mlstm_g_head[j].reshape(1, D_MODEL)
            x, c_p, n1, m1 = _mlstm_prompt(x, p, gate, gates.T, bias_col, bias_row, g_head, w_mout,
                                           j, n_mlstm, c_p)
            gate_t3 = gates[N_PROMPT:].reshape(DEC_BATCH, DEC_SEQ, GATE_ROWS).transpose(0, 2, 1)
            m_in = jnp.broadcast_to(
                jnp.pad(state_mlstm_m[j], ((0, 0), (0, GATE_ROWS - N_HEADS)))[:, :, None],
                (DEC_BATCH, GATE_ROWS, LANES))
            y3, c_s, n2, m2 = _mlstm_sample(
                p[N_PROMPT:].reshape(DEC_BATCH, DEC_SEQ, 4 * D_MODEL),
                gate[N_PROMPT:].reshape(DEC_BATCH, DEC_SEQ, LANES), gate_t3,
                bias_col, bias_row, g_head, state_mlstm_C, state_mlstm_n[j], m_in, j, c_s)
            x = _matmul_residual_sample(x, y3.reshape(N_SAMPLE, D_MODEL), w_mout, j)
            n_p.append(n1); m_p.append(m1[:, :N_HEADS, 0])
            n_s.append(n2); m_s.append(m2[:, :N_HEADS, 0])
        else:
            scale = pool_scale[j].reshape(1, D_MODEL)
            x, b1 = _pool_prompt(x, g[1], p_win, p_wgrp, scale, p_wout, j)
            xs_t = _swap_token_seq(x[N_PROMPT:], DEC_BATCH)
            z_t, buf_s = _pool_sample(xs_t, state_pool_buf, g[1], p_win, p_wgrp, scale, j, buf_s)
            x = _matmul_residual_sample(x, _swap_token_seq(z_t, DEC_SEQ), p_wout, j)
            buf_p.append(b1[:, POOL_HALO - POOL_BUF:])
        x = _xattn_prompt(x, g[2], wq, mem_kb, mem_vb, wo, l)
        q_s = _norm_matmul_sample(x, g[2], wq, l)
        a_s = _xattn_sample(_heads_major(q_s), cache_mem_k, cache_mem_v, l)
        x = _matmul_residual_sample(x, _tokens_major(a_s), wo, l)
        x = _ffn([x], g[3], wup, wdn, l, 1,
                 final_g=final_g.reshape(1, D_MODEL) if l == DEPTH - 1 else None)

    y_prompt, y_sample = x
    return (y_prompt.reshape(BATCH, SEQ, D_MODEL), y_sample.reshape(DEC_BATCH, DEC_SEQ, D_MODEL),
            mem_k, mem_v, c_p, jnp.stack(n_p), jnp.stack(m_p),
            c_s, jnp.stack(n_s), jnp.stack(m_s), jnp.stack(buf_p), buf_s)
```

```python
import functools

import jax
import jax.numpy as jnp
from jax import lax
from jax.experimental import pallas as pl
from jax.experimental.pallas import tpu as pltpu

F32 = jnp.float32
BF16 = jnp.bfloat16

D_MODEL = 1024
BATCH = 8
SEQ = 2048
DEPTH = 4
DEC_BATCH = 128
DEC_SEQ = 4
PAST_LEN = 16384
N_HEADS = 4
HEAD_DIM = D_MODEL // N_HEADS
CHUNK = 128
POOL_WINDOWS = (2, 4, 8, 16)
POOL_GROUP_DIM = D_MODEL // len(POOL_WINDOWS)
POOL_BUF = max(POOL_WINDOWS) - 1
POOL_HALO = 16
N_MEM = 256
N_XHEADS = 4
XHEAD_DIM = D_MODEL // N_XHEADS
D_FF = ((8 * D_MODEL // 3 + 127) // 128) * 128
EPS = 1e-6

N_PROMPT = BATCH * SEQ
N_SAMPLE = DEC_BATCH * DEC_SEQ
N_ROWS = N_PROMPT + N_SAMPLE

LANES = 128
GATE_ROWS = 8
ROW_TILE = 512
PROMPT_TILES = N_PROMPT // ROW_TILE
assert N_SAMPLE == ROW_TILE
FFN_CHUNK = 1408
VMEM_LIMIT = 56 * 1024 * 1024

_NT = (((1,), (1,)), ((), ()))
_TN = (((0,), (0,)), ((), ()))


def _params(*sem):
    return pltpu.CompilerParams(dimension_semantics=sem, vmem_limit_bytes=VMEM_LIMIT)


def _resident(block_shape, index_map):
    return pl.BlockSpec(block_shape, index_map, pipeline_mode=pl.Buffered(1))


def _rms(x, g):
    return x * lax.rsqrt(jnp.mean(x * x, axis=-1, keepdims=True) + EPS) * g


def _log_sigmoid(x):
    return jnp.minimum(x, 0.0) - jnp.log1p(jnp.exp(-jnp.abs(x)))


def _dot(a, b):
    return jnp.dot(a, b, preferred_element_type=F32)


def _ffn_kernel(*refs, first, final):
    is_prompt = pl.program_id(0) < PROMPT_TILES
    if first:
        xp_ref, xs_ref, *refs = refs
        x = jnp.where(is_prompt, xp_ref[...], xs_ref[...])
    else:
        x_ref, *refs = refs
        x = x_ref[...]
    g_ref, wup_ref, wdn_ref, *rest = refs
    xn = _rms(x, g_ref[...]).astype(BF16)
    acc = None
    for c in range(D_FF // FFN_CHUNK):
        lo, hi = c * FFN_CHUNK, (c + 1) * FFN_CHUNK
        gate = _dot(xn, wup_ref[:, lo:hi])
        up = _dot(xn, wup_ref[:, D_FF + lo:D_FF + hi])
        act = (gate * jax.nn.sigmoid(gate) * up).astype(BF16)
        part = _dot(act, wdn_ref[lo:hi, :])
        acc = part if acc is None else acc + part
    y = x + 0.5 * acc
    if not final:
        (o_ref,) = rest
        o_ref[...] = y
        return
    fg_ref, op_ref, os_ref = rest
    y = _rms(y, fg_ref[...])

    @pl.when(is_prompt)
    def _():
        op_ref[...] = y

    @pl.when(jnp.logical_not(is_prompt))
    def _():
        os_ref[...] = y


def _ffn(xs, g, wup_all, wdn_all, l, i, final_g=None):
    first, final = len(xs) == 2, final_g is not None
    row = pl.BlockSpec((ROW_TILE, D_MODEL), lambda r: (r, 0))
    prompt_row = pl.BlockSpec((ROW_TILE, D_MODEL), lambda r: (jnp.minimum(r, PROMPT_TILES - 1), 0))
    sample_row = pl.BlockSpec((ROW_TILE, D_MODEL), lambda r: (0, 0))
    vec = pl.BlockSpec((1, D_MODEL), lambda r: (0, 0))
    in_specs = ([prompt_row, sample_row] if first else [row]) + [
        vec,
        _resident((None, None, D_MODEL, 2 * D_FF), lambda r: (l, i, 0, 0)),
        _resident((None, None, D_FF, D_MODEL), lambda r: (l, i, 0, 0))]
    args = list(xs) + [g, wup_all, wdn_all]
    if final:
        in_specs.append(vec)
        args.append(final_g)
        out_specs = [prompt_row, sample_row]
        out_shape = [jax.ShapeDtypeStruct((N_PROMPT, D_MODEL), F32),
                     jax.ShapeDtypeStruct((N_SAMPLE, D_MODEL), F32)]
    else:
        out_specs = row
        out_shape = jax.ShapeDtypeStruct((N_ROWS, D_MODEL), F32)
    return pl.pallas_call(
        functools.partial(_ffn_kernel, first=first, final=final),
        grid=(N_ROWS // ROW_TILE,),
        in_specs=in_specs,
        out_specs=out_specs,
        out_shape=out_shape,
        compiler_params=_params("arbitrary"),
        name="ffn_final" if final else ("ffn_first" if first else "ffn"),
    )(*args)


def _prompt_rows(n):
    return pl.BlockSpec((ROW_TILE, n), lambda r: (jnp.minimum(r, PROMPT_TILES - 1), 0))


def _sample_rows(n):
    return pl.BlockSpec((ROW_TILE, n), lambda r: (0, 0))


def _mlstm_proj_kernel(x_ref, g_ref, w_ref, wg_ref, pp_ref, gp_ref, ps_ref, gs_ref):
    xn = _rms(x_ref[...], g_ref[...]).astype(BF16)
    p = _dot(xn, w_ref[...])
    gate = _dot(xn, wg_ref[...])
    is_prompt = pl.program_id(0) < PROMPT_TILES

    @pl.when(is_prompt)
    def _():
        pp_ref[...] = p
        gp_ref[...] = gate

    @pl.when(jnp.logical_not(is_prompt))
    def _():
        ps_ref[...] = p
        gs_ref[...] = gate


def _mlstm_proj(x, g, w_all, wg_all, j):
    return pl.pallas_call(
        _mlstm_proj_kernel,
        grid=(N_ROWS // ROW_TILE,),
        in_specs=[pl.BlockSpec((ROW_TILE, D_MODEL), lambda r: (r, 0)),
                  pl.BlockSpec((1, D_MODEL), lambda r: (0, 0)),
                  _resident((None, D_MODEL, 4 * D_MODEL), lambda r: (j, 0, 0)),
                  _resident((None, D_MODEL, LANES), lambda r: (j, 0, 0))],
        out_specs=[_prompt_rows(4 * D_MODEL), _prompt_rows(LANES),
                   _sample_rows(4 * D_MODEL), _sample_rows(LANES)],
        out_shape=[jax.ShapeDtypeStruct((N_PROMPT, 4 * D_MODEL), F32),
                   jax.ShapeDtypeStruct((N_PROMPT, LANES), F32),
                   jax.ShapeDtypeStruct((N_SAMPLE, 4 * D_MODEL), F32),
                   jax.ShapeDtypeStruct((N_SAMPLE, LANES), F32)],
        compiler_params=_params("arbitrary"),
        name="mlstm_proj",
    )(x, g, w_all, wg_all)


def _gate_terms(gate_col, gate_row, bias_col, bias_row, length):
    r = lax.broadcasted_iota(jnp.int32, (length, length), 0)
    c = lax.broadcasted_iota(jnp.int32, (length, length), 1)
    mask = r >= c
    lower = mask.astype(F32)
    upper = (r <= c).astype(F32)
    zc = gate_col + bias_col
    lane = lax.broadcasted_iota(jnp.int32, zc.shape, 1)
    log_col = jnp.where(lane < N_HEADS, zc, _log_sigmoid(zc))
    cum_col = jnp.dot(lower, log_col, precision=lax.Precision.HIGHEST,
                      preferred_element_type=F32)
    zr = gate_row + bias_row
    sub = lax.broadcasted_iota(jnp.int32, zr.shape, 0)
    log_row = jnp.where(sub < N_HEADS, zr, _log_sigmoid(zr))
    cum_row = jnp.dot(log_row, upper, precision=lax.Precision.HIGHEST,
                      preferred_element_type=F32)
    return mask, log_col, cum_col, log_row, cum_row


def _mlstm_chunk(p_views, gates, bias_col, bias_row, g_head_ref, length, get_state, put_state, put_y):
    pairs = [(s, h) for s in range(len(p_views)) for h in range(N_HEADS)]
    terms = [_gate_terms(gc, gr, bias_col, bias_row, length) for gc, gr in gates]
    mask = terms[0][0]

    def cols(h):
        return slice(h * HEAD_DIM, (h + 1) * HEAD_DIM)

    def part(s, h, which):
        lo = which * D_MODEL + h * HEAD_DIM
        return p_views[s][:, lo:lo + HEAD_DIM]

    b_col, b_row, i_col, i_row = {}, {}, {}, {}
    for s, h in pairs:
        _, log_col, cum_col, log_row, cum_row = terms[s]
        f = N_HEADS + h
        b_col[s, h], b_row[s, h] = cum_col[:, f:f + 1], cum_row[f:f + 1, :]
        i_col[s, h], i_row[s, h] = log_col[:, h:h + 1], log_row[h:h + 1, :]

    a, dlog, mt, dw, inter = {}, {}, {}, {}, {}
    for sh in pairs:
        a[sh] = b_col[sh] + get_state(*sh, "m")
        dlog[sh] = jnp.where(mask, b_col[sh] - b_row[sh] + i_row[sh], -jnp.inf)
    for sh in pairs:
        mt[sh] = jnp.maximum(a[sh], jnp.max(dlog[sh], axis=1, keepdims=True))
    for sh in pairs:
        dw[sh] = jnp.exp(dlog[sh] - mt[sh])
        inter[sh] = jnp.exp(a[sh] - mt[sh])

    sc, q_c, q_n, vb, k32 = {}, {}, {}, {}, {}
    for s, h in pairs:
        q = part(s, h, 0)
        k32[s, h] = part(s, h, 1) * (HEAD_DIM ** -0.5)
        qb = q.astype(BF16)
        sc[s, h] = lax.dot_general(qb, k32[s, h].astype(BF16), _NT,
                                   preferred_element_type=F32) * dw[s, h]
        q_c[s, h] = _dot(qb, get_state(s, h, "c").astype(BF16))
        q_n[s, h] = jnp.sum(q * get_state(s, h, "n"), axis=1, keepdims=True)
    hid = {}
    for s, h in pairs:
        vb[s, h] = part(s, h, 2).astype(BF16)
        num = inter[s, h] * q_c[s, h] + _dot(sc[s, h].astype(BF16), vb[s, h])
        den = inter[s, h] * q_n[s, h] + jnp.sum(sc[s, h], axis=1, keepdims=True)
        hid[s, h] = num / jnp.maximum(jnp.abs(den), jnp.exp(-mt[s, h]))
    for s, h in pairs:
        hh = hid[s, h]
        hn = hh * lax.rsqrt(jnp.mean(hh * hh, axis=1, keepdims=True) + EPS) * g_head_ref[:, cols(h)]
        put_y(s, h, jax.nn.sigmoid(part(s, h, 3)) * hn)
    for s, h in pairs:
        c_state, n_state, m_state = (get_state(s, h, which) for which in "cnm")
        m_new = mt[s, h][length - 1:length, :]
        b_last = b_col[s, h][length - 1:length, :]
        decay = jnp.exp(b_last + m_state - m_new)
        kw = k32[s, h] * jnp.exp(b_last - b_col[s, h] + i_col[s, h] - m_new)
        c_new = decay * c_state + lax.dot_general(kw.astype(BF16), vb[s, h], _TN,
                                                  preferred_element_type=F32)
        n_new = decay * n_state + jnp.sum(kw, axis=0, keepdims=True)
        put_state(s, h, c_new, n_new, m_new)


def _own_layer(ref, layer, first):
    if not first:
        return ref
    for other in range(ref.shape[0]):
        if other != layer:
            ref[other] = jnp.zeros(ref.shape[1:], ref.dtype)
    return ref.at[layer]


MLSTM_PROMPT_BATCH = 2


def _mlstm_prompt_kernel(p_ref, gc_ref, gr_ref, bc_ref, br_ref, gh_ref, *rest, layer, first):
    y_ref, co_ref, no_ref, mo_ref, c_sc, n_sc, m_sc = rest[-7:]
    chunk = pl.program_id(1)

    @pl.when(chunk == 0)
    def _():
        c_sc[...] = jnp.zeros_like(c_sc)
        n_sc[...] = jnp.zeros_like(n_sc)
        m_sc[...] = jnp.zeros_like(m_sc)

    def get_state(b, h, which):
        return {"c": lambda: c_sc[b, h], "n": lambda: n_sc[b, h:h + 1, :],
                "m": lambda: m_sc[b, h:h + 1, 0:1]}[which]()

    def put_state(b, h, c_new, n_new, m_new):
        c_sc[b, h] = c_new
        n_sc[b, h:h + 1, :] = n_new
        m_sc[b, h:h + 1, :] = jnp.broadcast_to(m_new, (1, LANES))

    def put_y(b, h, y):
        y_ref[b, :, h * HEAD_DIM:(h + 1) * HEAD_DIM] = y.astype(BF16)

    seqs = range(MLSTM_PROMPT_BATCH)
    _mlstm_chunk([p_ref.at[b] for b in seqs], [(gc_ref[b], gr_ref[b]) for b in seqs],
                 bc_ref[...], br_ref[...], gh_ref, CHUNK, get_state, put_state, put_y)

    @pl.when(chunk == pl.num_programs(1) - 1)
    def _():
        _own_layer(co_ref, layer, first)[...] = c_sc[...]
        no_ref[...] = n_sc[:, 0:N_HEADS, :]
        mo_ref[...] = m_sc[...]


def _mlstm_prompt(p, gate, gate_t, bias_col, bias_row, g_head, j, n_layers, c_prev):
    bs = MLSTM_PROMPT_BATCH
    first = c_prev is None
    state_block = (bs, N_HEADS, HEAD_DIM, HEAD_DIM)
    if first:
        c_spec = pl.BlockSpec((n_layers,) + state_block, lambda i, c: (0, i, 0, 0, 0))
    else:
        c_spec = pl.BlockSpec((None,) + state_block, lambda i, c: (j, i, 0, 0, 0))
    tok = lambda n: pl.BlockSpec((bs, CHUNK, n), lambda i, c: (i, c, 0))
    const = lambda shape: pl.BlockSpec(shape, lambda i, c: (0,) * len(shape))
    in_specs = [tok(4 * D_MODEL), tok(LANES),
                pl.BlockSpec((bs, GATE_ROWS, CHUNK), lambda i, c: (i, 0, c)),
                const((1, LANES)), const((GATE_ROWS, 1)), const((1, D_MODEL))]
    args = [p, gate, gate_t, bias_col, bias_row, g_head]
    aliases = {}
    if c_prev is not None:
        in_specs.append(pl.BlockSpec(memory_space=pl.ANY))
        args.append(c_prev)
        aliases[len(args) - 1] = 1
    return pl.pallas_call(
        functools.partial(_mlstm_prompt_kernel, layer=j, first=first),
        grid=(BATCH // bs, SEQ // CHUNK),
        in_specs=in_specs,
        out_specs=[tok(D_MODEL), c_spec,
                   pl.BlockSpec((bs, N_HEADS, HEAD_DIM), lambda i, c: (i, 0, 0)),
                   pl.BlockSpec((bs, GATE_ROWS, LANES), lambda i, c: (i, 0, 0))],
        out_shape=[jax.ShapeDtypeStruct((BATCH, SEQ, D_MODEL), BF16),
                   jax.ShapeDtypeStruct((n_layers, BATCH, N_HEADS, HEAD_DIM, HEAD_DIM), F32),
                   jax.ShapeDtypeStruct((BATCH, N_HEADS, HEAD_DIM), F32),
                   jax.ShapeDtypeStruct((BATCH, GATE_ROWS, LANES), F32)],
        scratch_shapes=[pltpu.VMEM((bs, N_HEADS, HEAD_DIM, HEAD_DIM), F32),
                        pltpu.VMEM((bs, GATE_ROWS, HEAD_DIM), F32),
                        pltpu.VMEM((bs, GATE_ROWS, LANES), F32)],
        input_output_aliases=aliases,
        compiler_params=_params("arbitrary", "arbitrary"),
        name="mlstm_prompt",
    )(*args)


MLSTM_SAMPLE_BATCH = 4


def _mlstm_sample_kernel(p_ref, gc_ref, gr_ref, bc_ref, br_ref, gh_ref, ci_ref, ni_ref, mi_ref, *rest,
                         layer, first):
    y_ref, co_ref, no_ref, mo_ref = rest[-4:]
    co_ref = _own_layer(co_ref, layer, first)

    def get_state(b, h, which):
        return {"c": lambda: ci_ref[b, h], "n": lambda: ni_ref[b, h:h + 1, :],
                "m": lambda: mi_ref[b, h:h + 1, 0:1]}[which]()

    def put_state(b, h, c_new, n_new, m_new):
        co_ref[b, h] = c_new
        no_ref[b, h:h + 1, :] = n_new
        mo_ref[b, h:h + 1, :] = jnp.broadcast_to(m_new, (1, LANES))

    def put_y(b, h, y):
        y_ref[b, :, h * HEAD_DIM:(h + 1) * HEAD_DIM] = y

    mo_ref[...] = jnp.zeros(mo_ref.shape, F32)
    seqs = range(MLSTM_SAMPLE_BATCH)
    _mlstm_chunk([p_ref.at[b] for b in seqs], [(gc_ref[b], gr_ref[b]) for b in seqs],
                 bc_ref[...], br_ref[...], gh_ref, DEC_SEQ, get_state, put_state, put_y)


def _mlstm_sample(p3, gate3, gate_t3, bias_col, bias_row, g_head, c_all, n_in, m_in, j, c_prev):
    bs = MLSTM_SAMPLE_BATCH
    first = c_prev is None
    const = lambda shape: pl.BlockSpec(shape, lambda i: (0,) * len(shape))
    tok = lambda n: pl.BlockSpec((bs, DEC_SEQ, n), lambda i: (i, 0, 0))
    state_block = (bs, N_HEADS, HEAD_DIM, HEAD_DIM)
    c_spec = pl.BlockSpec((None,) + state_block, lambda i: (j, i, 0, 0, 0))
    c_out = pl.BlockSpec((c_all.shape[0],) + state_block, lambda i: (0, i, 0, 0, 0)) if first else c_spec
    n_spec = pl.BlockSpec((bs, N_HEADS, HEAD_DIM), lambda i: (i, 0, 0))
    m_spec = pl.BlockSpec((bs, GATE_ROWS, LANES), lambda i: (i, 0, 0))
    in_specs = [tok(4 * D_MODEL), tok(LANES),
                pl.BlockSpec((bs, GATE_ROWS, DEC_SEQ), lambda i: (i, 0, 0)),
                const((1, LANES)), const((GATE_ROWS, 1)), const((1, D_MODEL)),
                c_spec, n_spec, m_spec]
    args = [p3, gate3, gate_t3, bias_col, bias_row, g_head, c_all, n_in, m_in]
    aliases = {}
    if c_prev is not None:
        in_specs.append(pl.BlockSpec(memory_space=pl.ANY))
        args.append(c_prev)
        aliases[len(args) - 1] = 1
    return pl.pallas_call(
        functools.partial(_mlstm_sample_kernel, layer=j, first=first),
        grid=(DEC_BATCH // bs,),
        in_specs=in_specs,
        out_specs=[tok(D_MODEL), c_out, n_spec, m_spec],
        out_shape=[jax.ShapeDtypeStruct((DEC_BATCH, DEC_SEQ, D_MODEL), F32),
                   jax.ShapeDtypeStruct(c_all.shape, F32),
                   jax.ShapeDtypeStruct((DEC_BATCH, N_HEADS, HEAD_DIM), F32),
                   jax.ShapeDtypeStruct((DEC_BATCH, GATE_ROWS, LANES), F32)],
        input_output_aliases=aliases,
        compiler_params=_params("arbitrary"),
        name="mlstm_sample",
    )(*args)


def _norm_matmul_kernel(x_ref, g_ref, w_ref, o_ref):
    o_ref[...] = _dot(_rms(x_ref[...], g_ref[...]).astype(BF16), w_ref[...])


def _norm_matmul_sample(x, g, w_all, l):
    first = N_PROMPT // ROW_TILE
    return pl.pallas_call(
        _norm_matmul_kernel,
        grid=(N_SAMPLE // ROW_TILE,),
        in_specs=[pl.BlockSpec((ROW_TILE, D_MODEL), lambda r: (first + r, 0)),
                  pl.BlockSpec((1, D_MODEL), lambda r: (0, 0)),
                  _resident((None, D_MODEL, D_MODEL), lambda r: (l, 0, 0))],
        out_specs=pl.BlockSpec((ROW_TILE, D_MODEL), lambda r: (r, 0)),
        out_shape=jax.ShapeDtypeStruct((N_SAMPLE, D_MODEL), F32),
        compiler_params=_params("parallel"),
        name="norm_matmul_sample",
    )(x, g, w_all)


def _matmul_residual_all_kernel(x_ref, yp_ref, ys_ref, w_ref, o_ref):
    y = jnp.where(pl.program_id(0) < PROMPT_TILES, yp_ref[...], ys_ref[...])
    o_ref[...] = x_ref[...] + _dot(y, w_ref[...])


def _matmul_residual_all(x, y_prompt, y_sample, w_all, l):
    row = pl.BlockSpec((ROW_TILE, D_MODEL), lambda r: (r, 0))
    return pl.pallas_call(
        _matmul_residual_all_kernel,
        grid=(N_ROWS // ROW_TILE,),
        in_specs=[row, _prompt_rows(D_MODEL), _sample_rows(D_MODEL),
                  _resident((None, D_MODEL, D_MODEL), lambda r: (l, 0, 0))],
        out_specs=row,
        out_shape=jax.ShapeDtypeStruct((N_ROWS, D_MODEL), F32),
        input_output_aliases={0: 0},
        compiler_params=_params("arbitrary"),
        name="matmul_residual_all",
    )(x, y_prompt, y_sample, w_all)


def _matmul_residual_kernel(x_ref, y_ref, w_ref, o_ref):
    o_ref[...] = x_ref[...] + _dot(y_ref[...].astype(BF16), w_ref[...])


def _matmul_residual_sample(x, y, w_all, l):
    first = N_PROMPT // ROW_TILE
    xrow = pl.BlockSpec((ROW_TILE, D_MODEL), lambda r: (first + r, 0))
    return pl.pallas_call(
        _matmul_residual_kernel,
        grid=(N_SAMPLE // ROW_TILE,),
        in_specs=[xrow, pl.BlockSpec((ROW_TILE, D_MODEL), lambda r: (r, 0)),
                  _resident((None, D_MODEL, D_MODEL), lambda r: (l, 0, 0))],
        out_specs=xrow,
        out_shape=jax.ShapeDtypeStruct((N_ROWS, D_MODEL), F32),
        input_output_aliases={0: 0},
        compiler_params=_params("parallel"),
        name="matmul_residual_sample",
    )(x, y, w_all)


MEM_KV_BATCH = 2


def _mem_kv_kernel(m_ref, w_ref, k_ref, v_ref, kb_ref, vb_ref):
    kv = _dot(m_ref[...].astype(BF16), w_ref[...])
    kb_ref[...] = kv[:, :D_MODEL].astype(BF16)
    vb_ref[...] = kv[:, D_MODEL:].astype(BF16)
    for b in range(MEM_KV_BATCH):
        rows = slice(b * N_MEM, (b + 1) * N_MEM)
        for h in range(N_XHEADS):
            lo, hi = h * XHEAD_DIM, (h + 1) * XHEAD_DIM
            k_ref[b, :, h, :] = kv[rows, lo:hi]
            v_ref[b, :, h, :] = kv[rows, D_MODEL + lo:D_MODEL + hi]


def _mem_kv(mem, wkv_all):
    bs = MEM_KV_BATCH
    out = pl.BlockSpec((None, bs, N_MEM, N_XHEADS, XHEAD_DIM), lambda l, r: (l, r, 0, 0, 0))
    shape = jax.ShapeDtypeStruct((DEPTH, BATCH, N_MEM, N_XHEADS, XHEAD_DIM), F32)
    out_b = pl.BlockSpec((None, bs * N_MEM, D_MODEL), lambda l, r: (l, r, 0))
    shape_b = jax.ShapeDtypeStruct((DEPTH, BATCH * N_MEM, D_MODEL), BF16)
    return pl.pallas_call(
        _mem_kv_kernel,
        grid=(DEPTH, BATCH // bs),
        in_specs=[pl.BlockSpec((bs * N_MEM, D_MODEL), lambda l, r: (r, 0)),
                  pl.BlockSpec((None, D_MODEL, 2 * D_MODEL), lambda l, r: (l, 0, 0))],
        out_specs=[out, out, out_b, out_b],
        out_shape=[shape, shape, shape_b, shape_b],
        compiler_params=_params("arbitrary", "arbitrary"),
        name="mem_kv",
    )(mem, wkv_all)


def _softmax_rows(s):
    e = jnp.exp(s - jnp.max(s, axis=1, keepdims=True))
    return e / jnp.sum(e, axis=1, keepdims=True)


def _xattn_prompt_kernel(x_ref, g_ref, wq_ref, k_ref, v_ref, wo_ref, o_ref, a_sc):
    x = x_ref[...]
    q = _dot(_rms(x, g_ref[...]).astype(BF16), wq_ref[...])
    for h in range(N_XHEADS):
        lo, hi = h * XHEAD_DIM, (h + 1) * XHEAD_DIM
        s = lax.dot_general(q[:, lo:hi].astype(BF16), k_ref[:, lo:hi], _NT,
                            preferred_element_type=F32) * (XHEAD_DIM ** -0.5)
        a_sc[:, lo:hi] = _dot(_softmax_rows(s).astype(BF16), v_ref[:, lo:hi]).astype(BF16)
    o_ref[...] = x + _dot(a_sc[...], wo_ref[...])


def _xattn_prompt(x, g, wq_all, k_all, v_all, wo_all, l):
    nt = SEQ // ROW_TILE
    xrow = pl.BlockSpec((ROW_TILE, D_MODEL), lambda b, t: (b * nt + t, 0))
    kv = pl.BlockSpec((None, N_MEM, D_MODEL), lambda b, t: (l, b, 0))
    w = _resident((None, D_MODEL, D_MODEL), lambda b, t: (l, 0, 0))
    return pl.pallas_call(
        _xattn_prompt_kernel,
        grid=(BATCH, nt),
        in_specs=[xrow, pl.BlockSpec((1, D_MODEL), lambda b, t: (0, 0)), w, kv, kv, w],
        out_specs=xrow,
        out_shape=jax.ShapeDtypeStruct((N_ROWS, D_MODEL), F32),
        scratch_shapes=[pltpu.VMEM((ROW_TILE, D_MODEL), BF16)],
        input_output_aliases={0: 0},
        compiler_params=_params("parallel", "parallel"),
        name="xattn_prompt",
    )(x, g, wq_all, k_all, v_all, wo_all)


XATTN_SAMPLE_BATCH = 4


def _xattn_sample_kernel(q_ref, k_ref, v_ref, o_ref):
    n_q, n_kv = N_XHEADS * DEC_SEQ, N_MEM * N_XHEADS
    q_head = lax.broadcasted_iota(jnp.int32, (n_q, n_kv), 0) // DEC_SEQ
    kv_head = lax.broadcasted_iota(jnp.int32, (n_q, n_kv), 1) % N_XHEADS
    same_head = q_head == kv_head

    seqs = range(XATTN_SAMPLE_BATCH)
    scores = [lax.dot_general(q_ref[b].astype(BF16),
                              k_ref[b].reshape(n_kv, XHEAD_DIM).astype(BF16), _NT,
                              preferred_element_type=F32) * (XHEAD_DIM ** -0.5) for b in seqs]
    probs = [_softmax_rows(jnp.where(same_head, s, -jnp.inf)).astype(BF16) for s in scores]
    for b in seqs:
        o_ref[b] = _dot(probs[b], v_ref[b].reshape(n_kv, XHEAD_DIM).astype(BF16))


def _xattn_sample(q, k_cache, v_cache, l):
    bs = XATTN_SAMPLE_BATCH
    tok = pl.BlockSpec((bs, N_XHEADS * DEC_SEQ, XHEAD_DIM), lambda i: (i, 0, 0))
    kv = pl.BlockSpec((None, bs, N_MEM, N_XHEADS, XHEAD_DIM), lambda i: (l, i, 0, 0, 0))
    return pl.pallas_call(
        _xattn_sample_kernel,
        grid=(DEC_BATCH // bs,),
        in_specs=[tok, kv, kv],
        out_specs=tok,
        out_shape=jax.ShapeDtypeStruct((DEC_BATCH, N_XHEADS * DEC_SEQ, XHEAD_DIM), F32),
        compiler_params=_params("parallel"),
        name="xattn_sample",
    )(q, k_cache, v_cache)


def _heads_major(a):
    a = a.reshape(DEC_BATCH, DEC_SEQ, N_XHEADS, XHEAD_DIM)
    return a.transpose(0, 2, 1, 3).reshape(DEC_BATCH, N_XHEADS * DEC_SEQ, XHEAD_DIM)


def _tokens_major(a):
    a = a.reshape(DEC_BATCH, N_XHEADS, DEC_SEQ, XHEAD_DIM)
    return a.transpose(0, 2, 1, 3).reshape(N_SAMPLE, D_MODEL)


def _pool_prompt_kernel(x_ref, g_ref, win_ref, wgrp_ref, scale_ref, wout_ref,
                        o_ref, buf_ref, ext_sc, z_sc):
    tile = pl.program_id(1)

    @pl.when(tile == 0)
    def _():
        ext_sc[0:POOL_HALO, :] = jnp.zeros((POOL_HALO, D_MODEL), F32)

    x = x_ref[...]
    u = _dot(_rms(x, g_ref[...]).astype(BF16), win_ref[...])
    ext_sc[POOL_HALO:, :] = u
    pos = tile * ROW_TILE + lax.broadcasted_iota(jnp.int32, (ROW_TILE, 1), 0)
    for g, w in enumerate(POOL_WINDOWS):
        lo, hi = g * POOL_GROUP_DIM, (g + 1) * POOL_GROUP_DIM
        assert w & (w - 1) == 0 and w <= POOL_HALO
        win = ext_sc[:, lo:hi]
        span = 1
        while span < w:
            win = win + pltpu.roll(win, span, 0)
            span *= 2
        win = win[POOL_HALO:, :]
        cnt = jnp.minimum(pos + 1, w).astype(F32)
        d = win / cnt - u[:, lo:hi]
        z_sc[:, lo:hi] = (_dot(d.astype(BF16), wgrp_ref[g]) * scale_ref[:, lo:hi]).astype(BF16)
    o_ref[...] = x + _dot(z_sc[...], wout_ref[...])
    tail = ext_sc[ROW_TILE:ROW_TILE + POOL_HALO, :]
    ext_sc[0:POOL_HALO, :] = tail

    @pl.when(tile == pl.num_programs(1) - 1)
    def _():
        buf_ref[...] = tail


def _pool_prompt(x, g, win_all, wgrp_all, scale, wout_all, j):
    nt = SEQ // ROW_TILE
    xrow = pl.BlockSpec((ROW_TILE, D_MODEL), lambda b, t: (b * nt + t, 0))
    vec = pl.BlockSpec((1, D_MODEL), lambda b, t: (0, 0))
    w = _resident((None, D_MODEL, D_MODEL), lambda b, t: (j, 0, 0))
    wgrp = _resident((None, len(POOL_WINDOWS), POOL_GROUP_DIM, POOL_GROUP_DIM),
                     lambda b, t: (j, 0, 0, 0))
    return pl.pallas_call(
        _pool_prompt_kernel,
        grid=(BATCH, nt),
        in_specs=[xrow, vec, w, wgrp, vec, w],
        out_specs=[xrow, pl.BlockSpec((None, POOL_HALO, D_MODEL), lambda b, t: (b, 0, 0))],
        out_shape=[jax.ShapeDtypeStruct((N_ROWS, D_MODEL), F32),
                   jax.ShapeDtypeStruct((BATCH, POOL_HALO, D_MODEL), F32)],
        scratch_shapes=[pltpu.VMEM((POOL_HALO + ROW_TILE, D_MODEL), F32),
                        pltpu.VMEM((ROW_TILE, D_MODEL), BF16)],
        input_output_aliases={0: 0},
        compiler_params=_params("arbitrary", "arbitrary"),
        name="pool_prompt",
    )(x, g, win_all, wgrp_all, scale, wout_all)


def _pool_sample_kernel(x_ref, buf_ref, g_ref, win_ref, wgrp_ref, scale_ref, *rest, layer, first):
    z_ref, nbuf_ref = rest[-2:]
    nbuf_ref = _own_layer(nbuf_ref, layer, first)
    u = _dot(_rms(x_ref[...], g_ref[...]).astype(BF16), win_ref[...])

    def ext(row, lo, hi):
        if row < POOL_BUF:
            return buf_ref[:, row, lo:hi]
        t = row - POOL_BUF
        return u[t * DEC_BATCH:(t + 1) * DEC_BATCH, lo:hi]

    for g, w in enumerate(POOL_WINDOWS):
        lo, hi = g * POOL_GROUP_DIM, (g + 1) * POOL_GROUP_DIM
        ds = []
        for t in range(DEC_SEQ):
            win = ext(POOL_BUF + t, lo, hi)
            for back in range(1, w):
                win = win + ext(POOL_BUF + t - back, lo, hi)
            cnt = float(min(PAST_LEN + 1 + t, w))
            ds.append(win / cnt - ext(POOL_BUF + t, lo, hi))
        d = jnp.concatenate(ds, axis=0).astype(BF16)
        z_ref[:, lo:hi] = _dot(d, wgrp_ref[g]) * scale_ref[:, lo:hi]
    for row in range(POOL_BUF):
        nbuf_ref[:, row, :] = ext(row + DEC_SEQ, 0, D_MODEL)


def _pool_sample(xs_t, buf_all, g, win_all, wgrp_all, scale, j, buf_prev):
    first = buf_prev is None
    buf_block = (DEC_BATCH, POOL_BUF, D_MODEL)
    once = lambda shape, imap: pl.BlockSpec(shape, imap, pipeline_mode=pl.Buffered(1))
    if first:
        nbuf = once((buf_all.shape[0],) + buf_block, lambda i: (0, 0, 0, 0))
    else:
        nbuf = once((None,) + buf_block, lambda i: (j, 0, 0, 0))
    vec = pl.BlockSpec((1, D_MODEL), lambda i: (0, 0))
    buf = once((None, DEC_BATCH, POOL_BUF, D_MODEL), lambda i: (j, 0, 0, 0))
    in_specs = [pl.BlockSpec((N_SAMPLE, D_MODEL), lambda i: (0, 0)), buf, vec,
                once((None, D_MODEL, D_MODEL), lambda i: (j, 0, 0)),
                once((None, len(POOL_WINDOWS), POOL_GROUP_DIM, POOL_GROUP_DIM), lambda i: (j, 0, 0, 0)),
                vec]
    args = [xs_t, buf_all, g, win_all, wgrp_all, scale]
    aliases = {}
    if buf_prev is not None:
        in_specs.append(pl.BlockSpec(memory_space=pl.ANY))
        args.append(buf_prev)
        aliases[len(args) - 1] = 1
    return pl.pallas_call(
        functools.partial(_pool_sample_kernel, layer=j, first=first),
        grid=(1,),
        in_specs=in_specs,
        out_specs=[pl.BlockSpec((N_SAMPLE, D_MODEL), lambda i: (0, 0)), nbuf],
        out_shape=[jax.ShapeDtypeStruct((N_SAMPLE, D_MODEL), F32),
                   jax.ShapeDtypeStruct(buf_all.shape, F32)],
        input_output_aliases=aliases,
        compiler_params=_params("arbitrary"),
        name="pool_sample",
    )(*args)


def _swap_token_seq(a, lead):
    n = a.shape[0] // lead
    return a.reshape(lead, n, a.shape[1]).transpose(1, 0, 2).reshape(a.shape)


def kernel(x_prompt, x_sample, mem_prompt, cache_mem_k, cache_mem_v, state_mlstm_C, state_mlstm_n,
           state_mlstm_m, state_pool_buf, norm_g, final_g, ffn_w_up, ffn_w_down, mlstm_w_in,
           mlstm_b_i, mlstm_b_f, mlstm_g_head, mlstm_w_out, pool_w_in, pool_w_grp, pool_scale,
           pool_w_out, xattn_w_q, xattn_w_kv, xattn_w_o):
    n_mlstm = mlstm_w_in.shape[0]
    wup = ffn_w_up.astype(BF16)
    wdn = ffn_w_down.astype(BF16)
    w_in = mlstm_w_in.astype(BF16)
    w_gate = jnp.pad(mlstm_w_in[:, :, 4 * D_MODEL:],
                     ((0, 0), (0, 0), (0, LANES - 2 * N_HEADS))).astype(BF16)
    w_mout = mlstm_w_out.astype(BF16)
    gate_bias = jnp.concatenate([mlstm_b_i, mlstm_b_f], axis=1).astype(F32)
    p_win = pool_w_in.astype(BF16)
    p_wgrp = pool_w_grp.astype(BF16)
    p_wout = pool_w_out.astype(BF16)
    wq = xattn_w_q.astype(BF16)
    wkv = xattn_w_kv.astype(BF16)
    wo = xattn_w_o.astype(BF16)

    mem_k, mem_v, mem_kb, mem_vb = _mem_kv(mem_prompt.reshape(BATCH * N_MEM, D_MODEL), wkv)

    x = [x_prompt.reshape(N_PROMPT, D_MODEL), x_sample.reshape(N_SAMPLE, D_MODEL)]
    c_p = c_s = buf_s = None
    n_p, m_p, n_s, m_s, buf_p = [], [], [], [], []
    for l in range(DEPTH):
        g = norm_g[l].reshape(4, 1, D_MODEL)
        j = l // 2
        x = _ffn(x if l == 0 else [x], g[0], wup, wdn, l, 0)
        if l % 2 == 0:
            p_p, gate_p, p_s, gate_s = _mlstm_proj(x, g[1], w_in, w_gate, j)
            bias_col = jnp.pad(gate_bias[j], (0, LANES - GATE_ROWS)).reshape(1, LANES)
            bias_row = gate_bias[j].reshape(GATE_ROWS, 1)
            g_head = mlstm_g_head[j].reshape(1, D_MODEL)
            gate_p = gate_p.reshape(BATCH, SEQ, LANES)
            y_p, c_p, n1, m1 = _mlstm_prompt(
                p_p.reshape(BATCH, SEQ, 4 * D_MODEL), gate_p,
                gate_p[:, :, :GATE_ROWS].transpose(0, 2, 1), bias_col, bias_row, g_head,
                j, n_mlstm, c_p)
            gate_s = gate_s.reshape(DEC_BATCH, DEC_SEQ, LANES)
            m_in = jnp.broadcast_to(
                jnp.pad(state_mlstm_m[j], ((0, 0), (0, GATE_ROWS - N_HEADS)))[:, :, None],
                (DEC_BATCH, GATE_ROWS, LANES))
            y_s, c_s, n2, m2 = _mlstm_sample(
                p_s.reshape(DEC_BATCH, DEC_SEQ, 4 * D_MODEL), gate_s,
                gate_s[:, :, :GATE_ROWS].transpose(0, 2, 1),
                bias_col, bias_row, g_head, state_mlstm_C, state_mlstm_n[j], m_in, j, c_s)
            x = _matmul_residual_all(x, y_p.reshape(N_PROMPT, D_MODEL),
                                     y_s.reshape(N_SAMPLE, D_MODEL).astype(BF16), w_mout, j)
            n_p.append(n1); m_p.append(m1[:, :N_HEADS, 0])
            n_s.append(n2); m_s.append(m2[:, :N_HEADS, 0])
        else:
            scale = pool_scale[j].reshape(1, D_MODEL)
            x, b1 = _pool_prompt(x, g[1], p_win, p_wgrp, scale, p_wout, j)
            xs_t = _swap_token_seq(x[N_PROMPT:], DEC_BATCH)
            z_t, buf_s = _pool_sample(xs_t, state_pool_buf, g[1], p_win, p_wgrp, scale, j, buf_s)
            x = _matmul_residual_sample(x, _swap_token_seq(z_t, DEC_SEQ), p_wout, j)
            buf_p.append(b1[:, POOL_HALO - POOL_BUF:])
        x = _xattn_prompt(x, g[2], wq, mem_kb, mem_vb, wo, l)
        q_s = _norm_matmul_sample(x, g[2], wq, l)
        a_s = _xattn_sample(_heads_major(q_s), cache_mem_k, cache_mem_v, l)
        x = _matmul_residual_sample(x, _tokens_major(a_s), wo, l)
        x = _ffn([x], g[3], wup, wdn, l, 1,
                 final_g=final_g.reshape(1, D_MODEL) if l == DEPTH - 1 else None)

    y_prompt, y_sample = x
    return (y_prompt.reshape(BATCH, SEQ, D_MODEL), y_sample.reshape(DEC_BATCH, DEC_SEQ, D_MODEL),
            mem_k, mem_v, c_p, jnp.stack(n_p), jnp.stack(m_p),
            c_s, jnp.stack(n_s), jnp.stack(m_s), jnp.stack(buf_p), buf_s)
```

```python
import functools

import jax
import jax.numpy as jnp
from jax import lax
from jax.experimental import pallas as pl
from jax.experimental.pallas import tpu as pltpu

F32 = jnp.float32
BF16 = jnp.bfloat16

D_MODEL = 1024
BATCH = 8
SEQ = 2048
DEPTH = 4
DEC_BATCH = 128
DEC_SEQ = 4
PAST_LEN = 16384
N_HEADS = 4
HEAD_DIM = D_MODEL // N_HEADS
CHUNK = 128
POOL_WINDOWS = (2, 4, 8, 16)
POOL_GROUP_DIM = D_MODEL // len(POOL_WINDOWS)
POOL_BUF = max(POOL_WINDOWS) - 1
POOL_HALO = 16
N_MEM = 256
N_XHEADS = 4
XHEAD_DIM = D_MODEL // N_XHEADS
D_FF = ((8 * D_MODEL // 3 + 127) // 128) * 128
EPS = 1e-6

N_PROMPT = BATCH * SEQ
N_SAMPLE = DEC_BATCH * DEC_SEQ
N_ROWS = N_PROMPT + N_SAMPLE

LANES = 128
GATE_ROWS = 8
ROW_TILE = 512
PROMPT_TILES = N_PROMPT // ROW_TILE
assert N_SAMPLE == ROW_TILE
FFN_CHUNK = 1408
VMEM_LIMIT = 56 * 1024 * 1024

_NT = (((1,), (1,)), ((), ()))
_TN = (((0,), (0,)), ((), ()))


def _params(*sem):
    return pltpu.CompilerParams(dimension_semantics=sem, vmem_limit_bytes=VMEM_LIMIT)


def _resident(block_shape, index_map):
    return pl.BlockSpec(block_shape, index_map, pipeline_mode=pl.Buffered(1))


def _rms(x, g):
    return x * lax.rsqrt(jnp.mean(x * x, axis=-1, keepdims=True) + EPS) * g


def _log_sigmoid(x):
    return jnp.minimum(x, 0.0) - jnp.log1p(jnp.exp(-jnp.abs(x)))


def _dot(a, b):
    return jnp.dot(a, b, preferred_element_type=F32)


def _ffn_kernel(*refs, first, final):
    is_prompt = pl.program_id(0) < PROMPT_TILES
    if first:
        xp_ref, xs_ref, *refs = refs
        x = jnp.where(is_prompt, xp_ref[...], xs_ref[...])
    else:
        x_ref, *refs = refs
        x = x_ref[...]
    g_ref, wup_ref, wdn_ref, *rest = refs
    xn = _rms(x, g_ref[...]).astype(BF16)
    acc = None
    for c in range(D_FF // FFN_CHUNK):
        lo, hi = c * FFN_CHUNK, (c + 1) * FFN_CHUNK
        gate = _dot(xn, wup_ref[:, lo:hi])
        up = _dot(xn, wup_ref[:, D_FF + lo:D_FF + hi])
        act = (gate * jax.nn.sigmoid(gate) * up).astype(BF16)
        part = _dot(act, wdn_ref[lo:hi, :])
        acc = part if acc is None else acc + part
    y = x + 0.5 * acc
    if not final:
        (o_ref,) = rest
        o_ref[...] = y
        return
    fg_ref, op_ref, os_ref = rest
    y = _rms(y, fg_ref[...])

    @pl.when(is_prompt)
    def _():
        op_ref[...] = y

    @pl.when(jnp.logical_not(is_prompt))
    def _():
        os_ref[...] = y


def _ffn(xs, g, wup_all, wdn_all, l, i, final_g=None):
    first, final = len(xs) == 2, final_g is not None
    row = pl.BlockSpec((ROW_TILE, D_MODEL), lambda r: (r, 0))
    prompt_row = pl.BlockSpec((ROW_TILE, D_MODEL), lambda r: (jnp.minimum(r, PROMPT_TILES - 1), 0))
    sample_row = pl.BlockSpec((ROW_TILE, D_MODEL), lambda r: (0, 0))
    vec = pl.BlockSpec((1, D_MODEL), lambda r: (0, 0))
    in_specs = ([prompt_row, sample_row] if first else [row]) + [
        vec,
        _resident((None, None, D_MODEL, 2 * D_FF), lambda r: (l, i, 0, 0)),
        _resident((None, None, D_FF, D_MODEL), lambda r: (l, i, 0, 0))]
    args = list(xs) + [g, wup_all, wdn_all]
    if final:
        in_specs.append(vec)
        args.append(final_g)
        out_specs = [prompt_row, sample_row]
        out_shape = [jax.ShapeDtypeStruct((N_PROMPT, D_MODEL), F32),
                     jax.ShapeDtypeStruct((N_SAMPLE, D_MODEL), F32)]
    else:
        out_specs = row
        out_shape = jax.ShapeDtypeStruct((N_ROWS, D_MODEL), F32)
    return pl.pallas_call(
        functools.partial(_ffn_kernel, first=first, final=final),
        grid=(N_ROWS // ROW_TILE,),
        in_specs=in_specs,
        out_specs=out_specs,
        out_shape=out_shape,
        compiler_params=_params("arbitrary"),
        name="ffn_final" if final else ("ffn_first" if first else "ffn"),
    )(*args)


def _prompt_rows(n):
    return pl.BlockSpec((ROW_TILE, n), lambda r: (jnp.minimum(r, PROMPT_TILES - 1), 0))


def _sample_rows(n):
    return pl.BlockSpec((ROW_TILE, n), lambda r: (0, 0))


def _mlstm_proj_kernel(x_ref, g_ref, w_ref, wg_ref, pp_ref, gp_ref, ps_ref, gs_ref):
    xn = _rms(x_ref[...], g_ref[...]).astype(BF16)
    is_prompt = pl.program_id(0) < PROMPT_TILES

    @pl.when(is_prompt)
    def _():
        pp_ref[...] = _dot(xn, w_ref[...])
        gp_ref[...] = _dot(xn, wg_ref[...])

    @pl.when(jnp.logical_not(is_prompt))
    def _():
        ps_ref[...] = _dot(xn, w_ref[...])
        gs_ref[...] = _dot(xn, wg_ref[...])


def _mlstm_proj(x, g, w_all, wg_all, j):
    return pl.pallas_call(
        _mlstm_proj_kernel,
        grid=(N_ROWS // ROW_TILE,),
        in_specs=[pl.BlockSpec((ROW_TILE, D_MODEL), lambda r: (r, 0)),
                  pl.BlockSpec((1, D_MODEL), lambda r: (0, 0)),
                  _resident((None, D_MODEL, 4 * D_MODEL), lambda r: (j, 0, 0)),
                  _resident((None, D_MODEL, LANES), lambda r: (j, 0, 0))],
        out_specs=[_prompt_rows(4 * D_MODEL), _prompt_rows(LANES),
                   _sample_rows(4 * D_MODEL), _sample_rows(LANES)],
        out_shape=[jax.ShapeDtypeStruct((N_PROMPT, 4 * D_MODEL), F32),
                   jax.ShapeDtypeStruct((N_PROMPT, LANES), F32),
                   jax.ShapeDtypeStruct((N_SAMPLE, 4 * D_MODEL), F32),
                   jax.ShapeDtypeStruct((N_SAMPLE, LANES), F32)],
        compiler_params=_params("arbitrary"),
        name="mlstm_proj",
    )(x, g, w_all, wg_all)


def _gate_terms(gate_col, gate_row, bias_col, bias_row, length):
    r = lax.broadcasted_iota(jnp.int32, (length, length), 0)
    c = lax.broadcasted_iota(jnp.int32, (length, length), 1)
    mask = r >= c
    lower = mask.astype(F32)
    upper = (r <= c).astype(F32)
    zc = gate_col + bias_col
    lane = lax.broadcasted_iota(jnp.int32, zc.shape, 1)
    log_col = jnp.where(lane < N_HEADS, zc, _log_sigmoid(zc))
    cum_col = jnp.dot(lower, log_col, precision=lax.Precision.HIGHEST,
                      preferred_element_type=F32)
    zr = gate_row + bias_row
    sub = lax.broadcasted_iota(jnp.int32, zr.shape, 0)
    log_row = jnp.where(sub < N_HEADS, zr, _log_sigmoid(zr))
    cum_row = jnp.dot(log_row, upper, precision=lax.Precision.HIGHEST,
                      preferred_element_type=F32)
    return mask, log_col, cum_col, log_row, cum_row


def _mlstm_chunk(p_views, gates, bias_col, bias_row, g_head_ref, length, get_state, put_state, put_y):
    pairs = [(s, h) for s in range(len(p_views)) for h in range(N_HEADS)]
    terms = [_gate_terms(gc, gr, bias_col, bias_row, length) for gc, gr in gates]
    mask = terms[0][0]

    def cols(h):
        return slice(h * HEAD_DIM, (h + 1) * HEAD_DIM)

    def part(s, h, which):
        lo = which * D_MODEL + h * HEAD_DIM
        return p_views[s][:, lo:lo + HEAD_DIM]

    b_col, b_row, i_col, i_row = {}, {}, {}, {}
    for s, h in pairs:
        _, log_col, cum_col, log_row, cum_row = terms[s]
        f = N_HEADS + h
        b_col[s, h], b_row[s, h] = cum_col[:, f:f + 1], cum_row[f:f + 1, :]
        i_col[s, h], i_row[s, h] = log_col[:, h:h + 1], log_row[h:h + 1, :]

    a, dlog, mt, dw, inter = {}, {}, {}, {}, {}
    for sh in pairs:
        a[sh] = b_col[sh] + get_state(*sh, "m")
        dlog[sh] = jnp.where(mask, b_col[sh] - b_row[sh] + i_row[sh], -jnp.inf)
    for sh in pairs:
        mt[sh] = jnp.maximum(a[sh], jnp.max(dlog[sh], axis=1, keepdims=True))
    for sh in pairs:
        dw[sh] = jnp.exp(dlog[sh] - mt[sh])
        inter[sh] = jnp.exp(a[sh] - mt[sh])

    sc, q_c, q_n, vb, k32 = {}, {}, {}, {}, {}
    for s, h in pairs:
        q = part(s, h, 0)
        k32[s, h] = part(s, h, 1) * (HEAD_DIM ** -0.5)
        qb = q.astype(BF16)
        sc[s, h] = lax.dot_general(qb, k32[s, h].astype(BF16), _NT,
                                   preferred_element_type=F32) * dw[s, h]
        q_c[s, h] = _dot(qb, get_state(s, h, "c").astype(BF16))
        q_n[s, h] = jnp.sum(q * get_state(s, h, "n"), axis=1, keepdims=True)
    hid = {}
    for s, h in pairs:
        vb[s, h] = part(s, h, 2).astype(BF16)
        num = inter[s, h] * q_c[s, h] + _dot(sc[s, h].astype(BF16), vb[s, h])
        den = inter[s, h] * q_n[s, h] + jnp.sum(sc[s, h], axis=1, keepdims=True)
        hid[s, h] = num / jnp.maximum(jnp.abs(den), jnp.exp(-mt[s, h]))
    for s, h in pairs:
        hh = hid[s, h]
        hn = hh * lax.rsqrt(jnp.mean(hh * hh, axis=1, keepdims=True) + EPS) * g_head_ref[:, cols(h)]
        put_y(s, h, jax.nn.sigmoid(part(s, h, 3)) * hn)
    for s, h in pairs:
        c_state, n_state, m_state = (get_state(s, h, which) for which in "cnm")
        m_new = mt[s, h][length - 1:length, :]
        b_last = b_col[s, h][length - 1:length, :]
        decay = jnp.exp(b_last + m_state - m_new)
        kw = k32[s, h] * jnp.exp(b_last - b_col[s, h] + i_col[s, h] - m_new)
        c_new = decay * c_state + lax.dot_general(kw.astype(BF16), vb[s, h], _TN,
                                                  preferred_element_type=F32)
        n_new = decay * n_state + jnp.sum(kw, axis=0, keepdims=True)
        put_state(s, h, c_new, n_new, m_new)


def _own_layer(ref, layer, first):
    if not first:
        return ref
    for other in range(ref.shape[0]):
        if other != layer:
            ref[other] = jnp.zeros(ref.shape[1:], ref.dtype)
    return ref.at[layer]


def _mlstm_prompt_kernel(p_ref, gc_ref, gr_ref, x_ref, bc_ref, br_ref, gh_ref, wo_ref, *rest,
                         layer, first):
    xo_ref, co_ref, no_ref, mo_ref, c_sc, n_sc, m_sc, y_sc = rest[-8:]
    chunk = pl.program_id(1)

    @pl.when(chunk == 0)
    def _():
        c_sc[...] = jnp.zeros_like(c_sc)
        n_sc[...] = jnp.zeros_like(n_sc)
        m_sc[...] = jnp.zeros_like(m_sc)

    def get_state(_, h, which):
        return {"c": lambda: c_sc[h], "n": lambda: n_sc[h:h + 1, :],
                "m": lambda: m_sc[h:h + 1, 0:1]}[which]()

    def put_state(_, h, c_new, n_new, m_new):
        c_sc[h] = c_new
        n_sc[h:h + 1, :] = n_new
        m_sc[h:h + 1, :] = jnp.broadcast_to(m_new, (1, LANES))

    def put_y(_, h, y):
        y_sc[:, h * HEAD_DIM:(h + 1) * HEAD_DIM] = y.astype(BF16)

    _mlstm_chunk([p_ref], [(gc_ref[...], gr_ref[...])], bc_ref[...], br_ref[...], gh_ref, CHUNK,
                 get_state, put_state, put_y)
    xo_ref[...] = x_ref[...] + _dot(y_sc[...], wo_ref[...])

    @pl.when(chunk == pl.num_programs(1) - 1)
    def _():
        _own_layer(co_ref, layer, first)[...] = c_sc[...]
        no_ref[...] = n_sc[0:N_HEADS, :]
        mo_ref[...] = m_sc[...]


def _mlstm_prompt(x, p, gate, gate_t, bias_col, bias_row, g_head, wo_all, j, n_layers, c_prev):
    nc = SEQ // CHUNK
    first = c_prev is None
    state_block = (None, N_HEADS, HEAD_DIM, HEAD_DIM)
    if first:
        c_spec = pl.BlockSpec((n_layers,) + state_block, lambda b, c: (0, b, 0, 0, 0))
    else:
        c_spec = pl.BlockSpec((None,) + state_block, lambda b, c: (j, b, 0, 0, 0))
    tok = lambda n: pl.BlockSpec((CHUNK, n), lambda b, c: (b * nc + c, 0))
    const = lambda shape: pl.BlockSpec(shape, lambda b, c: (0,) * len(shape))
    in_specs = [tok(4 * D_MODEL), tok(LANES),
                pl.BlockSpec((GATE_ROWS, CHUNK), lambda b, c: (0, b * nc + c)),
                tok(D_MODEL), const((1, LANES)), const((GATE_ROWS, 1)), const((1, D_MODEL)),
                _resident((None, D_MODEL, D_MODEL), lambda b, c: (j, 0, 0))]
    args = [p, gate, gate_t, x, bias_col, bias_row, g_head, wo_all]
    aliases = {3: 0}
    if c_prev is not None:
        in_specs.append(pl.BlockSpec(memory_space=pl.ANY))
        args.append(c_prev)
        aliases[len(args) - 1] = 1
    return pl.pallas_call(
        functools.partial(_mlstm_prompt_kernel, layer=j, first=first),
        grid=(BATCH, nc),
        in_specs=in_specs,
        out_specs=[tok(D_MODEL), c_spec,
                   pl.BlockSpec((None, N_HEADS, HEAD_DIM), lambda b, c: (b, 0, 0)),
                   pl.BlockSpec((None, GATE_ROWS, LANES), lambda b, c: (b, 0, 0))],
        out_shape=[jax.ShapeDtypeStruct((N_ROWS, D_MODEL), F32),
                   jax.ShapeDtypeStruct((n_layers, BATCH, N_HEADS, HEAD_DIM, HEAD_DIM), F32),
                   jax.ShapeDtypeStruct((BATCH, N_HEADS, HEAD_DIM), F32),
                   jax.ShapeDtypeStruct((BATCH, GATE_ROWS, LANES), F32)],
        scratch_shapes=[pltpu.VMEM((N_HEADS, HEAD_DIM, HEAD_DIM), F32),
                        pltpu.VMEM((GATE_ROWS, HEAD_DIM), F32),
                        pltpu.VMEM((GATE_ROWS, LANES), F32),
                        pltpu.VMEM((CHUNK, D_MODEL), BF16)],
        input_output_aliases=aliases,
        compiler_params=_params("arbitrary", "arbitrary"),
        name="mlstm_prompt",
    )(*args)


MLSTM_SAMPLE_BATCH = 4


def _mlstm_sample_kernel(p_ref, gc_ref, gr_ref, bc_ref, br_ref, gh_ref, ci_ref, ni_ref, mi_ref, *rest,
                         layer, first):
    y_ref, co_ref, no_ref, mo_ref = rest[-4:]
    co_ref = _own_layer(co_ref, layer, first)

    def get_state(b, h, which):
        return {"c": lambda: ci_ref[b, h], "n": lambda: ni_ref[b, h:h + 1, :],
                "m": lambda: mi_ref[b, h:h + 1, 0:1]}[which]()

    def put_state(b, h, c_new, n_new, m_new):
        co_ref[b, h] = c_new
        no_ref[b, h:h + 1, :] = n_new
        mo_ref[b, h:h + 1, :] = jnp.broadcast_to(m_new, (1, LANES))

    def put_y(b, h, y):
        y_ref[b, :, h * HEAD_DIM:(h + 1) * HEAD_DIM] = y

    mo_ref[...] = jnp.zeros(mo_ref.shape, F32)
    seqs = range(MLSTM_SAMPLE_BATCH)
    _mlstm_chunk([p_ref.at[b] for b in seqs], [(gc_ref[b], gr_ref[b]) for b in seqs],
                 bc_ref[...], br_ref[...], gh_ref, DEC_SEQ, get_state, put_state, put_y)


def _mlstm_sample(p3, gate3, gate_t3, bias_col, bias_row, g_head, c_all, n_in, m_in, j, c_prev):
    bs = MLSTM_SAMPLE_BATCH
    first = c_prev is None
    const = lambda shape: pl.BlockSpec(shape, lambda i: (0,) * len(shape))
    tok = lambda n: pl.BlockSpec((bs, DEC_SEQ, n), lambda i: (i, 0, 0))
    state_block = (bs, N_HEADS, HEAD_DIM, HEAD_DIM)
    c_spec = pl.BlockSpec((None,) + state_block, lambda i: (j, i, 0, 0, 0))
    c_out = pl.BlockSpec((c_all.shape[0],) + state_block, lambda i: (0, i, 0, 0, 0)) if first else c_spec
    n_spec = pl.BlockSpec((bs, N_HEADS, HEAD_DIM), lambda i: (i, 0, 0))
    m_spec = pl.BlockSpec((bs, GATE_ROWS, LANES), lambda i: (i, 0, 0))
    in_specs = [tok(4 * D_MODEL), tok(LANES),
                pl.BlockSpec((bs, GATE_ROWS, DEC_SEQ), lambda i: (i, 0, 0)),
                const((1, LANES)), const((GATE_ROWS, 1)), const((1, D_MODEL)),
                c_spec, n_spec, m_spec]
    args = [p3, gate3, gate_t3, bias_col, bias_row, g_head, c_all, n_in, m_in]
    aliases = {}
    if c_prev is not None:
        in_specs.append(pl.BlockSpec(memory_space=pl.ANY))
        args.append(c_prev)
        aliases[len(args) - 1] = 1
    return pl.pallas_call(
        functools.partial(_mlstm_sample_kernel, layer=j, first=first),
        grid=(DEC_BATCH // bs,),
        in_specs=in_specs,
        out_specs=[tok(D_MODEL), c_out, n_spec, m_spec],
        out_shape=[jax.ShapeDtypeStruct((DEC_BATCH, DEC_SEQ, D_MODEL), F32),
                   jax.ShapeDtypeStruct(c_all.shape, F32),
                   jax.ShapeDtypeStruct((DEC_BATCH, N_HEADS, HEAD_DIM), F32),
                   jax.ShapeDtypeStruct((DEC_BATCH, GATE_ROWS, LANES), F32)],
        input_output_aliases=aliases,
        compiler_params=_params("arbitrary"),
        name="mlstm_sample",
    )(*args)


def _norm_matmul_kernel(x_ref, g_ref, w_ref, o_ref):
    o_ref[...] = _dot(_rms(x_ref[...], g_ref[...]).astype(BF16), w_ref[...])


def _norm_matmul_sample(x, g, w_all, l):
    first = N_PROMPT // ROW_TILE
    return pl.pallas_call(
        _norm_matmul_kernel,
        grid=(N_SAMPLE // ROW_TILE,),
        in_specs=[pl.BlockSpec((ROW_TILE, D_MODEL), lambda r: (first + r, 0)),
                  pl.BlockSpec((1, D_MODEL), lambda r: (0, 0)),
                  _resident((None, D_MODEL, D_MODEL), lambda r: (l, 0, 0))],
        out_specs=pl.BlockSpec((ROW_TILE, D_MODEL), lambda r: (r, 0)),
        out_shape=jax.ShapeDtypeStruct((N_SAMPLE, D_MODEL), F32),
        compiler_params=_params("parallel"),
        name="norm_matmul_sample",
    )(x, g, w_all)


def _matmul_residual_kernel(x_ref, y_ref, w_ref, o_ref):
    o_ref[...] = x_ref[...] + _dot(y_ref[...].astype(BF16), w_ref[...])


def _matmul_residual_sample(x, y, w_all, l):
    first = N_PROMPT // ROW_TILE
    xrow = pl.BlockSpec((ROW_TILE, D_MODEL), lambda r: (first + r, 0))
    return pl.pallas_call(
        _matmul_residual_kernel,
        grid=(N_SAMPLE // ROW_TILE,),
        in_specs=[xrow, pl.BlockSpec((ROW_TILE, D_MODEL), lambda r: (r, 0)),
                  _resident((None, D_MODEL, D_MODEL), lambda r: (l, 0, 0))],
        out_specs=xrow,
        out_shape=jax.ShapeDtypeStruct((N_ROWS, D_MODEL), F32),
        input_output_aliases={0: 0},
        compiler_params=_params("parallel"),
        name="matmul_residual_sample",
    )(x, y, w_all)


MEM_KV_BATCH = 2


def _mem_kv_kernel(m_ref, w_ref, k_ref, v_ref, kb_ref, vb_ref):
    kv = _dot(m_ref[...].astype(BF16), w_ref[...])
    kb_ref[...] = kv[:, :D_MODEL].astype(BF16)
    vb_ref[...] = kv[:, D_MODEL:].astype(BF16)
    for b in range(MEM_KV_BATCH):
        rows = slice(b * N_MEM, (b + 1) * N_MEM)
        for h in range(N_XHEADS):
            lo, hi = h * XHEAD_DIM, (h + 1) * XHEAD_DIM
            k_ref[b, :, h, :] = kv[rows, lo:hi]
            v_ref[b, :, h, :] = kv[rows, D_MODEL + lo:D_MODEL + hi]


def _mem_kv(mem, wkv_all):
    bs = MEM_KV_BATCH
    out = pl.BlockSpec((None, bs, N_MEM, N_XHEADS, XHEAD_DIM), lambda l, r: (l, r, 0, 0, 0))
    shape = jax.ShapeDtypeStruct((DEPTH, BATCH, N_MEM, N_XHEADS, XHEAD_DIM), F32)
    out_b = pl.BlockSpec((None, bs * N_MEM, D_MODEL), lambda l, r: (l, r, 0))
    shape_b = jax.ShapeDtypeStruct((DEPTH, BATCH * N_MEM, D_MODEL), BF16)
    return pl.pallas_call(
        _mem_kv_kernel,
        grid=(DEPTH, BATCH // bs),
        in_specs=[pl.BlockSpec((bs * N_MEM, D_MODEL), lambda l, r: (r, 0)),
                  pl.BlockSpec((None, D_MODEL, 2 * D_MODEL), lambda l, r: (l, 0, 0))],
        out_specs=[out, out, out_b, out_b],
        out_shape=[shape, shape, shape_b, shape_b],
        compiler_params=_params("arbitrary", "arbitrary"),
        name="mem_kv",
    )(mem, wkv_all)


def _softmax_rows(s):
    e = jnp.exp(s - jnp.max(s, axis=1, keepdims=True))
    return e / jnp.sum(e, axis=1, keepdims=True)


def _xattn_prompt_kernel(x_ref, g_ref, wq_ref, k_ref, v_ref, wo_ref, o_ref, a_sc):
    x = x_ref[...]
    q = _dot(_rms(x, g_ref[...]).astype(BF16), wq_ref[...])
    cols = [slice(h * XHEAD_DIM, (h + 1) * XHEAD_DIM) for h in range(N_XHEADS)]
    scores = [lax.dot_general(q[:, c].astype(BF16), k_ref[:, c], _NT,
                              preferred_element_type=F32) * (XHEAD_DIM ** -0.5) for c in cols]
    probs = [_softmax_rows(s).astype(BF16) for s in scores]
    for c, p in zip(cols, probs):
        a_sc[:, c] = _dot(p, v_ref[:, c]).astype(BF16)
    o_ref[...] = x + _dot(a_sc[...], wo_ref[...])


def _xattn_prompt(x, g, wq_all, k_all, v_all, wo_all, l):
    nt = SEQ // ROW_TILE
    xrow = pl.BlockSpec((ROW_TILE, D_MODEL), lambda b, t: (b * nt + t, 0))
    kv = pl.BlockSpec((None, N_MEM, D_MODEL), lambda b, t: (l, b, 0))
    w = _resident((None, D_MODEL, D_MODEL), lambda b, t: (l, 0, 0))
    return pl.pallas_call(
        _xattn_prompt_kernel,
        grid=(BATCH, nt),
        in_specs=[xrow, pl.BlockSpec((1, D_MODEL), lambda b, t: (0, 0)), w, kv, kv, w],
        out_specs=xrow,
        out_shape=jax.ShapeDtypeStruct((N_ROWS, D_MODEL), F32),
        scratch_shapes=[pltpu.VMEM((ROW_TILE, D_MODEL), BF16)],
        input_output_aliases={0: 0},
        compiler_params=_params("parallel", "parallel"),
        name="xattn_prompt",
    )(x, g, wq_all, k_all, v_all, wo_all)


XATTN_SAMPLE_BATCH = 4


def _xattn_sample_kernel(q_ref, k_ref, v_ref, o_ref):
    n_q, n_kv = N_XHEADS * DEC_SEQ, N_MEM * N_XHEADS
    q_head = lax.broadcasted_iota(jnp.int32, (n_q, n_kv), 0) // DEC_SEQ
    kv_head = lax.broadcasted_iota(jnp.int32, (n_q, n_kv), 1) % N_XHEADS
    same_head = q_head == kv_head

    seqs = range(XATTN_SAMPLE_BATCH)
    scores = [lax.dot_general(q_ref[b].astype(BF16),
                              k_ref[b].reshape(n_kv, XHEAD_DIM).astype(BF16), _NT,
                              preferred_element_type=F32) * (XHEAD_DIM ** -0.5) for b in seqs]
    probs = [_softmax_rows(jnp.where(same_head, s, -jnp.inf)).astype(BF16) for s in scores]
    for b in seqs:
        o_ref[b] = _dot(probs[b], v_ref[b].reshape(n_kv, XHEAD_DIM).astype(BF16))


def _xattn_sample(q, k_cache, v_cache, l):
    bs = XATTN_SAMPLE_BATCH
    tok = pl.BlockSpec((bs, N_XHEADS * DEC_SEQ, XHEAD_DIM), lambda i: (i, 0, 0))
    kv = pl.BlockSpec((None, bs, N_MEM, N_XHEADS, XHEAD_DIM), lambda i: (l, i, 0, 0, 0))
    return pl.pallas_call(
        _xattn_sample_kernel,
        grid=(DEC_BATCH // bs,),
        in_specs=[tok, kv, kv],
        out_specs=tok,
        out_shape=jax.ShapeDtypeStruct((DEC_BATCH, N_XHEADS * DEC_SEQ, XHEAD_DIM), F32),
        compiler_params=_params("parallel"),
        name="xattn_sample",
    )(q, k_cache, v_cache)


def _heads_major(a):
    a = a.reshape(DEC_BATCH, DEC_SEQ, N_XHEADS, XHEAD_DIM)
    return a.transpose(0, 2, 1, 3).reshape(DEC_BATCH, N_XHEADS * DEC_SEQ, XHEAD_DIM)


def _tokens_major(a):
    a = a.reshape(DEC_BATCH, N_XHEADS, DEC_SEQ, XHEAD_DIM)
    return a.transpose(0, 2, 1, 3).reshape(N_SAMPLE, D_MODEL)


def _pool_prompt_kernel(x_ref, g_ref, win_ref, wgrp_ref, scale_ref, wout_ref,
                        o_ref, buf_ref, ext_sc, z_sc):
    tile = pl.program_id(1)

    @pl.when(tile == 0)
    def _():
        ext_sc[0:POOL_HALO, :] = jnp.zeros((POOL_HALO, D_MODEL), F32)

    x = x_ref[...]
    u = _dot(_rms(x, g_ref[...]).astype(BF16), win_ref[...])
    ext_sc[POOL_HALO:, :] = u
    pos = tile * ROW_TILE + lax.broadcasted_iota(jnp.int32, (ROW_TILE, 1), 0)
    for g, w in enumerate(POOL_WINDOWS):
        lo, hi = g * POOL_GROUP_DIM, (g + 1) * POOL_GROUP_DIM
        assert w & (w - 1) == 0 and w <= POOL_HALO
        win = ext_sc[:, lo:hi]
        span = 1
        while span < w:
            win = win + pltpu.roll(win, span, 0)
            span *= 2
        win = win[POOL_HALO:, :]
        cnt = jnp.minimum(pos + 1, w).astype(F32)
        d = win / cnt - u[:, lo:hi]
        z_sc[:, lo:hi] = (_dot(d.astype(BF16), wgrp_ref[g]) * scale_ref[:, lo:hi]).astype(BF16)
    o_ref[...] = x + _dot(z_sc[...], wout_ref[...])
    tail = ext_sc[ROW_TILE:ROW_TILE + POOL_HALO, :]
    ext_sc[0:POOL_HALO, :] = tail

    @pl.when(tile == pl.num_programs(1) - 1)
    def _():
        buf_ref[...] = tail


def _pool_prompt(x, g, win_all, wgrp_all, scale, wout_all, j):
    nt = SEQ // ROW_TILE
    xrow = pl.BlockSpec((ROW_TILE, D_MODEL), lambda b, t: (b * nt + t, 0))
    vec = pl.BlockSpec((1, D_MODEL), lambda b, t: (0, 0))
    w = _resident((None, D_MODEL, D_MODEL), lambda b, t: (j, 0, 0))
    wgrp = _resident((None, len(POOL_WINDOWS), POOL_GROUP_DIM, POOL_GROUP_DIM),
                     lambda b, t: (j, 0, 0, 0))
    return pl.pallas_call(
        _pool_prompt_kernel,
        grid=(BATCH, nt),
        in_specs=[xrow, vec, w, wgrp, vec, w],
        out_specs=[xrow, pl.BlockSpec((None, POOL_HALO, D_MODEL), lambda b, t: (b, 0, 0))],
        out_shape=[jax.ShapeDtypeStruct((N_ROWS, D_MODEL), F32),
                   jax.ShapeDtypeStruct((BATCH, POOL_HALO, D_MODEL), F32)],
        scratch_shapes=[pltpu.VMEM((POOL_HALO + ROW_TILE, D_MODEL), F32),
                        pltpu.VMEM((ROW_TILE, D_MODEL), BF16)],
        input_output_aliases={0: 0},
        compiler_params=_params("arbitrary", "arbitrary"),
        name="pool_prompt",
    )(x, g, win_all, wgrp_all, scale, wout_all)


def _pool_sample_kernel(x_ref, buf_ref, g_ref, win_ref, wgrp_ref, scale_ref, *rest, layer, first):
    z_ref, nbuf_ref = rest[-2:]
    nbuf_ref = _own_layer(nbuf_ref, layer, first)
    u = _dot(_rms(x_ref[...], g_ref[...]).astype(BF16), win_ref[...])

    def ext(row, lo, hi):
        if row < POOL_BUF:
            return buf_ref[:, row, lo:hi]
        t = row - POOL_BUF
        return u[t * DEC_BATCH:(t + 1) * DEC_BATCH, lo:hi]

    for g, w in enumerate(POOL_WINDOWS):
        lo, hi = g * POOL_GROUP_DIM, (g + 1) * POOL_GROUP_DIM
        ds = []
        for t in range(DEC_SEQ):
            win = ext(POOL_BUF + t, lo, hi)
            for back in range(1, w):
                win = win + ext(POOL_BUF + t - back, lo, hi)
            cnt = float(min(PAST_LEN + 1 + t, w))
            ds.append(win / cnt - ext(POOL_BUF + t, lo, hi))
        d = jnp.concatenate(ds, axis=0).astype(BF16)
        z_ref[:, lo:hi] = _dot(d, wgrp_ref[g]) * scale_ref[:, lo:hi]
    for row in range(POOL_BUF):
        nbuf_ref[:, row, :] = ext(row + DEC_SEQ, 0, D_MODEL)


def _pool_sample(xs_t, buf_all, g, win_all, wgrp_all, scale, j, buf_prev):
    first = buf_prev is None
    buf_block = (DEC_BATCH, POOL_BUF, D_MODEL)
    once = lambda shape, imap: pl.BlockSpec(shape, imap, pipeline_mode=pl.Buffered(1))
    if first:
        nbuf = once((buf_all.shape[0],) + buf_block, lambda i: (0, 0, 0, 0))
    else:
        nbuf = once((None,) + buf_block, lambda i: (j, 0, 0, 0))
    vec = pl.BlockSpec((1, D_MODEL), lambda i: (0, 0))
    buf = once((None, DEC_BATCH, POOL_BUF, D_MODEL), lambda i: (j, 0, 0, 0))
    in_specs = [pl.BlockSpec((N_SAMPLE, D_MODEL), lambda i: (0, 0)), buf, vec,
                once((None, D_MODEL, D_MODEL), lambda i: (j, 0, 0)),
                once((None, len(POOL_WINDOWS), POOL_GROUP_DIM, POOL_GROUP_DIM), lambda i: (j, 0, 0, 0)),
                vec]
    args = [xs_t, buf_all, g, win_all, wgrp_all, scale]
    aliases = {}
    if buf_prev is not None:
        in_specs.append(pl.BlockSpec(memory_space=pl.ANY))
        args.append(buf_prev)
        aliases[len(args) - 1] = 1
    return pl.pallas_call(
        functools.partial(_pool_sample_kernel, layer=j, first=first),
        grid=(1,),
        in_specs=in_specs,
        out_specs=[pl.BlockSpec((N_SAMPLE, D_MODEL), lambda i: (0, 0)), nbuf],
        out_shape=[jax.ShapeDtypeStruct((N_SAMPLE, D_MODEL), F32),
                   jax.ShapeDtypeStruct(buf_all.shape, F32)],
        input_output_aliases=aliases,
        compiler_params=_params("arbitrary"),
        name="pool_sample",
    )(*args)


def _swap_token_seq(a, lead):
    n = a.shape[0] // lead
    return a.reshape(lead, n, a.shape[1]).transpose(1, 0, 2).reshape(a.shape)


def kernel(x_prompt, x_sample, mem_prompt, cache_mem_k, cache_mem_v, state_mlstm_C, state_mlstm_n,
           state_mlstm_m, state_pool_buf, norm_g, final_g, ffn_w_up, ffn_w_down, mlstm_w_in,
           mlstm_b_i, mlstm_b_f, mlstm_g_head, mlstm_w_out, pool_w_in, pool_w_grp, pool_scale,
           pool_w_out, xattn_w_q, xattn_w_kv, xattn_w_o):
    n_mlstm = mlstm_w_in.shape[0]
    wup = ffn_w_up.astype(BF16)
    wdn = ffn_w_down.astype(BF16)
    w_in = mlstm_w_in.astype(BF16)
    w_gate = jnp.pad(mlstm_w_in[:, :, 4 * D_MODEL:],
                     ((0, 0), (0, 0), (0, LANES - 2 * N_HEADS))).astype(BF16)
    w_mout = mlstm_w_out.astype(BF16)
    gate_bias = jnp.concatenate([mlstm_b_i, mlstm_b_f], axis=1).astype(F32)
    p_win = pool_w_in.astype(BF16)
    p_wgrp = pool_w_grp.astype(BF16)
    p_wout = pool_w_out.astype(BF16)
    wq = xattn_w_q.astype(BF16)
    wkv = xattn_w_kv.astype(BF16)
    wo = xattn_w_o.astype(BF16)

    mem_k, mem_v, mem_kb, mem_vb = _mem_kv(mem_prompt.reshape(BATCH * N_MEM, D_MODEL), wkv)

    x = [x_prompt.reshape(N_PROMPT, D_MODEL), x_sample.reshape(N_SAMPLE, D_MODEL)]
    c_p = c_s = buf_s = None
    n_p, m_p, n_s, m_s, buf_p = [], [], [], [], []
    for l in range(DEPTH):
        g = norm_g[l].reshape(4, 1, D_MODEL)
        j = l // 2
        x = _ffn(x if l == 0 else [x], g[0], wup, wdn, l, 0)
        if l % 2 == 0:
            p_p, gate_p, p_s, gate_s = _mlstm_proj(x, g[1], w_in, w_gate, j)
            bias_col = jnp.pad(gate_bias[j], (0, LANES - GATE_ROWS)).reshape(1, LANES)
            bias_row = gate_bias[j].reshape(GATE_ROWS, 1)
            g_head = mlstm_g_head[j].reshape(1, D_MODEL)
            x, c_p, n1, m1 = _mlstm_prompt(x, p_p, gate_p, gate_p[:, :GATE_ROWS].T, bias_col, bias_row,
                                           g_head, w_mout, j, n_mlstm, c_p)
            gate_s = gate_s.reshape(DEC_BATCH, DEC_SEQ, LANES)
            m_in = jnp.broadcast_to(
                jnp.pad(state_mlstm_m[j], ((0, 0), (0, GATE_ROWS - N_HEADS)))[:, :, None],
                (DEC_BATCH, GATE_ROWS, LANES))
            y_s, c_s, n2, m2 = _mlstm_sample(
                p_s.reshape(DEC_BATCH, DEC_SEQ, 4 * D_MODEL), gate_s,
                gate_s[:, :, :GATE_ROWS].transpose(0, 2, 1),
                bias_col, bias_row, g_head, state_mlstm_C, state_mlstm_n[j], m_in, j, c_s)
            x = _matmul_residual_sample(x, y_s.reshape(N_SAMPLE, D_MODEL), w_mout, j)
            n_p.append(n1); m_p.append(m1[:, :N_HEADS, 0])
            n_s.append(n2); m_s.append(m2[:, :N_HEADS, 0])
        else:
            scale = pool_scale[j].reshape(1, D_MODEL)
            x, b1 = _pool_prompt(x, g[1], p_win, p_wgrp, scale, p_wout, j)
            xs_t = _swap_token_seq(x[N_PROMPT:], DEC_BATCH)
            z_t, buf_s = _pool_sample(xs_t, state_pool_buf, g[1], p_win, p_wgrp, scale, j, buf_s)
            x = _matmul_residual_sample(x, _swap_token_seq(z_t, DEC_SEQ), p_wout, j)
            buf_p.append(b1[:, POOL_HALO - POOL_BUF:])
        x = _xattn_prompt(x, g[2], wq, mem_kb, mem_vb, wo, l)
        q_s = _norm_matmul_sample(x, g[2], wq, l)
        a_s = _xattn_sample(_heads_major(q_s), cache_mem_k, cache_mem_v, l)
        x = _matmul_residual_sample(x, _tokens_major(a_s), wo, l)
        x = _ffn([x], g[3], wup, wdn, l, 1,
                 final_g=final_g.reshape(1, D_MODEL) if l == DEPTH - 1 else None)

    y_prompt, y_sample = x
    return (y_prompt.reshape(BATCH, SEQ, D_MODEL), y_sample.reshape(DEC_BATCH, DEC_SEQ, D_MODEL),
            mem_k, mem_v, c_p, jnp.stack(n_p), jnp.stack(m_p),
            c_s, jnp.stack(n_s), jnp.stack(m_s), jnp.stack(buf_p), buf_s)
```

```python
import functools

import jax
import jax.numpy as jnp
from jax import lax
from jax.experimental import pallas as pl
from jax.experimental.pallas import tpu as pltpu

F32 = jnp.float32
BF16 = jnp.bfloat16

D_MODEL = 1024
BATCH = 8
SEQ = 2048
DEPTH = 4
DEC_BATCH = 128
DEC_SEQ = 4
PAST_LEN = 16384
N_HEADS = 4
HEAD_DIM = D_MODEL // N_HEADS
CHUNK = 256
POOL_WINDOWS = (2, 4, 8, 16)
POOL_GROUP_DIM = D_MODEL // len(POOL_WINDOWS)
POOL_BUF = max(POOL_WINDOWS) - 1
POOL_HALO = 16
N_MEM = 256
N_XHEADS = 4
XHEAD_DIM = D_MODEL // N_XHEADS
D_FF = ((8 * D_MODEL // 3 + 127) // 128) * 128
EPS = 1e-6

N_PROMPT = BATCH * SEQ
N_SAMPLE = DEC_BATCH * DEC_SEQ
N_ROWS = N_PROMPT + N_SAMPLE

LANES = 128
GATE_ROWS = 8
ROW_TILE = 512
PROMPT_TILES = N_PROMPT // ROW_TILE
assert N_SAMPLE == ROW_TILE
VMEM_LIMIT = 56 * 1024 * 1024

_NT = (((1,), (1,)), ((), ()))
_TN = (((0,), (0,)), ((), ()))


def _params(*sem):
    return pltpu.CompilerParams(dimension_semantics=sem, vmem_limit_bytes=VMEM_LIMIT)


def _resident(block_shape, index_map):
    return pl.BlockSpec(block_shape, index_map, pipeline_mode=pl.Buffered(1))


def _rms(x, g):
    return x * lax.rsqrt(jnp.mean(x * x, axis=-1, keepdims=True) + EPS) * g


def _log_sigmoid(x):
    return jnp.minimum(x, 0.0) - jnp.log1p(jnp.exp(-jnp.abs(x)))


def _dot(a, b):
    return jnp.dot(a, b, preferred_element_type=F32)


def _ffn_kernel(*refs, first, final):
    is_prompt = pl.program_id(0) < PROMPT_TILES
    if first:
        xp_ref, xs_ref, *refs = refs
        x = jnp.where(is_prompt, xp_ref[...], xs_ref[...])
    else:
        x_ref, *refs = refs
        x = x_ref[...]
    g_ref, wup_ref, wdn_ref, *rest = refs
    xn = _rms(x, g_ref[...]).astype(BF16)
    gate = _dot(xn, wup_ref[:, :D_FF])
    up = _dot(xn, wup_ref[:, D_FF:])
    act = (gate * jax.nn.sigmoid(gate) * up).astype(BF16)
    y = x + 0.5 * _dot(act, wdn_ref[...])
    if not final:
        (o_ref,) = rest
        o_ref[...] = y
        return
    fg_ref, op_ref, os_ref = rest
    y = _rms(y, fg_ref[...])

    @pl.when(is_prompt)
    def _():
        op_ref[...] = y

    @pl.when(jnp.logical_not(is_prompt))
    def _():
        os_ref[...] = y


def _ffn(xs, g, wup_all, wdn_all, l, i, final_g=None):
    first, final = len(xs) == 2, final_g is not None
    row = pl.BlockSpec((ROW_TILE, D_MODEL), lambda r: (r, 0))
    prompt_row = pl.BlockSpec((ROW_TILE, D_MODEL), lambda r: (jnp.minimum(r, PROMPT_TILES - 1), 0))
    sample_row = pl.BlockSpec((ROW_TILE, D_MODEL), lambda r: (0, 0))
    vec = pl.BlockSpec((1, D_MODEL), lambda r: (0, 0))
    in_specs = ([prompt_row, sample_row] if first else [row]) + [
        vec,
        _resident((None, None, D_MODEL, 2 * D_FF), lambda r: (l, i, 0, 0)),
        _resident((None, None, D_FF, D_MODEL), lambda r: (l, i, 0, 0))]
    args = list(xs) + [g, wup_all, wdn_all]
    if final:
        in_specs.append(vec)
        args.append(final_g)
        out_specs = [prompt_row, sample_row]
        out_shape = [jax.ShapeDtypeStruct((N_PROMPT, D_MODEL), F32),
                     jax.ShapeDtypeStruct((N_SAMPLE, D_MODEL), F32)]
    else:
        out_specs = row
        out_shape = jax.ShapeDtypeStruct((N_ROWS, D_MODEL), F32)
    return pl.pallas_call(
        functools.partial(_ffn_kernel, first=first, final=final),
        grid=(N_ROWS // ROW_TILE,),
        in_specs=in_specs,
        out_specs=out_specs,
        out_shape=out_shape,
        compiler_params=_params("arbitrary"),
        name="ffn_final" if final else ("ffn_first" if first else "ffn"),
    )(*args)


def _prompt_rows(n):
    return pl.BlockSpec((ROW_TILE, n), lambda r: (jnp.minimum(r, PROMPT_TILES - 1), 0))


def _sample_rows(n):
    return pl.BlockSpec((ROW_TILE, n), lambda r: (0, 0))


def _mlstm_proj_kernel(x_ref, g_ref, w_ref, wg_ref, pp_ref, gp_ref, ps_ref, gs_ref):
    xn = _rms(x_ref[...], g_ref[...]).astype(BF16)
    is_prompt = pl.program_id(0) < PROMPT_TILES

    @pl.when(is_prompt)
    def _():
        pp_ref[...] = _dot(xn, w_ref[...])
        gp_ref[...] = _dot(xn, wg_ref[...])

    @pl.when(jnp.logical_not(is_prompt))
    def _():
        ps_ref[...] = _dot(xn, w_ref[...])
        gs_ref[...] = _dot(xn, wg_ref[...])


def _mlstm_proj(x, g, w_all, wg_all, j):
    return pl.pallas_call(
        _mlstm_proj_kernel,
        grid=(N_ROWS // ROW_TILE,),
        in_specs=[pl.BlockSpec((ROW_TILE, D_MODEL), lambda r: (r, 0)),
                  pl.BlockSpec((1, D_MODEL), lambda r: (0, 0)),
                  _resident((None, D_MODEL, 4 * D_MODEL), lambda r: (j, 0, 0)),
                  _resident((None, D_MODEL, LANES), lambda r: (j, 0, 0))],
        out_specs=[_prompt_rows(4 * D_MODEL), _prompt_rows(LANES),
                   _sample_rows(4 * D_MODEL), _sample_rows(LANES)],
        out_shape=[jax.ShapeDtypeStruct((N_PROMPT, 4 * D_MODEL), F32),
                   jax.ShapeDtypeStruct((N_PROMPT, LANES), F32),
                   jax.ShapeDtypeStruct((N_SAMPLE, 4 * D_MODEL), F32),
                   jax.ShapeDtypeStruct((N_SAMPLE, LANES), F32)],
        compiler_params=_params("arbitrary"),
        name="mlstm_proj",
    )(x, g, w_all, wg_all)


def _gate_terms(gate_col, gate_row, bias_col, bias_row, length):
    r = lax.broadcasted_iota(jnp.int32, (length, length), 0)
    c = lax.broadcasted_iota(jnp.int32, (length, length), 1)
    mask = r >= c
    lower = mask.astype(F32)
    upper = (r <= c).astype(F32)
    zc = gate_col + bias_col
    lane = lax.broadcasted_iota(jnp.int32, zc.shape, 1)
    log_col = jnp.where(lane < N_HEADS, zc, _log_sigmoid(zc))
    cum_col = jnp.dot(lower, log_col, precision=lax.Precision.HIGHEST,
                      preferred_element_type=F32)
    zr = gate_row + bias_row
    sub = lax.broadcasted_iota(jnp.int32, zr.shape, 0)
    log_row = jnp.where(sub < N_HEADS, zr, _log_sigmoid(zr))
    cum_row = jnp.dot(log_row, upper, precision=lax.Precision.HIGHEST,
                      preferred_element_type=F32)
    return mask, log_col, cum_col, log_row, cum_row


def _mlstm_chunk(p_views, gates, bias_col, bias_row, g_head_ref, length, get_state, put_state, put_y):
    pairs = [(s, h) for s in range(len(p_views)) for h in range(N_HEADS)]
    terms = [_gate_terms(gc, gr, bias_col, bias_row, length) for gc, gr in gates]
    mask = terms[0][0]

    def cols(h):
        return slice(h * HEAD_DIM, (h + 1) * HEAD_DIM)

    def part(s, h, which):
        lo = which * D_MODEL + h * HEAD_DIM
        return p_views[s][:, lo:lo + HEAD_DIM]

    b_col, b_row, i_col, i_row = {}, {}, {}, {}
    for s, h in pairs:
        _, log_col, cum_col, log_row, cum_row = terms[s]
        f = N_HEADS + h
        b_col[s, h], b_row[s, h] = cum_col[:, f:f + 1], cum_row[f:f + 1, :]
        i_col[s, h], i_row[s, h] = log_col[:, h:h + 1], log_row[h:h + 1, :]

    a, dlog, mt, dw, inter = {}, {}, {}, {}, {}
    for sh in pairs:
        a[sh] = b_col[sh] + get_state(*sh, "m")
        dlog[sh] = jnp.where(mask, b_col[sh] - b_row[sh] + i_row[sh], -jnp.inf)
    for sh in pairs:
        mt[sh] = jnp.maximum(a[sh], jnp.max(dlog[sh], axis=1, keepdims=True))
    for sh in pairs:
        dw[sh] = jnp.exp(dlog[sh] - mt[sh])
        inter[sh] = jnp.exp(a[sh] - mt[sh])

    sc, q_c, q_n, vb, k32 = {}, {}, {}, {}, {}
    for s, h in pairs:
        q = part(s, h, 0)
        k32[s, h] = part(s, h, 1) * (HEAD_DIM ** -0.5)
        qb = q.astype(BF16)
        sc[s, h] = lax.dot_general(qb, k32[s, h].astype(BF16), _NT,
                                   preferred_element_type=F32) * dw[s, h]
        q_c[s, h] = _dot(qb, get_state(s, h, "c").astype(BF16))
        q_n[s, h] = jnp.sum(q * get_state(s, h, "n"), axis=1, keepdims=True)
    hid = {}
    for s, h in pairs:
        vb[s, h] = part(s, h, 2).astype(BF16)
        num = inter[s, h] * q_c[s, h] + _dot(sc[s, h].astype(BF16), vb[s, h])
        den = inter[s, h] * q_n[s, h] + jnp.sum(sc[s, h], axis=1, keepdims=True)
        hid[s, h] = num / jnp.maximum(jnp.abs(den), jnp.exp(-mt[s, h]))
    for s, h in pairs:
        hh = hid[s, h]
        hn = hh * lax.rsqrt(jnp.mean(hh * hh, axis=1, keepdims=True) + EPS) * g_head_ref[:, cols(h)]
        put_y(s, h, jax.nn.sigmoid(part(s, h, 3)) * hn)
    for s, h in pairs:
        c_state, n_state, m_state = (get_state(s, h, which) for which in "cnm")
        m_new = mt[s, h][length - 1:length, :]
        b_last = b_col[s, h][length - 1:length, :]
        decay = jnp.exp(b_last + m_state - m_new)
        kw = k32[s, h] * jnp.exp(b_last - b_col[s, h] + i_col[s, h] - m_new)
        c_new = decay * c_state + lax.dot_general(kw.astype(BF16), vb[s, h], _TN,
                                                  preferred_element_type=F32)
        n_new = decay * n_state + jnp.sum(kw, axis=0, keepdims=True)
        put_state(s, h, c_new, n_new, m_new)


def _own_layer(ref, layer, first):
    if not first:
        return ref
    for other in range(ref.shape[0]):
        if other != layer:
            ref[other] = jnp.zeros(ref.shape[1:], ref.dtype)
    return ref.at[layer]


def _mlstm_prompt_kernel(p_ref, gc_ref, gr_ref, x_ref, bc_ref, br_ref, gh_ref, wo_ref, *rest,
                         layer, first):
    xo_ref, co_ref, no_ref, mo_ref, c_sc, n_sc, m_sc, y_sc = rest[-8:]
    chunk = pl.program_id(1)

    @pl.when(chunk == 0)
    def _():
        c_sc[...] = jnp.zeros_like(c_sc)
        n_sc[...] = jnp.zeros_like(n_sc)
        m_sc[...] = jnp.zeros_like(m_sc)

    def get_state(_, h, which):
        return {"c": lambda: c_sc[h], "n": lambda: n_sc[h:h + 1, :],
                "m": lambda: m_sc[h:h + 1, 0:1]}[which]()

    def put_state(_, h, c_new, n_new, m_new):
        c_sc[h] = c_new
        n_sc[h:h + 1, :] = n_new
        m_sc[h:h + 1, :] = jnp.broadcast_to(m_new, (1, LANES))

    def put_y(_, h, y):
        y_sc[:, h * HEAD_DIM:(h + 1) * HEAD_DIM] = y.astype(BF16)

    _mlstm_chunk([p_ref], [(gc_ref[...], gr_ref[...])], bc_ref[...], br_ref[...], gh_ref, CHUNK,
                 get_state, put_state, put_y)
    xo_ref[...] = x_ref[...] + _dot(y_sc[...], wo_ref[...])

    @pl.when(chunk == pl.num_programs(1) - 1)
    def _():
        _own_layer(co_ref, layer, first)[...] = c_sc[...]
        no_ref[...] = n_sc[0:N_HEADS, :]
        mo_ref[...] = m_sc[...]


def _mlstm_prompt(x, p, gate, gate_t, bias_col, bias_row, g_head, wo_all, j, n_layers, c_prev):
    nc = SEQ // CHUNK
    first = c_prev is None
    state_block = (None, N_HEADS, HEAD_DIM, HEAD_DIM)
    if first:
        c_spec = pl.BlockSpec((n_layers,) + state_block, lambda b, c: (0, b, 0, 0, 0))
    else:
        c_spec = pl.BlockSpec((None,) + state_block, lambda b, c: (j, b, 0, 0, 0))
    tok = lambda n: pl.BlockSpec((CHUNK, n), lambda b, c: (b * nc + c, 0))
    const = lambda shape: pl.BlockSpec(shape, lambda b, c: (0,) * len(shape))
    in_specs = [tok(4 * D_MODEL), tok(LANES),
                pl.BlockSpec((GATE_ROWS, CHUNK), lambda b, c: (0, b * nc + c)),
                tok(D_MODEL), const((1, LANES)), const((GATE_ROWS, 1)), const((1, D_MODEL)),
                _resident((None, D_MODEL, D_MODEL), lambda b, c: (j, 0, 0))]
    args = [p, gate, gate_t, x, bias_col, bias_row, g_head, wo_all]
    aliases = {3: 0}
    if c_prev is not None:
        in_specs.append(pl.BlockSpec(memory_space=pl.ANY))
        args.append(c_prev)
        aliases[len(args) - 1] = 1
    return pl.pallas_call(
        functools.partial(_mlstm_prompt_kernel, layer=j, first=first),
        grid=(BATCH, nc),
        in_specs=in_specs,
        out_specs=[tok(D_MODEL), c_spec,
                   pl.BlockSpec((None, N_HEADS, HEAD_DIM), lambda b, c: (b, 0, 0)),
                   pl.BlockSpec((None, GATE_ROWS, LANES), lambda b, c: (b, 0, 0))],
        out_shape=[jax.ShapeDtypeStruct((N_ROWS, D_MODEL), F32),
                   jax.ShapeDtypeStruct((n_layers, BATCH, N_HEADS, HEAD_DIM, HEAD_DIM), F32),
                   jax.ShapeDtypeStruct((BATCH, N_HEADS, HEAD_DIM), F32),
                   jax.ShapeDtypeStruct((BATCH, GATE_ROWS, LANES), F32)],
        scratch_shapes=[pltpu.VMEM((N_HEADS, HEAD_DIM, HEAD_DIM), F32),
                        pltpu.VMEM((GATE_ROWS, HEAD_DIM), F32),
                        pltpu.VMEM((GATE_ROWS, LANES), F32),
                        pltpu.VMEM((CHUNK, D_MODEL), BF16)],
        input_output_aliases=aliases,
        compiler_params=_params("arbitrary", "arbitrary"),
        name="mlstm_prompt",
    )(*args)


MLSTM_SAMPLE_BATCH = 4


def _mlstm_sample_kernel(p_ref, gc_ref, gr_ref, bc_ref, br_ref, gh_ref, ci_ref, ni_ref, mi_ref, *rest,
                         layer, first):
    y_ref, co_ref, no_ref, mo_ref = rest[-4:]
    co_ref = _own_layer(co_ref, layer, first)

    def get_state(b, h, which):
        return {"c": lambda: ci_ref[b, h], "n": lambda: ni_ref[b, h:h + 1, :],
                "m": lambda: mi_ref[b, h:h + 1, 0:1]}[which]()

    def put_state(b, h, c_new, n_new, m_new):
        co_ref[b, h] = c_new
        no_ref[b, h:h + 1, :] = n_new
        mo_ref[b, h:h + 1, :] = jnp.broadcast_to(m_new, (1, LANES))

    def put_y(b, h, y):
        y_ref[b, :, h * HEAD_DIM:(h + 1) * HEAD_DIM] = y

    mo_ref[...] = jnp.zeros(mo_ref.shape, F32)
    seqs = range(MLSTM_SAMPLE_BATCH)
    _mlstm_chunk([p_ref.at[b] for b in seqs], [(gc_ref[b], gr_ref[b]) for b in seqs],
                 bc_ref[...], br_ref[...], gh_ref, DEC_SEQ, get_state, put_state, put_y)


def _mlstm_sample(p3, gate3, gate_t3, bias_col, bias_row, g_head, c_all, n_in, m_in, j, c_prev):
    bs = MLSTM_SAMPLE_BATCH
    first = c_prev is None
    const = lambda shape: pl.BlockSpec(shape, lambda i: (0,) * len(shape))
    tok = lambda n: pl.BlockSpec((bs, DEC_SEQ, n), lambda i: (i, 0, 0))
    state_block = (bs, N_HEADS, HEAD_DIM, HEAD_DIM)
    c_spec = pl.BlockSpec((None,) + state_block, lambda i: (j, i, 0, 0, 0))
    c_out = pl.BlockSpec((c_all.shape[0],) + state_block, lambda i: (0, i, 0, 0, 0)) if first else c_spec
    n_spec = pl.BlockSpec((bs, N_HEADS, HEAD_DIM), lambda i: (i, 0, 0))
    m_spec = pl.BlockSpec((bs, GATE_ROWS, LANES), lambda i: (i, 0, 0))
    in_specs = [tok(4 * D_MODEL), tok(LANES),
                pl.BlockSpec((bs, GATE_ROWS, DEC_SEQ), lambda i: (i, 0, 0)),
                const((1, LANES)), const((GATE_ROWS, 1)), const((1, D_MODEL)),
                c_spec, n_spec, m_spec]
    args = [p3, gate3, gate_t3, bias_col, bias_row, g_head, c_all, n_in, m_in]
    aliases = {}
    if c_prev is not None:
        in_specs.append(pl.BlockSpec(memory_space=pl.ANY))
        args.append(c_prev)
        aliases[len(args) - 1] = 1
    return pl.pallas_call(
        functools.partial(_mlstm_sample_kernel, layer=j, first=first),
        grid=(DEC_BATCH // bs,),
        in_specs=in_specs,
        out_specs=[tok(D_MODEL), c_out, n_spec, m_spec],
        out_shape=[jax.ShapeDtypeStruct((DEC_BATCH, DEC_SEQ, D_MODEL), F32),
                   jax.ShapeDtypeStruct(c_all.shape, F32),
                   jax.ShapeDtypeStruct((DEC_BATCH, N_HEADS, HEAD_DIM), F32),
                   jax.ShapeDtypeStruct((DEC_BATCH, GATE_ROWS, LANES), F32)],
        input_output_aliases=aliases,
        compiler_params=_params("arbitrary"),
        name="mlstm_sample",
    )(*args)


def _norm_matmul_kernel(x_ref, g_ref, w_ref, o_ref):
    o_ref[...] = _dot(_rms(x_ref[...], g_ref[...]).astype(BF16), w_ref[...])


def _norm_matmul_sample(x, g, w_all, l):
    first = N_PROMPT // ROW_TILE
    return pl.pallas_call(
        _norm_matmul_kernel,
        grid=(N_SAMPLE // ROW_TILE,),
        in_specs=[pl.BlockSpec((ROW_TILE, D_MODEL), lambda r: (first + r, 0)),
                  pl.BlockSpec((1, D_MODEL), lambda r: (0, 0)),
                  _resident((None, D_MODEL, D_MODEL), lambda r: (l, 0, 0))],
        out_specs=pl.BlockSpec((ROW_TILE, D_MODEL), lambda r: (r, 0)),
        out_shape=jax.ShapeDtypeStruct((N_SAMPLE, D_MODEL), F32),
        compiler_params=_params("parallel"),
        name="norm_matmul_sample",
    )(x, g, w_all)


def _matmul_residual_kernel(x_ref, y_ref, w_ref, o_ref):
    o_ref[...] = x_ref[...] + _dot(y_ref[...].astype(BF16), w_ref[...])


def _matmul_residual_sample(x, y, w_all, l):
    first = N_PROMPT // ROW_TILE
    xrow = pl.BlockSpec((ROW_TILE, D_MODEL), lambda r: (first + r, 0))
    return pl.pallas_call(
        _matmul_residual_kernel,
        grid=(N_SAMPLE // ROW_TILE,),
        in_specs=[xrow, pl.BlockSpec((ROW_TILE, D_MODEL), lambda r: (r, 0)),
                  _resident((None, D_MODEL, D_MODEL), lambda r: (l, 0, 0))],
        out_specs=xrow,
        out_shape=jax.ShapeDtypeStruct((N_ROWS, D_MODEL), F32),
        input_output_aliases={0: 0},
        compiler_params=_params("parallel"),
        name="matmul_residual_sample",
    )(x, y, w_all)


MEM_KV_BATCH = 2


def _mem_kv_kernel(m_ref, w_ref, k_ref, v_ref, kb_ref, vb_ref):
    kv = _dot(m_ref[...].astype(BF16), w_ref[...])
    kb_ref[...] = kv[:, :D_MODEL].astype(BF16)
    vb_ref[...] = kv[:, D_MODEL:].astype(BF16)
    for b in range(MEM_KV_BATCH):
        rows = slice(b * N_MEM, (b + 1) * N_MEM)
        for h in range(N_XHEADS):
            lo, hi = h * XHEAD_DIM, (h + 1) * XHEAD_DIM
            k_ref[b, :, h, :] = kv[rows, lo:hi]
            v_ref[b, :, h, :] = kv[rows, D_MODEL + lo:D_MODEL + hi]


def _mem_kv(mem, wkv_all):
    bs = MEM_KV_BATCH
    out = pl.BlockSpec((None, bs, N_MEM, N_XHEADS, XHEAD_DIM), lambda l, r: (l, r, 0, 0, 0))
    shape = jax.ShapeDtypeStruct((DEPTH, BATCH, N_MEM, N_XHEADS, XHEAD_DIM), F32)
    out_b = pl.BlockSpec((None, bs * N_MEM, D_MODEL), lambda l, r: (l, r, 0))
    shape_b = jax.ShapeDtypeStruct((DEPTH, BATCH * N_MEM, D_MODEL), BF16)
    return pl.pallas_call(
        _mem_kv_kernel,
        grid=(DEPTH, BATCH // bs),
        in_specs=[pl.BlockSpec((bs * N_MEM, D_MODEL), lambda l, r: (r, 0)),
                  pl.BlockSpec((None, D_MODEL, 2 * D_MODEL), lambda l, r: (l, 0, 0))],
        out_specs=[out, out, out_b, out_b],
        out_shape=[shape, shape, shape_b, shape_b],
        compiler_params=_params("arbitrary", "arbitrary"),
        name="mem_kv",
    )(mem, wkv_all)


def _softmax_rows(s):
    e = jnp.exp(s - jnp.max(s, axis=1, keepdims=True))
    return e / jnp.sum(e, axis=1, keepdims=True)


def _xattn_prompt_kernel(x_ref, g_ref, wq_ref, k_ref, v_ref, wo_ref, o_ref, a_sc):
    x = x_ref[...]
    q = _dot(_rms(x, g_ref[...]).astype(BF16), wq_ref[...])
    cols = [slice(h * XHEAD_DIM, (h + 1) * XHEAD_DIM) for h in range(N_XHEADS)]
    scores = [lax.dot_general(q[:, c].astype(BF16), k_ref[:, c], _NT,
                              preferred_element_type=F32) * (XHEAD_DIM ** -0.5) for c in cols]
    probs = [_softmax_rows(s).astype(BF16) for s in scores]
    for c, p in zip(cols, probs):
        a_sc[:, c] = _dot(p, v_ref[:, c]).astype(BF16)
    o_ref[...] = x + _dot(a_sc[...], wo_ref[...])


def _xattn_prompt(x, g, wq_all, k_all, v_all, wo_all, l):
    nt = SEQ // ROW_TILE
    xrow = pl.BlockSpec((ROW_TILE, D_MODEL), lambda b, t: (b * nt + t, 0))
    kv = pl.BlockSpec((None, N_MEM, D_MODEL), lambda b, t: (l, b, 0))
    w = _resident((None, D_MODEL, D_MODEL), lambda b, t: (l, 0, 0))
    return pl.pallas_call(
        _xattn_prompt_kernel,
        grid=(BATCH, nt),
        in_specs=[xrow, pl.BlockSpec((1, D_MODEL), lambda b, t: (0, 0)), w, kv, kv, w],
        out_specs=xrow,
        out_shape=jax.ShapeDtypeStruct((N_ROWS, D_MODEL), F32),
        scratch_shapes=[pltpu.VMEM((ROW_TILE, D_MODEL), BF16)],
        input_output_aliases={0: 0},
        compiler_params=_params("parallel", "parallel"),
        name="xattn_prompt",
    )(x, g, wq_all, k_all, v_all, wo_all)


XATTN_SAMPLE_BATCH = 4


def _xattn_sample_kernel(q_ref, k_ref, v_ref, o_ref):
    n_q, n_kv = N_XHEADS * DEC_SEQ, N_MEM * N_XHEADS
    q_head = lax.broadcasted_iota(jnp.int32, (n_q, n_kv), 0) // DEC_SEQ
    kv_head = lax.broadcasted_iota(jnp.int32, (n_q, n_kv), 1) % N_XHEADS
    same_head = q_head == kv_head

    seqs = range(XATTN_SAMPLE_BATCH)
    scores = [lax.dot_general(q_ref[b].astype(BF16),
                              k_ref[b].reshape(n_kv, XHEAD_DIM).astype(BF16), _NT,
                              preferred_element_type=F32) * (XHEAD_DIM ** -0.5) for b in seqs]
    probs = [_softmax_rows(jnp.where(same_head, s, -jnp.inf)).astype(BF16) for s in scores]
    for b in seqs:
        o_ref[b] = _dot(probs[b], v_ref[b].reshape(n_kv, XHEAD_DIM).astype(BF16))


def _xattn_sample(q, k_cache, v_cache, l):
    bs = XATTN_SAMPLE_BATCH
    tok = pl.BlockSpec((bs, N_XHEADS * DEC_SEQ, XHEAD_DIM), lambda i: (i, 0, 0))
    kv = pl.BlockSpec((None, bs, N_MEM, N_XHEADS, XHEAD_DIM), lambda i: (l, i, 0, 0, 0))
    return pl.pallas_call(
        _xattn_sample_kernel,
        grid=(DEC_BATCH // bs,),
        in_specs=[tok, kv, kv],
        out_specs=tok,
        out_shape=jax.ShapeDtypeStruct((DEC_BATCH, N_XHEADS * DEC_SEQ, XHEAD_DIM), F32),
        compiler_params=_params("parallel"),
        name="xattn_sample",
    )(q, k_cache, v_cache)


def _heads_major(a):
    a = a.reshape(DEC_BATCH, DEC_SEQ, N_XHEADS, XHEAD_DIM)
    return a.transpose(0, 2, 1, 3).reshape(DEC_BATCH, N_XHEADS * DEC_SEQ, XHEAD_DIM)


def _tokens_major(a):
    a = a.reshape(DEC_BATCH, N_XHEADS, DEC_SEQ, XHEAD_DIM)
    return a.transpose(0, 2, 1, 3).reshape(N_SAMPLE, D_MODEL)


def _pool_prompt_kernel(x_ref, g_ref, win_ref, wgrp_ref, scale_ref, wout_ref,
                        o_ref, buf_ref, ext_sc, z_sc):
    tile = pl.program_id(1)

    @pl.when(tile == 0)
    def _():
        ext_sc[0:POOL_HALO, :] = jnp.zeros((POOL_HALO, D_MODEL), F32)

    x = x_ref[...]
    u = _dot(_rms(x, g_ref[...]).astype(BF16), win_ref[...])
    ext_sc[POOL_HALO:, :] = u
    pos = tile * ROW_TILE + lax.broadcasted_iota(jnp.int32, (ROW_TILE, 1), 0)
    for g, w in enumerate(POOL_WINDOWS):
        lo, hi = g * POOL_GROUP_DIM, (g + 1) * POOL_GROUP_DIM
        assert w & (w - 1) == 0 and w <= POOL_HALO
        win = ext_sc[:, lo:hi]
        span = 1
        while span < w:
            win = win + pltpu.roll(win, span, 0)
            span *= 2
        win = win[POOL_HALO:, :]
        cnt = jnp.minimum(pos + 1, w).astype(F32)
        d = win / cnt - u[:, lo:hi]
        z_sc[:, lo:hi] = (_dot(d.astype(BF16), wgrp_ref[g]) * scale_ref[:, lo:hi]).astype(BF16)
    o_ref[...] = x + _dot(z_sc[...], wout_ref[...])
    tail = ext_sc[ROW_TILE:ROW_TILE + POOL_HALO, :]
    ext_sc[0:POOL_HALO, :] = tail

    @pl.when(tile == pl.num_programs(1) - 1)
    def _():
        buf_ref[...] = tail


def _pool_prompt(x, g, win_all, wgrp_all, scale, wout_all, j):
    nt = SEQ // ROW_TILE
    xrow = pl.BlockSpec((ROW_TILE, D_MODEL), lambda b, t: (b * nt + t, 0))
    vec = pl.BlockSpec((1, D_MODEL), lambda b, t: (0, 0))
    w = _resident((None, D_MODEL, D_MODEL), lambda b, t: (j, 0, 0))
    wgrp = _resident((None, len(POOL_WINDOWS), POOL_GROUP_DIM, POOL_GROUP_DIM),
                     lambda b, t: (j, 0, 0, 0))
    return pl.pallas_call(
        _pool_prompt_kernel,
        grid=(BATCH, nt),
        in_specs=[xrow, vec, w, wgrp, vec, w],
        out_specs=[xrow, pl.BlockSpec((None, POOL_HALO, D_MODEL), lambda b, t: (b, 0, 0))],
        out_shape=[jax.ShapeDtypeStruct((N_ROWS, D_MODEL), F32),
                   jax.ShapeDtypeStruct((BATCH, POOL_HALO, D_MODEL), F32)],
        scratch_shapes=[pltpu.VMEM((POOL_HALO + ROW_TILE, D_MODEL), F32),
                        pltpu.VMEM((ROW_TILE, D_MODEL), BF16)],
        input_output_aliases={0: 0},
        compiler_params=_params("arbitrary", "arbitrary"),
        name="pool_prompt",
    )(x, g, win_all, wgrp_all, scale, wout_all)


def _pool_sample_kernel(x_ref, buf_ref, g_ref, win_ref, wgrp_ref, scale_ref, *rest, layer, first):
    z_ref, nbuf_ref = rest[-2:]
    nbuf_ref = _own_layer(nbuf_ref, layer, first)
    u = _dot(_rms(x_ref[...], g_ref[...]).astype(BF16), win_ref[...])

    def ext(row, lo, hi):
        if row < POOL_BUF:
            return buf_ref[:, row, lo:hi]
        t = row - POOL_BUF
        return u[t * DEC_BATCH:(t + 1) * DEC_BATCH, lo:hi]

    for g, w in enumerate(POOL_WINDOWS):
        lo, hi = g * POOL_GROUP_DIM, (g + 1) * POOL_GROUP_DIM
        ds = []
        for t in range(DEC_SEQ):
            win = ext(POOL_BUF + t, lo, hi)
            for back in range(1, w):
                win = win + ext(POOL_BUF + t - back, lo, hi)
            cnt = float(min(PAST_LEN + 1 + t, w))
            ds.append(win / cnt - ext(POOL_BUF + t, lo, hi))
        d = jnp.concatenate(ds, axis=0).astype(BF16)
        z_ref[:, lo:hi] = _dot(d, wgrp_ref[g]) * scale_ref[:, lo:hi]
    for row in range(POOL_BUF):
        nbuf_ref[:, row, :] = ext(row + DEC_SEQ, 0, D_MODEL)


def _pool_sample(xs_t, buf_all, g, win_all, wgrp_all, scale, j, buf_prev):
    first = buf_prev is None
    buf_block = (DEC_BATCH, POOL_BUF, D_MODEL)
    once = lambda shape, imap: pl.BlockSpec(shape, imap, pipeline_mode=pl.Buffered(1))
    if first:
        nbuf = once((buf_all.shape[0],) + buf_block, lambda i: (0, 0, 0, 0))
    else:
        nbuf = once((None,) + buf_block, lambda i: (j, 0, 0, 0))
    vec = pl.BlockSpec((1, D_MODEL), lambda i: (0, 0))
    buf = once((None, DEC_BATCH, POOL_BUF, D_MODEL), lambda i: (j, 0, 0, 0))
    in_specs = [pl.BlockSpec((N_SAMPLE, D_MODEL), lambda i: (0, 0)), buf, vec,
                once((None, D_MODEL, D_MODEL), lambda i: (j, 0, 0)),
                once((None, len(POOL_WINDOWS), POOL_GROUP_DIM, POOL_GROUP_DIM), lambda i: (j, 0, 0, 0)),
                vec]
    args = [xs_t, buf_all, g, win_all, wgrp_all, scale]
    aliases = {}
    if buf_prev is not None:
        in_specs.append(pl.BlockSpec(memory_space=pl.ANY))
        args.append(buf_prev)
        aliases[len(args) - 1] = 1
    return pl.pallas_call(
        functools.partial(_pool_sample_kernel, layer=j, first=first),
        grid=(1,),
        in_specs=in_specs,
        out_specs=[pl.BlockSpec((N_SAMPLE, D_MODEL), lambda i: (0, 0)), nbuf],
        out_shape=[jax.ShapeDtypeStruct((N_SAMPLE, D_MODEL), F32),
                   jax.ShapeDtypeStruct(buf_all.shape, F32)],
        input_output_aliases=aliases,
        compiler_params=_params("arbitrary"),
        name="pool_sample",
    )(*args)


def _swap_token_seq(a, lead):
    n = a.shape[0] // lead
    return a.reshape(lead, n, a.shape[1]).transpose(1, 0, 2).reshape(a.shape)


def kernel(x_prompt, x_sample, mem_prompt, cache_mem_k, cache_mem_v, state_mlstm_C, state_mlstm_n,
           state_mlstm_m, state_pool_buf, norm_g, final_g, ffn_w_up, ffn_w_down, mlstm_w_in,
           mlstm_b_i, mlstm_b_f, mlstm_g_head, mlstm_w_out, pool_w_in, pool_w_grp, pool_scale,
           pool_w_out, xattn_w_q, xattn_w_kv, xattn_w_o):
    n_mlstm = mlstm_w_in.shape[0]
    wup = ffn_w_up.astype(BF16)
    wdn = ffn_w_down.astype(BF16)
    w_in = mlstm_w_in.astype(BF16)
    w_gate = jnp.pad(mlstm_w_in[:, :, 4 * D_MODEL:],
                     ((0, 0), (0, 0), (0, LANES - 2 * N_HEADS))).astype(BF16)
    w_mout = mlstm_w_out.astype(BF16)
    gate_bias = jnp.concatenate([mlstm_b_i, mlstm_b_f], axis=1).astype(F32)
    p_win = pool_w_in.astype(BF16)
    p_wgrp = pool_w_grp.astype(BF16)
    p_wout = pool_w_out.astype(BF16)
    wq = xattn_w_q.astype(BF16)
    wkv = xattn_w_kv.astype(BF16)
    wo = xattn_w_o.astype(BF16)

    mem_k, mem_v, mem_kb, mem_vb = _mem_kv(mem_prompt.reshape(BATCH * N_MEM, D_MODEL), wkv)

    x = [x_prompt.reshape(N_PROMPT, D_MODEL), x_sample.reshape(N_SAMPLE, D_MODEL)]
    c_p = c_s = buf_s = None
    n_p, m_p, n_s, m_s, buf_p = [], [], [], [], []
    for l in range(DEPTH):
        g = norm_g[l].reshape(4, 1, D_MODEL)
        j = l // 2
        x = _ffn(x if l == 0 else [x], g[0], wup, wdn, l, 0)
        if l % 2 == 0:
            p_p, gate_p, p_s, gate_s = _mlstm_proj(x, g[1], w_in, w_gate, j)
            bias_col = jnp.pad(gate_bias[j], (0, LANES - GATE_ROWS)).reshape(1, LANES)
            bias_row = gate_bias[j].reshape(GATE_ROWS, 1)
            g_head = mlstm_g_head[j].reshape(1, D_MODEL)
            x, c_p, n1, m1 = _mlstm_prompt(x, p_p, gate_p, gate_p[:, :GATE_ROWS].T, bias_col, bias_row,
                                           g_head, w_mout, j, n_mlstm, c_p)
            gate_s = gate_s.reshape(DEC_BATCH, DEC_SEQ, LANES)
            m_in = jnp.broadcast_to(
                jnp.pad(state_mlstm_m[j], ((0, 0), (0, GATE_ROWS - N_HEADS)))[:, :, None],
                (DEC_BATCH, GATE_ROWS, LANES))
            y_s, c_s, n2, m2 = _mlstm_sample(
                p_s.reshape(DEC_BATCH, DEC_SEQ, 4 * D_MODEL), gate_s,
                gate_s[:, :, :GATE_ROWS].transpose(0, 2, 1),
                bias_col, bias_row, g_head, state_mlstm_C, state_mlstm_n[j], m_in, j, c_s)
            x = _matmul_residual_sample(x, y_s.reshape(N_SAMPLE, D_MODEL), w_mout, j)
            n_p.append(n1); m_p.append(m1[:, :N_HEADS, 0])
            n_s.append(n2); m_s.append(m2[:, :N_HEADS, 0])
        else:
            scale = pool_scale[j].reshape(1, D_MODEL)
            x, b1 = _pool_prompt(x, g[1], p_win, p_wgrp, scale, p_wout, j)
            xs_t = _swap_token_seq(x[N_PROMPT:], DEC_BATCH)
            z_t, buf_s = _pool_sample(xs_t, state_pool_buf, g[1], p_win, p_wgrp, scale, j, buf_s)
            x = _matmul_residual_sample(x, _swap_token_seq(z_t, DEC_SEQ), p_wout, j)
            buf_p.append(b1[:, POOL_HALO - POOL_BUF:])
        x = _xattn_prompt(x, g[2], wq, mem_kb, mem_vb, wo, l)
        q_s = _norm_matmul_sample(x, g[2], wq, l)
        a_s = _xattn_sample(_heads_major(q_s), cache_mem_k, cache_mem_v, l)
        x = _matmul_residual_sample(x, _tokens_major(a_s), wo, l)
        x = _ffn([x], g[3], wup, wdn, l, 1,
                 final_g=final_g.reshape(1, D_MODEL) if l == DEPTH - 1 else None)

    y_prompt, y_sample = x
    return (y_prompt.reshape(BATCH, SEQ, D_MODEL), y_sample.reshape(DEC_BATCH, DEC_SEQ, D_MODEL),
            mem_k, mem_v, c_p, jnp.stack(n_p), jnp.stack(m_p),
            c_s, jnp.stack(n_s), jnp.stack(m_s), jnp.stack(buf_p), buf_s)
```

```python
import functools

import jax
import jax.numpy as jnp
from jax import lax
from jax.experimental import pallas as pl
from jax.experimental.pallas import tpu as pltpu

F32 = jnp.float32
BF16 = jnp.bfloat16

D_MODEL = 1024
BATCH = 8
SEQ = 2048
DEPTH = 4
DEC_BATCH = 128
DEC_SEQ = 4
PAST_LEN = 16384
N_HEADS = 4
HEAD_DIM = D_MODEL // N_HEADS
CHUNK = 256
POOL_WINDOWS = (2, 4, 8, 16)
POOL_GROUP_DIM = D_MODEL // len(POOL_WINDOWS)
POOL_BUF = max(POOL_WINDOWS) - 1
POOL_HALO = 16
N_MEM = 256
N_XHEADS = 4
XHEAD_DIM = D_MODEL // N_XHEADS
D_FF = ((8 * D_MODEL // 3 + 127) // 128) * 128
EPS = 1e-6

N_PROMPT = BATCH * SEQ
N_SAMPLE = DEC_BATCH * DEC_SEQ
N_ROWS = N_PROMPT + N_SAMPLE

LANES = 128
GATE_ROWS = 8
ROW_TILE = 512
PROMPT_TILES = N_PROMPT // ROW_TILE
assert N_SAMPLE == ROW_TILE
VMEM_LIMIT = 56 * 1024 * 1024

_NT = (((1,), (1,)), ((), ()))
_TN = (((0,), (0,)), ((), ()))


def _params(*sem):
    return pltpu.CompilerParams(dimension_semantics=sem, vmem_limit_bytes=VMEM_LIMIT)


def _resident(block_shape, index_map):
    return pl.BlockSpec(block_shape, index_map, pipeline_mode=pl.Buffered(1))


def _rms(x, g):
    return x * lax.rsqrt(jnp.mean(x * x, axis=-1, keepdims=True) + EPS) * g


def _log_sigmoid(x):
    return jnp.minimum(x, 0.0) - jnp.log1p(jnp.exp(-jnp.abs(x)))


def _dot(a, b):
    return jnp.dot(a, b, preferred_element_type=F32)


ATTEND_BATCH = DEC_BATCH // PROMPT_TILES
assert ATTEND_BATCH * PROMPT_TILES == DEC_BATCH


def _ffn_kernel(*refs, first, final, attend):
    refs = list(refs)
    step = pl.program_id(0)
    is_prompt = step < PROMPT_TILES
    if first:
        xp_ref, xs_ref = refs[:2]
        del refs[:2]
        x = jnp.where(is_prompt, xp_ref[...], xs_ref[...])
    else:
        x = refs.pop(0)[...]
    g_ref, wup_ref, wdn_ref = refs[:3]
    del refs[:3]
    if attend:
        q_ref, k_ref, v_ref, wo_ref = refs[:4]
        del refs[:4]
        a_sc = refs.pop()

        def attended():
            y = x
            for h in range(N_XHEADS):
                a = a_sc[h].reshape(N_SAMPLE, XHEAD_DIM).astype(BF16)
                y = y + _dot(a, wo_ref[h * XHEAD_DIM:(h + 1) * XHEAD_DIM, :])
            return y

        x = lax.cond(is_prompt, lambda: x, attended)
        seqs = range(ATTEND_BATCH)
        first_seq = jnp.minimum(step, PROMPT_TILES - 1) * ATTEND_BATCH
        outs = _sample_attention([q_ref[b] for b in seqs], [k_ref[b] for b in seqs],
                                 [v_ref[b] for b in seqs])
        for b, o in enumerate(outs):
            for h in range(N_XHEADS):
                a_sc[h, first_seq + b] = o[h * DEC_SEQ:(h + 1) * DEC_SEQ, :]
    rest = refs
    xn = _rms(x, g_ref[...]).astype(BF16)
    gate = _dot(xn, wup_ref[:, :D_FF])
    up = _dot(xn, wup_ref[:, D_FF:])
    act = (gate * jax.nn.sigmoid(gate) * up).astype(BF16)
    y = x + 0.5 * _dot(act, wdn_ref[...])
    if not final:
        (o_ref,) = rest
        o_ref[...] = y
        return
    fg_ref, op_ref, os_ref = rest
    y = _rms(y, fg_ref[...])

    @pl.when(is_prompt)
    def _():
        op_ref[...] = y

    @pl.when(jnp.logical_not(is_prompt))
    def _():
        os_ref[...] = y


def _ffn(xs, g, wup_all, wdn_all, l, i, final_g=None, attend=None):
    first, final = len(xs) == 2, final_g is not None
    row = pl.BlockSpec((ROW_TILE, D_MODEL), lambda r: (r, 0))
    prompt_row = pl.BlockSpec((ROW_TILE, D_MODEL), lambda r: (jnp.minimum(r, PROMPT_TILES - 1), 0))
    sample_row = pl.BlockSpec((ROW_TILE, D_MODEL), lambda r: (0, 0))
    vec = pl.BlockSpec((1, D_MODEL), lambda r: (0, 0))
    in_specs = ([prompt_row, sample_row] if first else [row]) + [
        vec,
        _resident((None, None, D_MODEL, 2 * D_FF), lambda r: (l, i, 0, 0)),
        _resident((None, None, D_FF, D_MODEL), lambda r: (l, i, 0, 0))]
    args = list(xs) + [g, wup_all, wdn_all]
    scratch = []
    if attend is not None:
        bs = ATTEND_BATCH
        tile = lambda r: jnp.minimum(r, PROMPT_TILES - 1)
        cache = pl.BlockSpec((None, bs, N_MEM, N_XHEADS, XHEAD_DIM), lambda r: (l, tile(r), 0, 0, 0))
        in_specs += [pl.BlockSpec((bs, N_XHEADS * DEC_SEQ, XHEAD_DIM), lambda r: (tile(r), 0, 0)),
                     cache, cache, _resident((None, D_MODEL, D_MODEL), lambda r: (l, 0, 0))]
        args += list(attend)
        scratch = [pltpu.VMEM((N_XHEADS, DEC_BATCH, DEC_SEQ, XHEAD_DIM), F32)]
    if final:
        in_specs.append(vec)
        args.append(final_g)
        out_specs = [prompt_row, sample_row]
        out_shape = [jax.ShapeDtypeStruct((N_PROMPT, D_MODEL), F32),
                     jax.ShapeDtypeStruct((N_SAMPLE, D_MODEL), F32)]
    else:
        out_specs = row
        out_shape = jax.ShapeDtypeStruct((N_ROWS, D_MODEL), F32)
    return pl.pallas_call(
        functools.partial(_ffn_kernel, first=first, final=final, attend=attend is not None),
        grid=(N_ROWS // ROW_TILE,),
        in_specs=in_specs,
        out_specs=out_specs,
        out_shape=out_shape,
        scratch_shapes=scratch,
        compiler_params=_params("arbitrary"),
        name=("ffn_final" if final else "ffn_first" if first else "ffn") + ("_attend" if scratch else ""),
    )(*args)


def _prompt_rows(n):
    return pl.BlockSpec((ROW_TILE, n), lambda r: (jnp.minimum(r, PROMPT_TILES - 1), 0))


def _sample_rows(n):
    return pl.BlockSpec((ROW_TILE, n), lambda r: (0, 0))


def _mlstm_proj_kernel(x_ref, g_ref, w_ref, wg_ref, pp_ref, gp_ref, ps_ref, gs_ref):
    xn = _rms(x_ref[...], g_ref[...]).astype(BF16)
    is_prompt = pl.program_id(0) < PROMPT_TILES

    @pl.when(is_prompt)
    def _():
        pp_ref[...] = _dot(xn, w_ref[...])
        gp_ref[...] = _dot(xn, wg_ref[...])

    @pl.when(jnp.logical_not(is_prompt))
    def _():
        ps_ref[...] = _dot(xn, w_ref[...])
        gs_ref[...] = _dot(xn, wg_ref[...])


def _mlstm_proj(x, g, w_all, wg_all, j):
    return pl.pallas_call(
        _mlstm_proj_kernel,
        grid=(N_ROWS // ROW_TILE,),
        in_specs=[pl.BlockSpec((ROW_TILE, D_MODEL), lambda r: (r, 0)),
                  pl.BlockSpec((1, D_MODEL), lambda r: (0, 0)),
                  _resident((None, D_MODEL, 4 * D_MODEL), lambda r: (j, 0, 0)),
                  _resident((None, D_MODEL, LANES), lambda r: (j, 0, 0))],
        out_specs=[_prompt_rows(4 * D_MODEL), _prompt_rows(LANES),
                   _sample_rows(4 * D_MODEL), _sample_rows(LANES)],
        out_shape=[jax.ShapeDtypeStruct((N_PROMPT, 4 * D_MODEL), F32),
                   jax.ShapeDtypeStruct((N_PROMPT, LANES), F32),
                   jax.ShapeDtypeStruct((N_SAMPLE, 4 * D_MODEL), F32),
                   jax.ShapeDtypeStruct((N_SAMPLE, LANES), F32)],
        compiler_params=_params("arbitrary"),
        name="mlstm_proj",
    )(x, g, w_all, wg_all)


def _gate_terms(gate_col, gate_row, bias_col, bias_row, length):
    r = lax.broadcasted_iota(jnp.int32, (length, length), 0)
    c = lax.broadcasted_iota(jnp.int32, (length, length), 1)
    mask = r >= c
    lower = mask.astype(F32)
    upper = (r <= c).astype(F32)
    zc = gate_col + bias_col
    lane = lax.broadcasted_iota(jnp.int32, zc.shape, 1)
    log_col = jnp.where(lane < N_HEADS, zc, _log_sigmoid(zc))
    cum_col = jnp.dot(lower, log_col, precision=lax.Precision.HIGHEST,
                      preferred_element_type=F32)
    zr = gate_row + bias_row
    sub = lax.broadcasted_iota(jnp.int32, zr.shape, 0)
    log_row = jnp.where(sub < N_HEADS, zr, _log_sigmoid(zr))
    cum_row = jnp.dot(log_row, upper, precision=lax.Precision.HIGHEST,
                      preferred_element_type=F32)
    return mask, log_col, cum_col, log_row, cum_row


def _mlstm_chunk(p_views, gates, bias_col, bias_row, g_head_ref, length, get_state, put_state, put_y):
    pairs = [(s, h) for s in range(len(p_views)) for h in range(N_HEADS)]
    terms = [_gate_terms(gc, gr, bias_col, bias_row, length) for gc, gr in gates]
    mask = terms[0][0]

    def cols(h):
        return slice(h * HEAD_DIM, (h + 1) * HEAD_DIM)

    def part(s, h, which):
        lo = which * D_MODEL + h * HEAD_DIM
        return p_views[s][:, lo:lo + HEAD_DIM]

    b_col, b_row, i_col, i_row = {}, {}, {}, {}
    for s, h in pairs:
        _, log_col, cum_col, log_row, cum_row = terms[s]
        f = N_HEADS + h
        b_col[s, h], b_row[s, h] = cum_col[:, f:f + 1], cum_row[f:f + 1, :]
        i_col[s, h], i_row[s, h] = log_col[:, h:h + 1], log_row[h:h + 1, :]

    a, dlog, mt, dw, inter = {}, {}, {}, {}, {}
    for sh in pairs:
        a[sh] = b_col[sh] + get_state(*sh, "m")
        dlog[sh] = jnp.where(mask, b_col[sh] - b_row[sh] + i_row[sh], -jnp.inf)
    for sh in pairs:
        mt[sh] = jnp.maximum(a[sh], jnp.max(dlog[sh], axis=1, keepdims=True))
    for sh in pairs:
        dw[sh] = jnp.exp(dlog[sh] - mt[sh])
        inter[sh] = jnp.exp(a[sh] - mt[sh])

    sc, q_c, q_n, vb, k32 = {}, {}, {}, {}, {}
    for s, h in pairs:
        q = part(s, h, 0)
        k32[s, h] = part(s, h, 1) * (HEAD_DIM ** -0.5)
        qb = q.astype(BF16)
        sc[s, h] = lax.dot_general(qb, k32[s, h].astype(BF16), _NT,
                                   preferred_element_type=F32) * dw[s, h]
        q_c[s, h] = _dot(qb, get_state(s, h, "c").astype(BF16))
        q_n[s, h] = jnp.sum(q * get_state(s, h, "n"), axis=1, keepdims=True)
    hid = {}
    for s, h in pairs:
        vb[s, h] = part(s, h, 2).astype(BF16)
        num = inter[s, h] * q_c[s, h] + _dot(sc[s, h].astype(BF16), vb[s, h])
        den = inter[s, h] * q_n[s, h] + jnp.sum(sc[s, h], axis=1, keepdims=True)
        hid[s, h] = num / jnp.maximum(jnp.abs(den), jnp.exp(-mt[s, h]))
    for s, h in pairs:
        hh = hid[s, h]
        hn = hh * lax.rsqrt(jnp.mean(hh * hh, axis=1, keepdims=True) + EPS) * g_head_ref[:, cols(h)]
        put_y(s, h, jax.nn.sigmoid(part(s, h, 3)) * hn)
    for s, h in pairs:
        c_state, n_state, m_state = (get_state(s, h, which) for which in "cnm")
        m_new = mt[s, h][length - 1:length, :]
        b_last = b_col[s, h][length - 1:length, :]
        decay = jnp.exp(b_last + m_state - m_new)
        kw = k32[s, h] * jnp.exp(b_last - b_col[s, h] + i_col[s, h] - m_new)
        c_new = decay * c_state + lax.dot_general(kw.astype(BF16), vb[s, h], _TN,
                                                  preferred_element_type=F32)
        n_new = decay * n_state + jnp.sum(kw, axis=0, keepdims=True)
        put_state(s, h, c_new, n_new, m_new)


def _own_layer(ref, layer, first):
    if not first:
        return ref
    for other in range(ref.shape[0]):
        if other != layer:
            ref[other] = jnp.zeros(ref.shape[1:], ref.dtype)
    return ref.at[layer]


def _mlstm_prompt_kernel(p_ref, gc_ref, gr_ref, x_ref, bc_ref, br_ref, gh_ref, wo_ref, *rest,
                         layer, first):
    xo_ref, co_ref, no_ref, mo_ref, c_sc, n_sc, m_sc, y_sc = rest[-8:]
    chunk = pl.program_id(1)

    @pl.when(chunk == 0)
    def _():
        c_sc[...] = jnp.zeros_like(c_sc)
        n_sc[...] = jnp.zeros_like(n_sc)
        m_sc[...] = jnp.zeros_like(m_sc)

    def get_state(_, h, which):
        return {"c": lambda: c_sc[h], "n": lambda: n_sc[h:h + 1, :],
                "m": lambda: m_sc[h:h + 1, 0:1]}[which]()

    def put_state(_, h, c_new, n_new, m_new):
        c_sc[h] = c_new
        n_sc[h:h + 1, :] = n_new
        m_sc[h:h + 1, :] = jnp.broadcast_to(m_new, (1, LANES))

    def put_y(_, h, y):
        y_sc[:, h * HEAD_DIM:(h + 1) * HEAD_DIM] = y.astype(BF16)

    _mlstm_chunk([p_ref], [(gc_ref[...], gr_ref[...])], bc_ref[...], br_ref[...], gh_ref, CHUNK,
                 get_state, put_state, put_y)
    xo_ref[...] = x_ref[...] + _dot(y_sc[...], wo_ref[...])

    @pl.when(chunk == pl.num_programs(1) - 1)
    def _():
        _own_layer(co_ref, layer, first)[...] = c_sc[...]
        no_ref[...] = n_sc[0:N_HEADS, :]
        mo_ref[...] = m_sc[...]


def _mlstm_prompt(x, p, gate, gate_t, bias_col, bias_row, g_head, wo_all, j, n_layers, c_prev):
    nc = SEQ // CHUNK
    first = c_prev is None
    state_block = (None, N_HEADS, HEAD_DIM, HEAD_DIM)
    if first:
        c_spec = pl.BlockSpec((n_layers,) + state_block, lambda b, c: (0, b, 0, 0, 0))
    else:
        c_spec = pl.BlockSpec((None,) + state_block, lambda b, c: (j, b, 0, 0, 0))
    tok = lambda n: pl.BlockSpec((CHUNK, n), lambda b, c: (b * nc + c, 0))
    const = lambda shape: pl.BlockSpec(shape, lambda b, c: (0,) * len(shape))
    in_specs = [tok(4 * D_MODEL), tok(LANES),
                pl.BlockSpec((GATE_ROWS, CHUNK), lambda b, c: (0, b * nc + c)),
                tok(D_MODEL), const((1, LANES)), const((GATE_ROWS, 1)), const((1, D_MODEL)),
                _resident((None, D_MODEL, D_MODEL), lambda b, c: (j, 0, 0))]
    args = [p, gate, gate_t, x, bias_col, bias_row, g_head, wo_all]
    aliases = {3: 0}
    if c_prev is not None:
        in_specs.append(pl.BlockSpec(memory_space=pl.ANY))
        args.append(c_prev)
        aliases[len(args) - 1] = 1
    return pl.pallas_call(
        functools.partial(_mlstm_prompt_kernel, layer=j, first=first),
        grid=(BATCH, nc),
        in_specs=in_specs,
        out_specs=[tok(D_MODEL), c_spec,
                   pl.BlockSpec((None, N_HEADS, HEAD_DIM), lambda b, c: (b, 0, 0)),
                   pl.BlockSpec((None, GATE_ROWS, LANES), lambda b, c: (b, 0, 0))],
        out_shape=[jax.ShapeDtypeStruct((N_ROWS, D_MODEL), F32),
                   jax.ShapeDtypeStruct((n_layers, BATCH, N_HEADS, HEAD_DIM, HEAD_DIM), F32),
                   jax.ShapeDtypeStruct((BATCH, N_HEADS, HEAD_DIM), F32),
                   jax.ShapeDtypeStruct((BATCH, GATE_ROWS, LANES), F32)],
        scratch_shapes=[pltpu.VMEM((N_HEADS, HEAD_DIM, HEAD_DIM), F32),
                        pltpu.VMEM((GATE_ROWS, HEAD_DIM), F32),
                        pltpu.VMEM((GATE_ROWS, LANES), F32),
                        pltpu.VMEM((CHUNK, D_MODEL), BF16)],
        input_output_aliases=aliases,
        compiler_params=_params("arbitrary", "arbitrary"),
        name="mlstm_prompt",
    )(*args)


MLSTM_SAMPLE_BATCH = 4


def _mlstm_sample_kernel(p_ref, gc_ref, gr_ref, bc_ref, br_ref, gh_ref, ci_ref, ni_ref, mi_ref, *rest,
                         layer, first):
    y_ref, co_ref, no_ref, mo_ref = rest[-4:]
    co_ref = _own_layer(co_ref, layer, first)

    def get_state(b, h, which):
        return {"c": lambda: ci_ref[b, h], "n": lambda: ni_ref[b, h:h + 1, :],
                "m": lambda: mi_ref[b, h:h + 1, 0:1]}[which]()

    def put_state(b, h, c_new, n_new, m_new):
        co_ref[b, h] = c_new
        no_ref[b, h:h + 1, :] = n_new
        mo_ref[b, h:h + 1, :] = jnp.broadcast_to(m_new, (1, LANES))

    def put_y(b, h, y):
        y_ref[b, :, h * HEAD_DIM:(h + 1) * HEAD_DIM] = y

    mo_ref[...] = jnp.zeros(mo_ref.shape, F32)
    seqs = range(MLSTM_SAMPLE_BATCH)
    _mlstm_chunk([p_ref.at[b] for b in seqs], [(gc_ref[b], gr_ref[b]) for b in seqs],
                 bc_ref[...], br_ref[...], gh_ref, DEC_SEQ, get_state, put_state, put_y)


def _mlstm_sample(p3, gate3, gate_t3, bias_col, bias_row, g_head, c_all, n_in, m_in, j, c_prev):
    bs = MLSTM_SAMPLE_BATCH
    first = c_prev is None
    const = lambda shape: pl.BlockSpec(shape, lambda i: (0,) * len(shape))
    tok = lambda n: pl.BlockSpec((bs, DEC_SEQ, n), lambda i: (i, 0, 0))
    state_block = (bs, N_HEADS, HEAD_DIM, HEAD_DIM)
    c_spec = pl.BlockSpec((None,) + state_block, lambda i: (j, i, 0, 0, 0))
    c_out = pl.BlockSpec((c_all.shape[0],) + state_block, lambda i: (0, i, 0, 0, 0)) if first else c_spec
    n_spec = pl.BlockSpec((bs, N_HEADS, HEAD_DIM), lambda i: (i, 0, 0))
    m_spec = pl.BlockSpec((bs, GATE_ROWS, LANES), lambda i: (i, 0, 0))
    in_specs = [tok(4 * D_MODEL), tok(LANES),
                pl.BlockSpec((bs, GATE_ROWS, DEC_SEQ), lambda i: (i, 0, 0)),
                const((1, LANES)), const((GATE_ROWS, 1)), const((1, D_MODEL)),
                c_spec, n_spec, m_spec]
    args = [p3, gate3, gate_t3, bias_col, bias_row, g_head, c_all, n_in, m_in]
    aliases = {}
    if c_prev is not None:
        in_specs.append(pl.BlockSpec(memory_space=pl.ANY))
        args.append(c_prev)
        aliases[len(args) - 1] = 1
    return pl.pallas_call(
        functools.partial(_mlstm_sample_kernel, layer=j, first=first),
        grid=(DEC_BATCH // bs,),
        in_specs=in_specs,
        out_specs=[tok(D_MODEL), c_out, n_spec, m_spec],
        out_shape=[jax.ShapeDtypeStruct((DEC_BATCH, DEC_SEQ, D_MODEL), F32),
                   jax.ShapeDtypeStruct(c_all.shape, F32),
                   jax.ShapeDtypeStruct((DEC_BATCH, N_HEADS, HEAD_DIM), F32),
                   jax.ShapeDtypeStruct((DEC_BATCH, GATE_ROWS, LANES), F32)],
        input_output_aliases=aliases,
        compiler_params=_params("arbitrary"),
        name="mlstm_sample",
    )(*args)


def _norm_matmul_kernel(x_ref, g_ref, w_ref, o_ref):
    o_ref[...] = _dot(_rms(x_ref[...], g_ref[...]).astype(BF16), w_ref[...])


def _norm_matmul_sample(x, g, w_all, l):
    first = N_PROMPT // ROW_TILE
    return pl.pallas_call(
        _norm_matmul_kernel,
        grid=(N_SAMPLE // ROW_TILE,),
        in_specs=[pl.BlockSpec((ROW_TILE, D_MODEL), lambda r: (first + r, 0)),
                  pl.BlockSpec((1, D_MODEL), lambda r: (0, 0)),
                  _resident((None, D_MODEL, D_MODEL), lambda r: (l, 0, 0))],
        out_specs=pl.BlockSpec((ROW_TILE, D_MODEL), lambda r: (r, 0)),
        out_shape=jax.ShapeDtypeStruct((N_SAMPLE, D_MODEL), F32),
        compiler_params=_params("parallel"),
        name="norm_matmul_sample",
    )(x, g, w_all)


def _matmul_residual_kernel(x_ref, y_ref, w_ref, o_ref):
    o_ref[...] = x_ref[...] + _dot(y_ref[...].astype(BF16), w_ref[...])


def _matmul_residual_sample(x, y, w_all, l):
    first = N_PROMPT // ROW_TILE
    xrow = pl.BlockSpec((ROW_TILE, D_MODEL), lambda r: (first + r, 0))
    return pl.pallas_call(
        _matmul_residual_kernel,
        grid=(N_SAMPLE // ROW_TILE,),
        in_specs=[xrow, pl.BlockSpec((ROW_TILE, D_MODEL), lambda r: (r, 0)),
                  _resident((None, D_MODEL, D_MODEL), lambda r: (l, 0, 0))],
        out_specs=xrow,
        out_shape=jax.ShapeDtypeStruct((N_ROWS, D_MODEL), F32),
        input_output_aliases={0: 0},
        compiler_params=_params("parallel"),
        name="matmul_residual_sample",
    )(x, y, w_all)


MEM_KV_BATCH = 2


def _mem_kv_kernel(m_ref, w_ref, k_ref, v_ref, kb_ref, vb_ref):
    kv = _dot(m_ref[...].astype(BF16), w_ref[...])
    kb_ref[...] = kv[:, :D_MODEL].astype(BF16)
    vb_ref[...] = kv[:, D_MODEL:].astype(BF16)
    for b in range(MEM_KV_BATCH):
        rows = slice(b * N_MEM, (b + 1) * N_MEM)
        for h in range(N_XHEADS):
            lo, hi = h * XHEAD_DIM, (h + 1) * XHEAD_DIM
            k_ref[b, :, h, :] = kv[rows, lo:hi]
            v_ref[b, :, h, :] = kv[rows, D_MODEL + lo:D_MODEL + hi]


def _mem_kv(mem, wkv_all):
    bs = MEM_KV_BATCH
    out = pl.BlockSpec((None, bs, N_MEM, N_XHEADS, XHEAD_DIM), lambda l, r: (l, r, 0, 0, 0))
    shape = jax.ShapeDtypeStruct((DEPTH, BATCH, N_MEM, N_XHEADS, XHEAD_DIM), F32)
    out_b = pl.BlockSpec((None, bs * N_MEM, D_MODEL), lambda l, r: (l, r, 0))
    shape_b = jax.ShapeDtypeStruct((DEPTH, BATCH * N_MEM, D_MODEL), BF16)
    return pl.pallas_call(
        _mem_kv_kernel,
        grid=(DEPTH, BATCH // bs),
        in_specs=[pl.BlockSpec((bs * N_MEM, D_MODEL), lambda l, r: (r, 0)),
                  pl.BlockSpec((None, D_MODEL, 2 * D_MODEL), lambda l, r: (l, 0, 0))],
        out_specs=[out, out, out_b, out_b],
        out_shape=[shape, shape, shape_b, shape_b],
        compiler_params=_params("arbitrary", "arbitrary"),
        name="mem_kv",
    )(mem, wkv_all)


def _softmax_rows(s):
    e = jnp.exp(s - jnp.max(s, axis=1, keepdims=True))
    return e / jnp.sum(e, axis=1, keepdims=True)


def _xattn_prompt_kernel(x_ref, g_ref, wq_ref, k_ref, v_ref, wo_ref, o_ref, a_sc):
    x = x_ref[...]
    q = _dot(_rms(x, g_ref[...]).astype(BF16), wq_ref[...])
    cols = [slice(h * XHEAD_DIM, (h + 1) * XHEAD_DIM) for h in range(N_XHEADS)]
    scores = [lax.dot_general(q[:, c].astype(BF16), k_ref[:, c], _NT,
                              preferred_element_type=F32) * (XHEAD_DIM ** -0.5) for c in cols]
    probs = [_softmax_rows(s).astype(BF16) for s in scores]
    for c, p in zip(cols, probs):
        a_sc[:, c] = _dot(p, v_ref[:, c]).astype(BF16)
    o_ref[...] = x + _dot(a_sc[...], wo_ref[...])


def _xattn_prompt(x, g, wq_all, k_all, v_all, wo_all, l):
    nt = SEQ // ROW_TILE
    xrow = pl.BlockSpec((ROW_TILE, D_MODEL), lambda b, t: (b * nt + t, 0))
    kv = pl.BlockSpec((None, N_MEM, D_MODEL), lambda b, t: (l, b, 0))
    w = _resident((None, D_MODEL, D_MODEL), lambda b, t: (l, 0, 0))
    return pl.pallas_call(
        _xattn_prompt_kernel,
        grid=(BATCH, nt),
        in_specs=[xrow, pl.BlockSpec((1, D_MODEL), lambda b, t: (0, 0)), w, kv, kv, w],
        out_specs=xrow,
        out_shape=jax.ShapeDtypeStruct((N_ROWS, D_MODEL), F32),
        scratch_shapes=[pltpu.VMEM((ROW_TILE, D_MODEL), BF16)],
        input_output_aliases={0: 0},
        compiler_params=_params("parallel", "parallel"),
        name="xattn_prompt",
    )(x, g, wq_all, k_all, v_all, wo_all)


def _sample_attention(qs, ks, vs):
    n_q, n_kv = N_XHEADS * DEC_SEQ, N_MEM * N_XHEADS
    q_head = lax.broadcasted_iota(jnp.int32, (n_q, n_kv), 0) // DEC_SEQ
    kv_head = lax.broadcasted_iota(jnp.int32, (n_q, n_kv), 1) % N_XHEADS
    same_head = q_head == kv_head
    scores = [lax.dot_general(q.astype(BF16), k.reshape(n_kv, XHEAD_DIM).astype(BF16), _NT,
                              preferred_element_type=F32) * (XHEAD_DIM ** -0.5)
              for q, k in zip(qs, ks)]
    probs = [_softmax_rows(jnp.where(same_head, s, -jnp.inf)).astype(BF16) for s in scores]
    return [_dot(p, v.reshape(n_kv, XHEAD_DIM).astype(BF16)) for p, v in zip(probs, vs)]


def _heads_major(a):
    a = a.reshape(DEC_BATCH, DEC_SEQ, N_XHEADS, XHEAD_DIM)
    return a.transpose(0, 2, 1, 3).reshape(DEC_BATCH, N_XHEADS * DEC_SEQ, XHEAD_DIM)


def _pool_prompt_kernel(x_ref, g_ref, win_ref, wgrp_ref, scale_ref, wout_ref,
                        o_ref, buf_ref, ext_sc, z_sc):
    tile = pl.program_id(1)

    @pl.when(tile == 0)
    def _():
        ext_sc[0:POOL_HALO, :] = jnp.zeros((POOL_HALO, D_MODEL), F32)

    x = x_ref[...]
    u = _dot(_rms(x, g_ref[...]).astype(BF16), win_ref[...])
    ext_sc[POOL_HALO:, :] = u
    pos = tile * ROW_TILE + lax.broadcasted_iota(jnp.int32, (ROW_TILE, 1), 0)
    for g, w in enumerate(POOL_WINDOWS):
        lo, hi = g * POOL_GROUP_DIM, (g + 1) * POOL_GROUP_DIM
        assert w & (w - 1) == 0 and w <= POOL_HALO
        win = ext_sc[:, lo:hi]
        span = 1
        while span < w:
            win = win + pltpu.roll(win, span, 0)
            span *= 2
        win = win[POOL_HALO:, :]
        cnt = jnp.minimum(pos + 1, w).astype(F32)
        d = win / cnt - u[:, lo:hi]
        z_sc[:, lo:hi] = (_dot(d.astype(BF16), wgrp_ref[g]) * scale_ref[:, lo:hi]).astype(BF16)
    o_ref[...] = x + _dot(z_sc[...], wout_ref[...])
    tail = ext_sc[ROW_TILE:ROW_TILE + POOL_HALO, :]
    ext_sc[0:POOL_HALO, :] = tail

    @pl.when(tile == pl.num_programs(1) - 1)
    def _():
        buf_ref[...] = tail


def _pool_prompt(x, g, win_all, wgrp_all, scale, wout_all, j):
    nt = SEQ // ROW_TILE
    xrow = pl.BlockSpec((ROW_TILE, D_MODEL), lambda b, t: (b * nt + t, 0))
    vec = pl.BlockSpec((1, D_MODEL), lambda b, t: (0, 0))
    w = _resident((None, D_MODEL, D_MODEL), lambda b, t: (j, 0, 0))
    wgrp = _resident((None, len(POOL_WINDOWS), POOL_GROUP_DIM, POOL_GROUP_DIM),
                     lambda b, t: (j, 0, 0, 0))
    return pl.pallas_call(
        _pool_prompt_kernel,
        grid=(BATCH, nt),
        in_specs=[xrow, vec, w, wgrp, vec, w],
        out_specs=[xrow, pl.BlockSpec((None, POOL_HALO, D_MODEL), lambda b, t: (b, 0, 0))],
        out_shape=[jax.ShapeDtypeStruct((N_ROWS, D_MODEL), F32),
                   jax.ShapeDtypeStruct((BATCH, POOL_HALO, D_MODEL), F32)],
        scratch_shapes=[pltpu.VMEM((POOL_HALO + ROW_TILE, D_MODEL), F32),
                        pltpu.VMEM((ROW_TILE, D_MODEL), BF16)],
        input_output_aliases={0: 0},
        compiler_params=_params("arbitrary", "arbitrary"),
        name="pool_prompt",
    )(x, g, win_all, wgrp_all, scale, wout_all)


def _pool_sample_kernel(x_ref, buf_ref, g_ref, win_ref, wgrp_ref, scale_ref, *rest, layer, first):
    z_ref, nbuf_ref = rest[-2:]
    nbuf_ref = _own_layer(nbuf_ref, layer, first)
    u = _dot(_rms(x_ref[...], g_ref[...]).astype(BF16), win_ref[...])

    def ext(row, lo, hi):
        if row < POOL_BUF:
            return buf_ref[:, row, lo:hi]
        t = row - POOL_BUF
        return u[t * DEC_BATCH:(t + 1) * DEC_BATCH, lo:hi]

    for g, w in enumerate(POOL_WINDOWS):
        lo, hi = g * POOL_GROUP_DIM, (g + 1) * POOL_GROUP_DIM
        ds = []
        for t in range(DEC_SEQ):
            win = ext(POOL_BUF + t, lo, hi)
            for back in range(1, w):
                win = win + ext(POOL_BUF + t - back, lo, hi)
            cnt = float(min(PAST_LEN + 1 + t, w))
            ds.append(win / cnt - ext(POOL_BUF + t, lo, hi))
        d = jnp.concatenate(ds, axis=0).astype(BF16)
        z_ref[:, lo:hi] = _dot(d, wgrp_ref[g]) * scale_ref[:, lo:hi]
    for row in range(POOL_BUF):
        nbuf_ref[:, row, :] = ext(row + DEC_SEQ, 0, D_MODEL)


def _pool_sample(xs_t, buf_all, g, win_all, wgrp_all, scale, j, buf_prev):
    first = buf_prev is None
    buf_block = (DEC_BATCH, POOL_BUF, D_MODEL)
    once = lambda shape, imap: pl.BlockSpec(shape, imap, pipeline_mode=pl.Buffered(1))
    if first:
        nbuf = once((buf_all.shape[0],) + buf_block, lambda i: (0, 0, 0, 0))
    else:
        nbuf = once((None,) + buf_block, lambda i: (j, 0, 0, 0))
    vec = pl.BlockSpec((1, D_MODEL), lambda i: (0, 0))
    buf = once((None, DEC_BATCH, POOL_BUF, D_MODEL), lambda i: (j, 0, 0, 0))
    in_specs = [pl.BlockSpec((N_SAMPLE, D_MODEL), lambda i: (0, 0)), buf, vec,
                once((None, D_MODEL, D_MODEL), lambda i: (j, 0, 0)),
                once((None, len(POOL_WINDOWS), POOL_GROUP_DIM, POOL_GROUP_DIM), lambda i: (j, 0, 0, 0)),
                vec]
    args = [xs_t, buf_all, g, win_all, wgrp_all, scale]
    aliases = {}
    if buf_prev is not None:
        in_specs.append(pl.BlockSpec(memory_space=pl.ANY))
        args.append(buf_prev)
        aliases[len(args) - 1] = 1
    return pl.pallas_call(
        functools.partial(_pool_sample_kernel, layer=j, first=first),
        grid=(1,),
        in_specs=in_specs,
        out_specs=[pl.BlockSpec((N_SAMPLE, D_MODEL), lambda i: (0, 0)), nbuf],
        out_shape=[jax.ShapeDtypeStruct((N_SAMPLE, D_MODEL), F32),
                   jax.ShapeDtypeStruct(buf_all.shape, F32)],
        input_output_aliases=aliases,
        compiler_params=_params("arbitrary"),
        name="pool_sample",
    )(*args)


def _swap_token_seq(a, lead):
    n = a.shape[0] // lead
    return a.reshape(lead, n, a.shape[1]).transpose(1, 0, 2).reshape(a.shape)


def kernel(x_prompt, x_sample, mem_prompt, cache_mem_k, cache_mem_v, state_mlstm_C, state_mlstm_n,
           state_mlstm_m, state_pool_buf, norm_g, final_g, ffn_w_up, ffn_w_down, mlstm_w_in,
           mlstm_b_i, mlstm_b_f, mlstm_g_head, mlstm_w_out, pool_w_in, pool_w_grp, pool_scale,
           pool_w_out, xattn_w_q, xattn_w_kv, xattn_w_o):
    n_mlstm = mlstm_w_in.shape[0]
    wup = ffn_w_up.astype(BF16)
    wdn = ffn_w_down.astype(BF16)
    w_in = mlstm_w_in.astype(BF16)
    w_gate = jnp.pad(mlstm_w_in[:, :, 4 * D_MODEL:],
                     ((0, 0), (0, 0), (0, LANES - 2 * N_HEADS))).astype(BF16)
    w_mout = mlstm_w_out.astype(BF16)
    gate_bias = jnp.concatenate([mlstm_b_i, mlstm_b_f], axis=1).astype(F32)
    p_win = pool_w_in.astype(BF16)
    p_wgrp = pool_w_grp.astype(BF16)
    p_wout = pool_w_out.astype(BF16)
    wq = xattn_w_q.astype(BF16)
    wkv = xattn_w_kv.astype(BF16)
    wo = xattn_w_o.astype(BF16)

    mem_k, mem_v, mem_kb, mem_vb = _mem_kv(mem_prompt.reshape(BATCH * N_MEM, D_MODEL), wkv)

    x = [x_prompt.reshape(N_PROMPT, D_MODEL), x_sample.reshape(N_SAMPLE, D_MODEL)]
    c_p = c_s = buf_s = None
    n_p, m_p, n_s, m_s, buf_p = [], [], [], [], []
    for l in range(DEPTH):
        g = norm_g[l].reshape(4, 1, D_MODEL)
        j = l // 2
        x = _ffn(x if l == 0 else [x], g[0], wup, wdn, l, 0)
        if l % 2 == 0:
            p_p, gate_p, p_s, gate_s = _mlstm_proj(x, g[1], w_in, w_gate, j)
            bias_col = jnp.pad(gate_bias[j], (0, LANES - GATE_ROWS)).reshape(1, LANES)
            bias_row = gate_bias[j].reshape(GATE_ROWS, 1)
            g_head = mlstm_g_head[j].reshape(1, D_MODEL)
            x, c_p, n1, m1 = _mlstm_prompt(x, p_p, gate_p, gate_p[:, :GATE_ROWS].T, bias_col, bias_row,
                                           g_head, w_mout, j, n_mlstm, c_p)
            gate_s = gate_s.reshape(DEC_BATCH, DEC_SEQ, LANES)
            m_in = jnp.broadcast_to(
                jnp.pad(state_mlstm_m[j], ((0, 0), (0, GATE_ROWS - N_HEADS)))[:, :, None],
                (DEC_BATCH, GATE_ROWS, LANES))
            y_s, c_s, n2, m2 = _mlstm_sample(
                p_s.reshape(DEC_BATCH, DEC_SEQ, 4 * D_MODEL), gate_s,
                gate_s[:, :, :GATE_ROWS].transpose(0, 2, 1),
                bias_col, bias_row, g_head, state_mlstm_C, state_mlstm_n[j], m_in, j, c_s)
            x = _matmul_residual_sample(x, y_s.reshape(N_SAMPLE, D_MODEL), w_mout, j)
            n_p.append(n1); m_p.append(m1[:, :N_HEADS, 0])
            n_s.append(n2); m_s.append(m2[:, :N_HEADS, 0])
        else:
            scale = pool_scale[j].reshape(1, D_MODEL)
            x, b1 = _pool_prompt(x, g[1], p_win, p_wgrp, scale, p_wout, j)
            xs_t = _swap_token_seq(x[N_PROMPT:], DEC_BATCH)
            z_t, buf_s = _pool_sample(xs_t, state_pool_buf, g[1], p_win, p_wgrp, scale, j, buf_s)
            x = _matmul_residual_sample(x, _swap_token_seq(z_t, DEC_SEQ), p_wout, j)
            buf_p.append(b1[:, POOL_HALO - POOL_BUF:])
        x = _xattn_prompt(x, g[2], wq, mem_kb, mem_vb, wo, l)
        q_s = _norm_matmul_sample(x, g[2], wq, l)
        x = _ffn([x], g[3], wup, wdn, l, 1,
                 final_g=final_g.reshape(1, D_MODEL) if l == DEPTH - 1 else None,
                 attend=(_heads_major(q_s), cache_mem_k, cache_mem_v, wo))

    y_prompt, y_sample = x
    return (y_prompt.reshape(BATCH, SEQ, D_MODEL), y_sample.reshape(DEC_BATCH, DEC_SEQ, D_MODEL),
            mem_k, mem_v, c_p, jnp.stack(n_p), jnp.stack(m_p),
            c_s, jnp.stack(n_s), jnp.stack(m_s), jnp.stack(buf_p), buf_s)
```

```python
import functools

import jax
import jax.numpy as jnp
from jax import lax
from jax.experimental import pallas as pl
from jax.experimental.pallas import tpu as pltpu

F32 = jnp.float32
BF16 = jnp.bfloat16

D_MODEL = 1024
BATCH = 8
SEQ = 2048
DEPTH = 4
DEC_BATCH = 128
DEC_SEQ = 4
PAST_LEN = 16384
N_HEADS = 4
HEAD_DIM = D_MODEL // N_HEADS
CHUNK = 256
POOL_WINDOWS = (2, 4, 8, 16)
POOL_GROUP_DIM = D_MODEL // len(POOL_WINDOWS)
POOL_BUF = max(POOL_WINDOWS) - 1
POOL_HALO = 16
N_MEM = 256
N_XHEADS = 4
XHEAD_DIM = D_MODEL // N_XHEADS
D_FF = ((8 * D_MODEL // 3 + 127) // 128) * 128
EPS = 1e-6

N_PROMPT = BATCH * SEQ
N_SAMPLE = DEC_BATCH * DEC_SEQ
N_ROWS = N_PROMPT + N_SAMPLE

LANES = 128
GATE_ROWS = 8
ROW_TILE = 512
SEQ_TILE = 1024
PROMPT_TILES = N_PROMPT // ROW_TILE
assert N_SAMPLE == ROW_TILE
VMEM_LIMIT = 56 * 1024 * 1024

_NT = (((1,), (1,)), ((), ()))
_TN = (((0,), (0,)), ((), ()))


def _params(*sem):
    return pltpu.CompilerParams(dimension_semantics=sem, vmem_limit_bytes=VMEM_LIMIT)


def _resident(block_shape, index_map):
    return pl.BlockSpec(block_shape, index_map, pipeline_mode=pl.Buffered(1))


def _rms(x, g):
    return x * lax.rsqrt(jnp.mean(x * x, axis=-1, keepdims=True) + EPS) * g


def _log_sigmoid(x):
    return jnp.minimum(x, 0.0) - jnp.log1p(jnp.exp(-jnp.abs(x)))


def _dot(a, b):
    return jnp.dot(a, b, preferred_element_type=F32)


ATTEND_BATCH = DEC_BATCH // PROMPT_TILES
assert ATTEND_BATCH * PROMPT_TILES == DEC_BATCH


def _ffn_kernel(*refs, first, final, attend):
    refs = list(refs)
    step = pl.program_id(0)
    is_prompt = step < PROMPT_TILES
    if first:
        xp_ref, xs_ref = refs[:2]
        del refs[:2]
        x = jnp.where(is_prompt, xp_ref[...], xs_ref[...])
    else:
        x = refs.pop(0)[...]
    g_ref, wup_ref, wdn_ref = refs[:3]
    del refs[:3]
    if attend:
        q_ref, k_ref, v_ref, wo_ref = refs[:4]
        del refs[:4]
        a_sc = refs.pop()

        def attended():
            y = x
            for h in range(N_XHEADS):
                a = a_sc[h].reshape(N_SAMPLE, XHEAD_DIM).astype(BF16)
                y = y + _dot(a, wo_ref[h * XHEAD_DIM:(h + 1) * XHEAD_DIM, :])
            return y

        x = lax.cond(is_prompt, lambda: x, attended)
        seqs = range(ATTEND_BATCH)
        first_seq = jnp.minimum(step, PROMPT_TILES - 1) * ATTEND_BATCH
        outs = _sample_attention([q_ref[b] for b in seqs], [k_ref[b] for b in seqs],
                                 [v_ref[b] for b in seqs])
        for b, o in enumerate(outs):
            for h in range(N_XHEADS):
                a_sc[h, first_seq + b] = o[h * DEC_SEQ:(h + 1) * DEC_SEQ, :]
    rest = refs
    xn = _rms(x, g_ref[...]).astype(BF16)
    gate = _dot(xn, wup_ref[:, :D_FF])
    up = _dot(xn, wup_ref[:, D_FF:])
    act = (gate * jax.nn.sigmoid(gate) * up).astype(BF16)
    y = x + 0.5 * _dot(act, wdn_ref[...])
    if not final:
        (o_ref,) = rest
        o_ref[...] = y
        return
    fg_ref, op_ref, os_ref = rest
    y = _rms(y, fg_ref[...])

    @pl.when(is_prompt)
    def _():
        op_ref[...] = y

    @pl.when(jnp.logical_not(is_prompt))
    def _():
        os_ref[...] = y


def _ffn(xs, g, wup_all, wdn_all, l, i, final_g=None, attend=None):
    first, final = len(xs) == 2, final_g is not None
    row = pl.BlockSpec((ROW_TILE, D_MODEL), lambda r: (r, 0))
    prompt_row = pl.BlockSpec((ROW_TILE, D_MODEL), lambda r: (jnp.minimum(r, PROMPT_TILES - 1), 0))
    sample_row = pl.BlockSpec((ROW_TILE, D_MODEL), lambda r: (0, 0))
    vec = pl.BlockSpec((1, D_MODEL), lambda r: (0, 0))
    in_specs = ([prompt_row, sample_row] if first else [row]) + [
        vec,
        _resident((None, None, D_MODEL, 2 * D_FF), lambda r: (l, i, 0, 0)),
        _resident((None, None, D_FF, D_MODEL), lambda r: (l, i, 0, 0))]
    args = list(xs) + [g, wup_all, wdn_all]
    scratch = []
    if attend is not None:
        bs = ATTEND_BATCH
        tile = lambda r: jnp.minimum(r, PROMPT_TILES - 1)
        cache = pl.BlockSpec((None, bs, N_MEM, N_XHEADS, XHEAD_DIM), lambda r: (l, tile(r), 0, 0, 0))
        in_specs += [pl.BlockSpec((bs, N_XHEADS * DEC_SEQ, XHEAD_DIM), lambda r: (tile(r), 0, 0)),
                     cache, cache, _resident((None, D_MODEL, D_MODEL), lambda r: (l, 0, 0))]
        args += list(attend)
        scratch = [pltpu.VMEM((N_XHEADS, DEC_BATCH, DEC_SEQ, XHEAD_DIM), F32)]
    if final:
        in_specs.append(vec)
        args.append(final_g)
        out_specs = [prompt_row, sample_row]
        out_shape = [jax.ShapeDtypeStruct((N_PROMPT, D_MODEL), F32),
                     jax.ShapeDtypeStruct((N_SAMPLE, D_MODEL), F32)]
    else:
        out_specs = row
        out_shape = jax.ShapeDtypeStruct((N_ROWS, D_MODEL), F32)
    return pl.pallas_call(
        functools.partial(_ffn_kernel, first=first, final=final, attend=attend is not None),
        grid=(N_ROWS // ROW_TILE,),
        in_specs=in_specs,
        out_specs=out_specs,
        out_shape=out_shape,
        scratch_shapes=scratch,
        compiler_params=_params("arbitrary"),
        name=("ffn_final" if final else "ffn_first" if first else "ffn") + ("_attend" if scratch else ""),
    )(*args)


def _prompt_rows(n):
    return pl.BlockSpec((ROW_TILE, n), lambda r: (jnp.minimum(r, PROMPT_TILES - 1), 0))


def _sample_rows(n):
    return pl.BlockSpec((ROW_TILE, n), lambda r: (0, 0))


def _mlstm_proj_kernel(x_ref, g_ref, w_ref, wg_ref, pp_ref, gp_ref, ps_ref, gs_ref):
    xn = _rms(x_ref[...], g_ref[...]).astype(BF16)
    is_prompt = pl.program_id(0) < PROMPT_TILES

    @pl.when(is_prompt)
    def _():
        pp_ref[...] = _dot(xn, w_ref[...])
        gp_ref[...] = _dot(xn, wg_ref[...])

    @pl.when(jnp.logical_not(is_prompt))
    def _():
        ps_ref[...] = _dot(xn, w_ref[...])
        gs_ref[...] = _dot(xn, wg_ref[...])


def _mlstm_proj(x, g, w_all, wg_all, j):
    return pl.pallas_call(
        _mlstm_proj_kernel,
        grid=(N_ROWS // ROW_TILE,),
        in_specs=[pl.BlockSpec((ROW_TILE, D_MODEL), lambda r: (r, 0)),
                  pl.BlockSpec((1, D_MODEL), lambda r: (0, 0)),
                  _resident((None, D_MODEL, 4 * D_MODEL), lambda r: (j, 0, 0)),
                  _resident((None, D_MODEL, LANES), lambda r: (j, 0, 0))],
        out_specs=[_prompt_rows(4 * D_MODEL), _prompt_rows(LANES),
                   _sample_rows(4 * D_MODEL), _sample_rows(LANES)],
        out_shape=[jax.ShapeDtypeStruct((N_PROMPT, 4 * D_MODEL), F32),
                   jax.ShapeDtypeStruct((N_PROMPT, LANES), F32),
                   jax.ShapeDtypeStruct((N_SAMPLE, 4 * D_MODEL), F32),
                   jax.ShapeDtypeStruct((N_SAMPLE, LANES), F32)],
        compiler_params=_params("arbitrary"),
        name="mlstm_proj",
    )(x, g, w_all, wg_all)


def _gate_terms(gate_col, gate_row, bias_col, bias_row, length):
    r = lax.broadcasted_iota(jnp.int32, (length, length), 0)
    c = lax.broadcasted_iota(jnp.int32, (length, length), 1)
    mask = r >= c
    lower = mask.astype(F32)
    upper = (r <= c).astype(F32)
    zc = gate_col + bias_col
    lane = lax.broadcasted_iota(jnp.int32, zc.shape, 1)
    log_col = jnp.where(lane < N_HEADS, zc, _log_sigmoid(zc))
    cum_col = jnp.dot(lower, log_col, precision=lax.Precision.HIGHEST,
                      preferred_element_type=F32)
    zr = gate_row + bias_row
    sub = lax.broadcasted_iota(jnp.int32, zr.shape, 0)
    log_row = jnp.where(sub < N_HEADS, zr, _log_sigmoid(zr))
    cum_row = jnp.dot(log_row, upper, precision=lax.Precision.HIGHEST,
                      preferred_element_type=F32)
    return mask, log_col, cum_col, log_row, cum_row


def _mlstm_chunk(p_views, gates, bias_col, bias_row, g_head_ref, length, get_state, put_state, put_y):
    pairs = [(s, h) for s in range(len(p_views)) for h in range(N_HEADS)]
    terms = [_gate_terms(gc, gr, bias_col, bias_row, length) for gc, gr in gates]
    mask = terms[0][0]

    def cols(h):
        return slice(h * HEAD_DIM, (h + 1) * HEAD_DIM)

    def part(s, h, which):
        lo = which * D_MODEL + h * HEAD_DIM
        return p_views[s][:, lo:lo + HEAD_DIM]

    b_col, b_row, i_col, i_row = {}, {}, {}, {}
    for s, h in pairs:
        _, log_col, cum_col, log_row, cum_row = terms[s]
        f = N_HEADS + h
        b_col[s, h], b_row[s, h] = cum_col[:, f:f + 1], cum_row[f:f + 1, :]
        i_col[s, h], i_row[s, h] = log_col[:, h:h + 1], log_row[h:h + 1, :]

    a, dlog, mt, dw, inter = {}, {}, {}, {}, {}
    for sh in pairs:
        a[sh] = b_col[sh] + get_state(*sh, "m")
        dlog[sh] = jnp.where(mask, b_col[sh] - b_row[sh] + i_row[sh], -jnp.inf)
    for sh in pairs:
        mt[sh] = jnp.maximum(a[sh], jnp.max(dlog[sh], axis=1, keepdims=True))
    for sh in pairs:
        dw[sh] = jnp.exp(dlog[sh] - mt[sh])
        inter[sh] = jnp.exp(a[sh] - mt[sh])

    sc, q_c, q_n, vb, k32 = {}, {}, {}, {}, {}
    for s, h in pairs:
        q = part(s, h, 0)
        k32[s, h] = part(s, h, 1) * (HEAD_DIM ** -0.5)
        qb = q.astype(BF16)
        sc[s, h] = lax.dot_general(qb, k32[s, h].astype(BF16), _NT,
                                   preferred_element_type=F32) * dw[s, h]
        q_c[s, h] = _dot(qb, get_state(s, h, "c").astype(BF16))
        q_n[s, h] = jnp.sum(q * get_state(s, h, "n"), axis=1, keepdims=True)
    hid = {}
    for s, h in pairs:
        vb[s, h] = part(s, h, 2).astype(BF16)
        num = inter[s, h] * q_c[s, h] + _dot(sc[s, h].astype(BF16), vb[s, h])
        den = inter[s, h] * q_n[s, h] + jnp.sum(sc[s, h], axis=1, keepdims=True)
        hid[s, h] = num * (1.0 / jnp.maximum(jnp.abs(den), jnp.exp(-mt[s, h])))
    for s, h in pairs:
        hh = hid[s, h]
        hn = hh * lax.rsqrt(jnp.mean(hh * hh, axis=1, keepdims=True) + EPS) * g_head_ref[:, cols(h)]
        put_y(s, h, jax.nn.sigmoid(part(s, h, 3)) * hn)
    for s, h in pairs:
        c_state, n_state, m_state = (get_state(s, h, which) for which in "cnm")
        m_new = mt[s, h][length - 1:length, :]
        b_last = b_col[s, h][length - 1:length, :]
        decay = jnp.exp(b_last + m_state - m_new)
        kw = k32[s, h] * jnp.exp(b_last - b_col[s, h] + i_col[s, h] - m_new)
        c_new = decay * c_state + lax.dot_general(kw.astype(BF16), vb[s, h], _TN,
                                                  preferred_element_type=F32)
        n_new = decay * n_state + jnp.sum(kw, axis=0, keepdims=True)
        put_state(s, h, c_new, n_new, m_new)


def _own_layer(ref, layer, first):
    if not first:
        return ref
    for other in range(ref.shape[0]):
        if other != layer:
            ref[other] = jnp.zeros(ref.shape[1:], ref.dtype)
    return ref.at[layer]


def _mlstm_prompt_kernel(p_ref, gc_ref, gr_ref, x_ref, bc_ref, br_ref, gh_ref, wo_ref, *rest,
                         layer, first):
    xo_ref, co_ref, no_ref, mo_ref, c_sc, n_sc, m_sc, y_sc = rest[-8:]
    chunk = pl.program_id(1)

    @pl.when(chunk == 0)
    def _():
        c_sc[...] = jnp.zeros_like(c_sc)
        n_sc[...] = jnp.zeros_like(n_sc)
        m_sc[...] = jnp.zeros_like(m_sc)

    def get_state(_, h, which):
        return {"c": lambda: c_sc[h], "n": lambda: n_sc[h:h + 1, :],
                "m": lambda: m_sc[h:h + 1, 0:1]}[which]()

    def put_state(_, h, c_new, n_new, m_new):
        c_sc[h] = c_new
        n_sc[h:h + 1, :] = n_new
        m_sc[h:h + 1, :] = jnp.broadcast_to(m_new, (1, LANES))

    def put_y(_, h, y):
        y_sc[:, h * HEAD_DIM:(h + 1) * HEAD_DIM] = y.astype(BF16)

    _mlstm_chunk([p_ref], [(gc_ref[...], gr_ref[...])], bc_ref[...], br_ref[...], gh_ref, CHUNK,
                 get_state, put_state, put_y)
    xo_ref[...] = x_ref[...] + _dot(y_sc[...], wo_ref[...])

    @pl.when(chunk == pl.num_programs(1) - 1)
    def _():
        _own_layer(co_ref, layer, first)[...] = c_sc[...]
        no_ref[...] = n_sc[0:N_HEADS, :]
        mo_ref[...] = m_sc[...]


def _mlstm_prompt(x, p, gate, gate_t, bias_col, bias_row, g_head, wo_all, j, n_layers, c_prev):
    nc = SEQ // CHUNK
    first = c_prev is None
    state_block = (None, N_HEADS, HEAD_DIM, HEAD_DIM)
    if first:
        c_spec = pl.BlockSpec((n_layers,) + state_block, lambda b, c: (0, b, 0, 0, 0))
    else:
        c_spec = pl.BlockSpec((None,) + state_block, lambda b, c: (j, b, 0, 0, 0))
    tok = lambda n: pl.BlockSpec((CHUNK, n), lambda b, c: (b * nc + c, 0))
    const = lambda shape: pl.BlockSpec(shape, lambda b, c: (0,) * len(shape))
    in_specs = [tok(4 * D_MODEL), tok(LANES),
                pl.BlockSpec((GATE_ROWS, CHUNK), lambda b, c: (0, b * nc + c)),
                tok(D_MODEL), const((1, LANES)), const((GATE_ROWS, 1)), const((1, D_MODEL)),
                _resident((None, D_MODEL, D_MODEL), lambda b, c: (j, 0, 0))]
    args = [p, gate, gate_t, x, bias_col, bias_row, g_head, wo_all]
    aliases = {3: 0}
    if c_prev is not None:
        in_specs.append(pl.BlockSpec(memory_space=pl.ANY))
        args.append(c_prev)
        aliases[len(args) - 1] = 1
    return pl.pallas_call(
        functools.partial(_mlstm_prompt_kernel, layer=j, first=first),
        grid=(BATCH, nc),
        in_specs=in_specs,
        out_specs=[tok(D_MODEL), c_spec,
                   pl.BlockSpec((None, N_HEADS, HEAD_DIM), lambda b, c: (b, 0, 0)),
                   pl.BlockSpec((None, GATE_ROWS, LANES), lambda b, c: (b, 0, 0))],
        out_shape=[jax.ShapeDtypeStruct((N_ROWS, D_MODEL), F32),
                   jax.ShapeDtypeStruct((n_layers, BATCH, N_HEADS, HEAD_DIM, HEAD_DIM), F32),
                   jax.ShapeDtypeStruct((BATCH, N_HEADS, HEAD_DIM), F32),
                   jax.ShapeDtypeStruct((BATCH, GATE_ROWS, LANES), F32)],
        scratch_shapes=[pltpu.VMEM((N_HEADS, HEAD_DIM, HEAD_DIM), F32),
                        pltpu.VMEM((GATE_ROWS, HEAD_DIM), F32),
                        pltpu.VMEM((GATE_ROWS, LANES), F32),
                        pltpu.VMEM((CHUNK, D_MODEL), BF16)],
        input_output_aliases=aliases,
        compiler_params=_params("arbitrary", "arbitrary"),
        name="mlstm_prompt",
    )(*args)


MLSTM_SAMPLE_BATCH = 4


def _mlstm_sample_kernel(p_ref, gc_ref, gr_ref, bc_ref, br_ref, gh_ref, ci_ref, ni_ref, mi_ref, *rest,
                         layer, first):
    y_ref, co_ref, no_ref, mo_ref = rest[-4:]
    co_ref = _own_layer(co_ref, layer, first)

    def get_state(b, h, which):
        return {"c": lambda: ci_ref[b, h], "n": lambda: ni_ref[b, h:h + 1, :],
                "m": lambda: mi_ref[b, h:h + 1, 0:1]}[which]()

    def put_state(b, h, c_new, n_new, m_new):
        co_ref[b, h] = c_new
        no_ref[b, h:h + 1, :] = n_new
        mo_ref[b, h:h + 1, :] = jnp.broadcast_to(m_new, (1, LANES))

    def put_y(b, h, y):
        y_ref[b, :, h * HEAD_DIM:(h + 1) * HEAD_DIM] = y

    mo_ref[...] = jnp.zeros(mo_ref.shape, F32)
    seqs = range(MLSTM_SAMPLE_BATCH)
    _mlstm_chunk([p_ref.at[b] for b in seqs], [(gc_ref[b], gr_ref[b]) for b in seqs],
                 bc_ref[...], br_ref[...], gh_ref, DEC_SEQ, get_state, put_state, put_y)


def _mlstm_sample(p3, gate3, gate_t3, bias_col, bias_row, g_head, c_all, n_in, m_in, j, c_prev):
    bs = MLSTM_SAMPLE_BATCH
    first = c_prev is None
    const = lambda shape: pl.BlockSpec(shape, lambda i: (0,) * len(shape))
    tok = lambda n: pl.BlockSpec((bs, DEC_SEQ, n), lambda i: (i, 0, 0))
    state_block = (bs, N_HEADS, HEAD_DIM, HEAD_DIM)
    c_spec = pl.BlockSpec((None,) + state_block, lambda i: (j, i, 0, 0, 0))
    c_out = pl.BlockSpec((c_all.shape[0],) + state_block, lambda i: (0, i, 0, 0, 0)) if first else c_spec
    n_spec = pl.BlockSpec((bs, N_HEADS, HEAD_DIM), lambda i: (i, 0, 0))
    m_spec = pl.BlockSpec((bs, GATE_ROWS, LANES), lambda i: (i, 0, 0))
    in_specs = [tok(4 * D_MODEL), tok(LANES),
                pl.BlockSpec((bs, GATE_ROWS, DEC_SEQ), lambda i: (i, 0, 0)),
                const((1, LANES)), const((GATE_ROWS, 1)), const((1, D_MODEL)),
                c_spec, n_spec, m_spec]
    args = [p3, gate3, gate_t3, bias_col, bias_row, g_head, c_all, n_in, m_in]
    aliases = {}
    if c_prev is not None:
        in_specs.append(pl.BlockSpec(memory_space=pl.ANY))
        args.append(c_prev)
        aliases[len(args) - 1] = 1
    return pl.pallas_call(
        functools.partial(_mlstm_sample_kernel, layer=j, first=first),
        grid=(DEC_BATCH // bs,),
        in_specs=in_specs,
        out_specs=[tok(D_MODEL), c_out, n_spec, m_spec],
        out_shape=[jax.ShapeDtypeStruct((DEC_BATCH, DEC_SEQ, D_MODEL), F32),
                   jax.ShapeDtypeStruct(c_all.shape, F32),
                   jax.ShapeDtypeStruct((DEC_BATCH, N_HEADS, HEAD_DIM), F32),
                   jax.ShapeDtypeStruct((DEC_BATCH, GATE_ROWS, LANES), F32)],
        input_output_aliases=aliases,
        compiler_params=_params("arbitrary"),
        name="mlstm_sample",
    )(*args)


def _norm_matmul_kernel(x_ref, g_ref, w_ref, o_ref):
    o_ref[...] = _dot(_rms(x_ref[...], g_ref[...]).astype(BF16), w_ref[...])


def _norm_matmul_sample(x, g, w_all, l):
    first = N_PROMPT // ROW_TILE
    return pl.pallas_call(
        _norm_matmul_kernel,
        grid=(N_SAMPLE // ROW_TILE,),
        in_specs=[pl.BlockSpec((ROW_TILE, D_MODEL), lambda r: (first + r, 0)),
                  pl.BlockSpec((1, D_MODEL), lambda r: (0, 0)),
                  _resident((None, D_MODEL, D_MODEL), lambda r: (l, 0, 0))],
        out_specs=pl.BlockSpec((ROW_TILE, D_MODEL), lambda r: (r, 0)),
        out_shape=jax.ShapeDtypeStruct((N_SAMPLE, D_MODEL), F32),
        compiler_params=_params("parallel"),
        name="norm_matmul_sample",
    )(x, g, w_all)


def _matmul_residual_kernel(x_ref, y_ref, w_ref, o_ref):
    o_ref[...] = x_ref[...] + _dot(y_ref[...].astype(BF16), w_ref[...])


def _matmul_residual_sample(x, y, w_all, l):
    first = N_PROMPT // ROW_TILE
    xrow = pl.BlockSpec((ROW_TILE, D_MODEL), lambda r: (first + r, 0))
    return pl.pallas_call(
        _matmul_residual_kernel,
        grid=(N_SAMPLE // ROW_TILE,),
        in_specs=[xrow, pl.BlockSpec((ROW_TILE, D_MODEL), lambda r: (r, 0)),
                  _resident((None, D_MODEL, D_MODEL), lambda r: (l, 0, 0))],
        out_specs=xrow,
        out_shape=jax.ShapeDtypeStruct((N_ROWS, D_MODEL), F32),
        input_output_aliases={0: 0},
        compiler_params=_params("parallel"),
        name="matmul_residual_sample",
    )(x, y, w_all)


MEM_KV_BATCH = 2


def _mem_kv_kernel(m_ref, w_ref, k_ref, v_ref, kb_ref, vb_ref):
    kv = _dot(m_ref[...].astype(BF16), w_ref[...])
    kb_ref[...] = kv[:, :D_MODEL].astype(BF16)
    vb_ref[...] = kv[:, D_MODEL:].astype(BF16)
    for b in range(MEM_KV_BATCH):
        rows = slice(b * N_MEM, (b + 1) * N_MEM)
        for h in range(N_XHEADS):
            lo, hi = h * XHEAD_DIM, (h + 1) * XHEAD_DIM
            k_ref[b, :, h, :] = kv[rows, lo:hi]
            v_ref[b, :, h, :] = kv[rows, D_MODEL + lo:D_MODEL + hi]


def _mem_kv(mem, wkv_all):
    bs = MEM_KV_BATCH
    out = pl.BlockSpec((None, bs, N_MEM, N_XHEADS, XHEAD_DIM), lambda l, r: (l, r, 0, 0, 0))
    shape = jax.ShapeDtypeStruct((DEPTH, BATCH, N_MEM, N_XHEADS, XHEAD_DIM), F32)
    out_b = pl.BlockSpec((None, bs * N_MEM, D_MODEL), lambda l, r: (l, r, 0))
    shape_b = jax.ShapeDtypeStruct((DEPTH, BATCH * N_MEM, D_MODEL), BF16)
    return pl.pallas_call(
        _mem_kv_kernel,
        grid=(DEPTH, BATCH // bs),
        in_specs=[pl.BlockSpec((bs * N_MEM, D_MODEL), lambda l, r: (r, 0)),
                  pl.BlockSpec((None, D_MODEL, 2 * D_MODEL), lambda l, r: (l, 0, 0))],
        out_specs=[out, out, out_b, out_b],
        out_shape=[shape, shape, shape_b, shape_b],
        compiler_params=_params("arbitrary", "arbitrary"),
        name="mem_kv",
    )(mem, wkv_all)


def _softmax_rows(s):
    e = jnp.exp(s - jnp.max(s, axis=1, keepdims=True))
    return e * (1.0 / jnp.sum(e, axis=1, keepdims=True))


def _xattn_prompt_kernel(x_ref, g_ref, wq_ref, k_ref, v_ref, wo_ref, o_ref, a_sc):
    x = x_ref[...]
    q = _dot(_rms(x, g_ref[...]).astype(BF16), wq_ref[...])
    cols = [slice(h * XHEAD_DIM, (h + 1) * XHEAD_DIM) for h in range(N_XHEADS)]
    scores = [lax.dot_general(q[:, c].astype(BF16), k_ref[:, c], _NT,
                              preferred_element_type=F32) * (XHEAD_DIM ** -0.5) for c in cols]
    probs = [_softmax_rows(s).astype(BF16) for s in scores]
    for c, p in zip(cols, probs):
        a_sc[:, c] = _dot(p, v_ref[:, c]).astype(BF16)
    o_ref[...] = x + _dot(a_sc[...], wo_ref[...])


def _xattn_prompt(x, g, wq_all, k_all, v_all, wo_all, l):
    nt = SEQ // SEQ_TILE
    xrow = pl.BlockSpec((SEQ_TILE, D_MODEL), lambda b, t: (b * nt + t, 0))
    kv = pl.BlockSpec((None, N_MEM, D_MODEL), lambda b, t: (l, b, 0))
    w = _resident((None, D_MODEL, D_MODEL), lambda b, t: (l, 0, 0))
    return pl.pallas_call(
        _xattn_prompt_kernel,
        grid=(BATCH, nt),
        in_specs=[xrow, pl.BlockSpec((1, D_MODEL), lambda b, t: (0, 0)), w, kv, kv, w],
        out_specs=xrow,
        out_shape=jax.ShapeDtypeStruct((N_ROWS, D_MODEL), F32),
        scratch_shapes=[pltpu.VMEM((SEQ_TILE, D_MODEL), BF16)],
        input_output_aliases={0: 0},
        compiler_params=_params("parallel", "parallel"),
        name="xattn_prompt",
    )(x, g, wq_all, k_all, v_all, wo_all)


def _sample_attention(qs, ks, vs):
    n_q, n_kv = N_XHEADS * DEC_SEQ, N_MEM * N_XHEADS
    q_head = lax.broadcasted_iota(jnp.int32, (n_q, n_kv), 0) // DEC_SEQ
    kv_head = lax.broadcasted_iota(jnp.int32, (n_q, n_kv), 1) % N_XHEADS
    same_head = q_head == kv_head
    scores = [lax.dot_general(q.astype(BF16), k.reshape(n_kv, XHEAD_DIM).astype(BF16), _NT,
                              preferred_element_type=F32) * (XHEAD_DIM ** -0.5)
              for q, k in zip(qs, ks)]
    probs = [_softmax_rows(jnp.where(same_head, s, -jnp.inf)).astype(BF16) for s in scores]
    return [_dot(p, v.reshape(n_kv, XHEAD_DIM).astype(BF16)) for p, v in zip(probs, vs)]


def _heads_major(a):
    a = a.reshape(DEC_BATCH, DEC_SEQ, N_XHEADS, XHEAD_DIM)
    return a.transpose(0, 2, 1, 3).reshape(DEC_BATCH, N_XHEADS * DEC_SEQ, XHEAD_DIM)


def _pool_prompt_kernel(x_ref, g_ref, win_ref, wgrp_ref, scale_ref, wout_ref,
                        o_ref, buf_ref, ext_sc, z_sc):
    tile = pl.program_id(1)

    @pl.when(tile == 0)
    def _():
        ext_sc[0:POOL_HALO, :] = jnp.zeros((POOL_HALO, D_MODEL), F32)

    x = x_ref[...]
    u = _dot(_rms(x, g_ref[...]).astype(BF16), win_ref[...])
    ext_sc[POOL_HALO:, :] = u
    pos = tile * SEQ_TILE + lax.broadcasted_iota(jnp.int32, (SEQ_TILE, 1), 0)
    for g, w in enumerate(POOL_WINDOWS):
        lo, hi = g * POOL_GROUP_DIM, (g + 1) * POOL_GROUP_DIM
        assert w & (w - 1) == 0 and w <= POOL_HALO
        win = ext_sc[:, lo:hi]
        span = 1
        while span < w:
            win = win + pltpu.roll(win, span, 0)
            span *= 2
        win = win[POOL_HALO:, :]
        cnt = jnp.minimum(pos + 1, w).astype(F32)
        d = win * (1.0 / cnt) - u[:, lo:hi]
        z_sc[:, lo:hi] = (_dot(d.astype(BF16), wgrp_ref[g]) * scale_ref[:, lo:hi]).astype(BF16)
    o_ref[...] = x + _dot(z_sc[...], wout_ref[...])
    tail = ext_sc[SEQ_TILE:SEQ_TILE + POOL_HALO, :]
    ext_sc[0:POOL_HALO, :] = tail

    @pl.when(tile == pl.num_programs(1) - 1)
    def _():
        buf_ref[...] = tail


def _pool_prompt(x, g, win_all, wgrp_all, scale, wout_all, j):
    nt = SEQ // SEQ_TILE
    xrow = pl.BlockSpec((SEQ_TILE, D_MODEL), lambda b, t: (b * nt + t, 0))
    vec = pl.BlockSpec((1, D_MODEL), lambda b, t: (0, 0))
    w = _resident((None, D_MODEL, D_MODEL), lambda b, t: (j, 0, 0))
    wgrp = _resident((None, len(POOL_WINDOWS), POOL_GROUP_DIM, POOL_GROUP_DIM),
                     lambda b, t: (j, 0, 0, 0))
    return pl.pallas_call(
        _pool_prompt_kernel,
        grid=(BATCH, nt),
        in_specs=[xrow, vec, w, wgrp, vec, w],
        out_specs=[xrow, pl.BlockSpec((None, POOL_HALO, D_MODEL), lambda b, t: (b, 0, 0))],
        out_shape=[jax.ShapeDtypeStruct((N_ROWS, D_MODEL), F32),
                   jax.ShapeDtypeStruct((BATCH, POOL_HALO, D_MODEL), F32)],
        scratch_shapes=[pltpu.VMEM((POOL_HALO + SEQ_TILE, D_MODEL), F32),
                        pltpu.VMEM((SEQ_TILE, D_MODEL), BF16)],
        input_output_aliases={0: 0},
        compiler_params=_params("arbitrary", "arbitrary"),
        name="pool_prompt",
    )(x, g, win_all, wgrp_all, scale, wout_all)


def _pool_sample_kernel(x_ref, buf_ref, g_ref, win_ref, wgrp_ref, scale_ref, *rest, layer, first):
    z_ref, nbuf_ref = rest[-2:]
    nbuf_ref = _own_layer(nbuf_ref, layer, first)
    u = _dot(_rms(x_ref[...], g_ref[...]).astype(BF16), win_ref[...])

    def ext(row, lo, hi):
        if row < POOL_BUF:
            return buf_ref[:, row, lo:hi]
        t = row - POOL_BUF
        return u[t * DEC_BATCH:(t + 1) * DEC_BATCH, lo:hi]

    for g, w in enumerate(POOL_WINDOWS):
        lo, hi = g * POOL_GROUP_DIM, (g + 1) * POOL_GROUP_DIM
        ds = []
        for t in range(DEC_SEQ):
            win = ext(POOL_BUF + t, lo, hi)
            for back in range(1, w):
                win = win + ext(POOL_BUF + t - back, lo, hi)
            cnt = float(min(PAST_LEN + 1 + t, w))
            ds.append(win / cnt - ext(POOL_BUF + t, lo, hi))
        d = jnp.concatenate(ds, axis=0).astype(BF16)
        z_ref[:, lo:hi] = _dot(d, wgrp_ref[g]) * scale_ref[:, lo:hi]
    for row in range(POOL_BUF):
        nbuf_ref[:, row, :] = ext(row + DEC_SEQ, 0, D_MODEL)


def _pool_sample(xs_t, buf_all, g, win_all, wgrp_all, scale, j, buf_prev):
    first = buf_prev is None
    buf_block = (DEC_BATCH, POOL_BUF, D_MODEL)
    once = lambda shape, imap: pl.BlockSpec(shape, imap, pipeline_mode=pl.Buffered(1))
    if first:
        nbuf = once((buf_all.shape[0],) + buf_block, lambda i: (0, 0, 0, 0))
    else:
        nbuf = once((None,) + buf_block, lambda i: (j, 0, 0, 0))
    vec = pl.BlockSpec((1, D_MODEL), lambda i: (0, 0))
    buf = once((None, DEC_BATCH, POOL_BUF, D_MODEL), lambda i: (j, 0, 0, 0))
    in_specs = [pl.BlockSpec((N_SAMPLE, D_MODEL), lambda i: (0, 0)), buf, vec,
                once((None, D_MODEL, D_MODEL), lambda i: (j, 0, 0)),
                once((None, len(POOL_WINDOWS), POOL_GROUP_DIM, POOL_GROUP_DIM), lambda i: (j, 0, 0, 0)),
                vec]
    args = [xs_t, buf_all, g, win_all, wgrp_all, scale]
    aliases = {}
    if buf_prev is not None:
        in_specs.append(pl.BlockSpec(memory_space=pl.ANY))
        args.append(buf_prev)
        aliases[len(args) - 1] = 1
    return pl.pallas_call(
        functools.partial(_pool_sample_kernel, layer=j, first=first),
        grid=(1,),
        in_specs=in_specs,
        out_specs=[pl.BlockSpec((N_SAMPLE, D_MODEL), lambda i: (0, 0)), nbuf],
        out_shape=[jax.ShapeDtypeStruct((N_SAMPLE, D_MODEL), F32),
                   jax.ShapeDtypeStruct(buf_all.shape, F32)],
        input_output_aliases=aliases,
        compiler_params=_params("arbitrary"),
        name="pool_sample",
    )(*args)


def _swap_token_seq(a, lead):
    n = a.shape[0] // lead
    return a.reshape(lead, n, a.shape[1]).transpose(1, 0, 2).reshape(a.shape)


def kernel(x_prompt, x_sample, mem_prompt, cache_mem_k, cache_mem_v, state_mlstm_C, state_mlstm_n,
           state_mlstm_m, state_pool_buf, norm_g, final_g, ffn_w_up, ffn_w_down, mlstm_w_in,
           mlstm_b_i, mlstm_b_f, mlstm_g_head, mlstm_w_out, pool_w_in, pool_w_grp, pool_scale,
           pool_w_out, xattn_w_q, xattn_w_kv, xattn_w_o):
    n_mlstm = mlstm_w_in.shape[0]
    wup = ffn_w_up.astype(BF16)
    wdn = ffn_w_down.astype(BF16)
    w_in = mlstm_w_in.astype(BF16)
    w_gate = jnp.pad(mlstm_w_in[:, :, 4 * D_MODEL:],
                     ((0, 0), (0, 0), (0, LANES - 2 * N_HEADS))).astype(BF16)
    w_mout = mlstm_w_out.astype(BF16)
    gate_bias = jnp.concatenate([mlstm_b_i, mlstm_b_f], axis=1).astype(F32)
    p_win = pool_w_in.astype(BF16)
    p_wgrp = pool_w_grp.astype(BF16)
    p_wout = pool_w_out.astype(BF16)
    wq = xattn_w_q.astype(BF16)
    wkv = xattn_w_kv.astype(BF16)
    wo = xattn_w_o.astype(BF16)

    mem_k, mem_v, mem_kb, mem_vb = _mem_kv(mem_prompt.reshape(BATCH * N_MEM, D_MODEL), wkv)

    x = [x_prompt.reshape(N_PROMPT, D_MODEL), x_sample.reshape(N_SAMPLE, D_MODEL)]
    c_p = c_s = buf_s = None
    n_p, m_p, n_s, m_s, buf_p = [], [], [], [], []
    for l in range(DEPTH):
        g = norm_g[l].reshape(4, 1, D_MODEL)
        j = l // 2
        x = _ffn(x if l == 0 else [x], g[0], wup, wdn, l, 0)
        if l % 2 == 0:
            p_p, gate_p, p_s, gate_s = _mlstm_proj(x, g[1], w_in, w_gate, j)
            bias_col = jnp.pad(gate_bias[j], (0, LANES - GATE_ROWS)).reshape(1, LANES)
            bias_row = gate_bias[j].reshape(GATE_ROWS, 1)
            g_head = mlstm_g_head[j].reshape(1, D_MODEL)
            x, c_p, n1, m1 = _mlstm_prompt(x, p_p, gate_p, gate_p[:, :GATE_ROWS].T, bias_col, bias_row,
                                           g_head, w_mout, j, n_mlstm, c_p)
            gate_s = gate_s.reshape(DEC_BATCH, DEC_SEQ, LANES)
            m_in = jnp.broadcast_to(
                jnp.pad(state_mlstm_m[j], ((0, 0), (0, GATE_ROWS - N_HEADS)))[:, :, None],
                (DEC_BATCH, GATE_ROWS, LANES))
            y_s, c_s, n2, m2 = _mlstm_sample(
                p_s.reshape(DEC_BATCH, DEC_SEQ, 4 * D_MODEL), gate_s,
                gate_s[:, :, :GATE_ROWS].transpose(0, 2, 1),
                bias_col, bias_row, g_head, state_mlstm_C, state_mlstm_n[j], m_in, j, c_s)
            x = _matmul_residual_sample(x, y_s.reshape(N_SAMPLE, D_MODEL), w_mout, j)
            n_p.append(n1); m_p.append(m1[:, :N_HEADS, 0])
            n_s.append(n2); m_s.append(m2[:, :N_HEADS, 0])
        else:
            scale = pool_scale[j].reshape(1, D_MODEL)
            x, b1 = _pool_prompt(x, g[1], p_win, p_wgrp, scale, p_wout, j)
            xs_t = _swap_token_seq(x[N_PROMPT:], DEC_BATCH)
            z_t, buf_s = _pool_sample(xs_t, state_pool_buf, g[1], p_win, p_wgrp, scale, j, buf_s)
            x = _matmul_residual_sample(x, _swap_token_seq(z_t, DEC_SEQ), p_wout, j)
            buf_p.append(b1[:, POOL_HALO - POOL_BUF:])
        x = _xattn_prompt(x, g[2], wq, mem_kb, mem_vb, wo, l)
        q_s = _norm_matmul_sample(x, g[2], wq, l)
        x = _ffn([x], g[3], wup, wdn, l, 1,
                 final_g=final_g.reshape(1, D_MODEL) if l == DEPTH - 1 else None,
                 attend=(_heads_major(q_s), cache_mem_k, cache_mem_v, wo))

    y_prompt, y_sample = x
    return (y_prompt.reshape(BATCH, SEQ, D_MODEL), y_sample.reshape(DEC_BATCH, DEC_SEQ, D_MODEL),
            mem_k, mem_v, c_p, jnp.stack(n_p), jnp.stack(m_p),
            c_s, jnp.stack(n_s), jnp.stack(m_s), jnp.stack(buf_p), buf_s)
```

```python
import functools

import jax
import jax.numpy as jnp
from jax import lax
from jax.experimental import pallas as pl
from jax.experimental.pallas import tpu as pltpu

F32 = jnp.float32
BF16 = jnp.bfloat16

D_MODEL = 1024
BATCH = 8
SEQ = 2048
DEPTH = 4
DEC_BATCH = 128
DEC_SEQ = 4
PAST_LEN = 16384
N_HEADS = 4
HEAD_DIM = D_MODEL // N_HEADS
CHUNK = 256
POOL_WINDOWS = (2, 4, 8, 16)
POOL_GROUP_DIM = D_MODEL // len(POOL_WINDOWS)
POOL_BUF = max(POOL_WINDOWS) - 1
POOL_HALO = 16
N_MEM = 256
N_XHEADS = 4
XHEAD_DIM = D_MODEL // N_XHEADS
D_FF = ((8 * D_MODEL // 3 + 127) // 128) * 128
EPS = 1e-6

N_PROMPT = BATCH * SEQ
N_SAMPLE = DEC_BATCH * DEC_SEQ
N_ROWS = N_PROMPT + N_SAMPLE

LANES = 128
GATE_ROWS = 8
ROW_TILE = 512
SEQ_TILE = 1024
PROMPT_TILES = N_PROMPT // ROW_TILE
assert N_SAMPLE == ROW_TILE
VMEM_LIMIT = 56 * 1024 * 1024

_NT = (((1,), (1,)), ((), ()))
_TN = (((0,), (0,)), ((), ()))


def _params(*sem):
    return pltpu.CompilerParams(dimension_semantics=sem, vmem_limit_bytes=VMEM_LIMIT)


def _resident(block_shape, index_map):
    return pl.BlockSpec(block_shape, index_map, pipeline_mode=pl.Buffered(1))


def _rms(x, g):
    return x * lax.rsqrt(jnp.mean(x * x, axis=-1, keepdims=True) + EPS) * g


def _log_sigmoid(x):
    return jnp.minimum(x, 0.0) - jnp.log1p(jnp.exp(-jnp.abs(x)))


def _dot(a, b):
    return jnp.dot(a, b, preferred_element_type=F32)


ATTEND_BATCH = DEC_BATCH // PROMPT_TILES
assert ATTEND_BATCH * PROMPT_TILES == DEC_BATCH


def _ffn_kernel(*refs, first, final, attend):
    refs = list(refs)
    step = pl.program_id(0)
    is_prompt = step < PROMPT_TILES
    if first:
        xp_ref, xs_ref = refs[:2]
        del refs[:2]
        x = jnp.where(is_prompt, xp_ref[...], xs_ref[...])
    else:
        x = refs.pop(0)[...]
    g_ref, wup_ref, wdn_ref = refs[:3]
    del refs[:3]
    if attend:
        q_ref, k_ref, v_ref, wo_ref = refs[:4]
        del refs[:4]
        a_sc = refs.pop()

        def attended():
            y = x
            for h in range(N_XHEADS):
                a = a_sc[h].reshape(N_SAMPLE, XHEAD_DIM).astype(BF16)
                y = y + _dot(a, wo_ref[h * XHEAD_DIM:(h + 1) * XHEAD_DIM, :])
            return y

        x = lax.cond(is_prompt, lambda: x, attended)
        seqs = range(ATTEND_BATCH)
        first_seq = jnp.minimum(step, PROMPT_TILES - 1) * ATTEND_BATCH
        outs = _sample_attention([q_ref[b] for b in seqs], [k_ref[b] for b in seqs],
                                 [v_ref[b] for b in seqs])
        for b, o in enumerate(outs):
            for h in range(N_XHEADS):
                a_sc[h, first_seq + b] = o[h * DEC_SEQ:(h + 1) * DEC_SEQ, :]
    rest = refs
    xn = _rms(x, g_ref[...]).astype(BF16)
    gate = _dot(xn, wup_ref[:, :D_FF])
    up = _dot(xn, wup_ref[:, D_FF:])
    act = (gate * jax.nn.sigmoid(gate) * up).astype(BF16)
    y = x + 0.5 * _dot(act, wdn_ref[...])
    if not final:
        (o_ref,) = rest
        o_ref[...] = y
        return
    fg_ref, op_ref, os_ref = rest
    y = _rms(y, fg_ref[...])

    @pl.when(is_prompt)
    def _():
        op_ref[...] = y

    @pl.when(jnp.logical_not(is_prompt))
    def _():
        os_ref[...] = y


def _ffn(xs, g, wup_all, wdn_all, l, i, final_g=None, attend=None):
    first, final = len(xs) == 2, final_g is not None
    row = pl.BlockSpec((ROW_TILE, D_MODEL), lambda r: (r, 0))
    prompt_row = pl.BlockSpec((ROW_TILE, D_MODEL), lambda r: (jnp.minimum(r, PROMPT_TILES - 1), 0))
    sample_row = pl.BlockSpec((ROW_TILE, D_MODEL), lambda r: (0, 0))
    vec = pl.BlockSpec((1, D_MODEL), lambda r: (0, 0))
    in_specs = ([prompt_row, sample_row] if first else [row]) + [
        vec,
        _resident((None, None, D_MODEL, 2 * D_FF), lambda r: (l, i, 0, 0)),
        _resident((None, None, D_FF, D_MODEL), lambda r: (l, i, 0, 0))]
    args = list(xs) + [g, wup_all, wdn_all]
    scratch = []
    if attend is not None:
        bs = ATTEND_BATCH
        tile = lambda r: jnp.minimum(r, PROMPT_TILES - 1)
        cache = pl.BlockSpec((None, bs, N_MEM, N_XHEADS, XHEAD_DIM), lambda r: (l, tile(r), 0, 0, 0))
        in_specs += [pl.BlockSpec((bs, N_XHEADS * DEC_SEQ, XHEAD_DIM), lambda r: (tile(r), 0, 0)),
                     cache, cache, _resident((None, D_MODEL, D_MODEL), lambda r: (l, 0, 0))]
        args += list(attend)
        scratch = [pltpu.VMEM((N_XHEADS, DEC_BATCH, DEC_SEQ, XHEAD_DIM), F32)]
    if final:
        in_specs.append(vec)
        args.append(final_g)
        out_specs = [prompt_row, sample_row]
        out_shape = [jax.ShapeDtypeStruct((N_PROMPT, D_MODEL), F32),
                     jax.ShapeDtypeStruct((N_SAMPLE, D_MODEL), F32)]
    else:
        out_specs = row
        out_shape = jax.ShapeDtypeStruct((N_ROWS, D_MODEL), F32)
    return pl.pallas_call(
        functools.partial(_ffn_kernel, first=first, final=final, attend=attend is not None),
        grid=(N_ROWS // ROW_TILE,),
        in_specs=in_specs,
        out_specs=out_specs,
        out_shape=out_shape,
        scratch_shapes=scratch,
        compiler_params=_params("arbitrary"),
        name=("ffn_final" if final else "ffn_first" if first else "ffn") + ("_attend" if scratch else ""),
    )(*args)


def _prompt_rows(n):
    return pl.BlockSpec((ROW_TILE, n), lambda r: (jnp.minimum(r, PROMPT_TILES - 1), 0))


def _sample_rows(n):
    return pl.BlockSpec((ROW_TILE, n), lambda r: (0, 0))


def _mlstm_proj_kernel(x_ref, g_ref, w_ref, wg_ref, pp_ref, gp_ref, ps_ref, gs_ref):
    xn = _rms(x_ref[...], g_ref[...]).astype(BF16)
    is_prompt = pl.program_id(0) < PROMPT_TILES

    @pl.when(is_prompt)
    def _():
        pp_ref[...] = _dot(xn, w_ref[...])
        gp_ref[...] = _dot(xn, wg_ref[...])

    @pl.when(jnp.logical_not(is_prompt))
    def _():
        ps_ref[...] = _dot(xn, w_ref[...])
        gs_ref[...] = _dot(xn, wg_ref[...])


def _mlstm_proj(x, g, w_all, wg_all, j):
    return pl.pallas_call(
        _mlstm_proj_kernel,
        grid=(N_ROWS // ROW_TILE,),
        in_specs=[pl.BlockSpec((ROW_TILE, D_MODEL), lambda r: (r, 0)),
                  pl.BlockSpec((1, D_MODEL), lambda r: (0, 0)),
                  _resident((None, D_MODEL, 4 * D_MODEL), lambda r: (j, 0, 0)),
                  _resident((None, D_MODEL, LANES), lambda r: (j, 0, 0))],
        out_specs=[_prompt_rows(4 * D_MODEL), _prompt_rows(LANES),
                   _sample_rows(4 * D_MODEL), _sample_rows(LANES)],
        out_shape=[jax.ShapeDtypeStruct((N_PROMPT, 4 * D_MODEL), F32),
                   jax.ShapeDtypeStruct((N_PROMPT, LANES), F32),
                   jax.ShapeDtypeStruct((N_SAMPLE, 4 * D_MODEL), F32),
                   jax.ShapeDtypeStruct((N_SAMPLE, LANES), F32)],
        compiler_params=_params("arbitrary"),
        name="mlstm_proj",
    )(x, g, w_all, wg_all)


def _gate_terms(gate_col, gate_row, bias_col, bias_row, length):
    r = lax.broadcasted_iota(jnp.int32, (length, length), 0)
    c = lax.broadcasted_iota(jnp.int32, (length, length), 1)
    mask = r >= c
    lower = mask.astype(F32)
    upper = (r <= c).astype(F32)
    zc = gate_col + bias_col
    lane = lax.broadcasted_iota(jnp.int32, zc.shape, 1)
    log_col = jnp.where(lane < N_HEADS, zc, _log_sigmoid(zc))
    cum_col = jnp.dot(lower, log_col, precision=lax.Precision.HIGHEST,
                      preferred_element_type=F32)
    zr = gate_row + bias_row
    sub = lax.broadcasted_iota(jnp.int32, zr.shape, 0)
    log_row = jnp.where(sub < N_HEADS, zr, _log_sigmoid(zr))
    cum_row = jnp.dot(log_row, upper, precision=lax.Precision.HIGHEST,
                      preferred_element_type=F32)
    return mask, log_col, cum_col, log_row, cum_row


def _mlstm_chunk(p_views, gates, bias_col, bias_row, g_head_ref, length, get_state, put_state, put_y):
    pairs = [(s, h) for s in range(len(p_views)) for h in range(N_HEADS)]
    terms = [_gate_terms(gc, gr, bias_col, bias_row, length) for gc, gr in gates]
    mask = terms[0][0]

    def cols(h):
        return slice(h * HEAD_DIM, (h + 1) * HEAD_DIM)

    def part(s, h, which):
        lo = which * D_MODEL + h * HEAD_DIM
        return p_views[s][:, lo:lo + HEAD_DIM]

    b_col, b_row, i_col, i_row = {}, {}, {}, {}
    for s, h in pairs:
        _, log_col, cum_col, log_row, cum_row = terms[s]
        f = N_HEADS + h
        b_col[s, h], b_row[s, h] = cum_col[:, f:f + 1], cum_row[f:f + 1, :]
        i_col[s, h], i_row[s, h] = log_col[:, h:h + 1], log_row[h:h + 1, :]

    a, dlog, mt, dw, inter = {}, {}, {}, {}, {}
    for sh in pairs:
        a[sh] = b_col[sh] + get_state(*sh, "m")
        dlog[sh] = jnp.where(mask, b_col[sh] - b_row[sh] + i_row[sh], -jnp.inf)
    for sh in pairs:
        mt[sh] = jnp.maximum(a[sh], jnp.max(dlog[sh], axis=1, keepdims=True))
    for sh in pairs:
        dw[sh] = jnp.exp(dlog[sh] - mt[sh])
        inter[sh] = jnp.exp(a[sh] - mt[sh])

    sc, q_c, q_n, vb, k32 = {}, {}, {}, {}, {}
    for s, h in pairs:
        q = part(s, h, 0)
        k32[s, h] = part(s, h, 1) * (HEAD_DIM ** -0.5)
        qb = q.astype(BF16)
        sc[s, h] = lax.dot_general(qb, k32[s, h].astype(BF16), _NT,
                                   preferred_element_type=F32) * dw[s, h]
        q_c[s, h] = _dot(qb, get_state(s, h, "c").astype(BF16))
        q_n[s, h] = jnp.sum(q * get_state(s, h, "n"), axis=1, keepdims=True)
    hid = {}
    for s, h in pairs:
        vb[s, h] = part(s, h, 2).astype(BF16)
        num = inter[s, h] * q_c[s, h] + _dot(sc[s, h].astype(BF16), vb[s, h])
        den = inter[s, h] * q_n[s, h] + jnp.sum(sc[s, h], axis=1, keepdims=True)
        hid[s, h] = num * (1.0 / jnp.maximum(jnp.abs(den), jnp.exp(-mt[s, h])))
    for s, h in pairs:
        hh = hid[s, h]
        hn = hh * lax.rsqrt(jnp.mean(hh * hh, axis=1, keepdims=True) + EPS) * g_head_ref[:, cols(h)]
        put_y(s, h, jax.nn.sigmoid(part(s, h, 3)) * hn)
    for s, h in pairs:
        c_state, n_state, m_state = (get_state(s, h, which) for which in "cnm")
        m_new = mt[s, h][length - 1:length, :]
        b_last = b_col[s, h][length - 1:length, :]
        decay = jnp.exp(b_last + m_state - m_new)
        kw = k32[s, h] * jnp.exp(b_last - b_col[s, h] + i_col[s, h] - m_new)
        c_new = decay * c_state + lax.dot_general(kw.astype(BF16), vb[s, h], _TN,
                                                  preferred_element_type=F32)
        n_new = decay * n_state + jnp.sum(kw, axis=0, keepdims=True)
        put_state(s, h, c_new, n_new, m_new)


def _own_layer(ref, layer, first):
    if not first:
        return ref
    for other in range(ref.shape[0]):
        if other != layer:
            ref[other] = jnp.zeros(ref.shape[1:], ref.dtype)
    return ref.at[layer]


def _mlstm_prompt_kernel(p_ref, gc_ref, gr_ref, x_ref, bc_ref, br_ref, gh_ref, wo_ref, *rest,
                         layer, first):
    xo_ref, co_ref, no_ref, mo_ref, c_sc, n_sc, m_sc, y_sc = rest[-8:]
    chunk = pl.program_id(1)

    @pl.when(chunk == 0)
    def _():
        c_sc[...] = jnp.zeros_like(c_sc)
        n_sc[...] = jnp.zeros_like(n_sc)
        m_sc[...] = jnp.zeros_like(m_sc)

    def get_state(_, h, which):
        return {"c": lambda: c_sc[h], "n": lambda: n_sc[h:h + 1, :],
                "m": lambda: m_sc[h:h + 1, 0:1]}[which]()

    def put_state(_, h, c_new, n_new, m_new):
        c_sc[h] = c_new
        n_sc[h:h + 1, :] = n_new
        m_sc[h:h + 1, :] = jnp.broadcast_to(m_new, (1, LANES))

    def put_y(_, h, y):
        y_sc[:, h * HEAD_DIM:(h + 1) * HEAD_DIM] = y.astype(BF16)

    _mlstm_chunk([p_ref], [(gc_ref[...], gr_ref[...])], bc_ref[...], br_ref[...], gh_ref, CHUNK,
                 get_state, put_state, put_y)
    xo_ref[...] = x_ref[...] + _dot(y_sc[...], wo_ref[...])

    @pl.when(chunk == pl.num_programs(1) - 1)
    def _():
        _own_layer(co_ref, layer, first)[...] = c_sc[...]
        no_ref[...] = n_sc[0:N_HEADS, :]
        mo_ref[...] = m_sc[...]


def _mlstm_prompt(x, p, gate, gate_t, bias_col, bias_row, g_head, wo_all, j, n_layers, c_prev):
    nc = SEQ // CHUNK
    first = c_prev is None
    state_block = (None, N_HEADS, HEAD_DIM, HEAD_DIM)
    if first:
        c_spec = pl.BlockSpec((n_layers,) + state_block, lambda b, c: (0, b, 0, 0, 0))
    else:
        c_spec = pl.BlockSpec((None,) + state_block, lambda b, c: (j, b, 0, 0, 0))
    tok = lambda n: pl.BlockSpec((CHUNK, n), lambda b, c: (b * nc + c, 0))
    const = lambda shape: pl.BlockSpec(shape, lambda b, c: (0,) * len(shape))
    in_specs = [tok(4 * D_MODEL), tok(LANES),
                pl.BlockSpec((GATE_ROWS, CHUNK), lambda b, c: (0, b * nc + c)),
                tok(D_MODEL), const((1, LANES)), const((GATE_ROWS, 1)), const((1, D_MODEL)),
                _resident((None, D_MODEL, D_MODEL), lambda b, c: (j, 0, 0))]
    args = [p, gate, gate_t, x, bias_col, bias_row, g_head, wo_all]
    aliases = {3: 0}
    if c_prev is not None:
        in_specs.append(pl.BlockSpec(memory_space=pl.ANY))
        args.append(c_prev)
        aliases[len(args) - 1] = 1
    return pl.pallas_call(
        functools.partial(_mlstm_prompt_kernel, layer=j, first=first),
        grid=(BATCH, nc),
        in_specs=in_specs,
        out_specs=[tok(D_MODEL), c_spec,
                   pl.BlockSpec((None, N_HEADS, HEAD_DIM), lambda b, c: (b, 0, 0)),
                   pl.BlockSpec((None, GATE_ROWS, LANES), lambda b, c: (b, 0, 0))],
        out_shape=[jax.ShapeDtypeStruct((N_ROWS, D_MODEL), F32),
                   jax.ShapeDtypeStruct((n_layers, BATCH, N_HEADS, HEAD_DIM, HEAD_DIM), F32),
                   jax.ShapeDtypeStruct((BATCH, N_HEADS, HEAD_DIM), F32),
                   jax.ShapeDtypeStruct((BATCH, GATE_ROWS, LANES), F32)],
        scratch_shapes=[pltpu.VMEM((N_HEADS, HEAD_DIM, HEAD_DIM), F32),
                        pltpu.VMEM((GATE_ROWS, HEAD_DIM), F32),
                        pltpu.VMEM((GATE_ROWS, LANES), F32),
                        pltpu.VMEM((CHUNK, D_MODEL), BF16)],
        input_output_aliases=aliases,
        compiler_params=_params("arbitrary", "arbitrary"),
        name="mlstm_prompt",
    )(*args)


MLSTM_SAMPLE_BATCH = 4


def _mlstm_sample_step(p_ref, gc_ref, gr_ref, bc_ref, br_ref, gh_ref, ci_ref, ni_ref, mi_ref,
                       y_ref, co_ref, no_ref, mo_ref, layer, first):
    co_ref = _own_layer(co_ref, layer, first)

    def get_state(b, h, which):
        return {"c": lambda: ci_ref[b, h], "n": lambda: ni_ref[b, h:h + 1, :],
                "m": lambda: mi_ref[b, h:h + 1, 0:1]}[which]()

    def put_state(b, h, c_new, n_new, m_new):
        co_ref[b, h] = c_new
        no_ref[b, h:h + 1, :] = n_new
        mo_ref[b, h:h + 1, :] = jnp.broadcast_to(m_new, (1, LANES))

    def put_y(b, h, y):
        y_ref[b, :, h * HEAD_DIM:(h + 1) * HEAD_DIM] = y

    mo_ref[...] = jnp.zeros(mo_ref.shape, F32)
    seqs = range(MLSTM_SAMPLE_BATCH)
    _mlstm_chunk([p_ref.at[b] for b in seqs], [(gc_ref[b], gr_ref[b]) for b in seqs],
                 bc_ref[...], br_ref[...], gh_ref, DEC_SEQ, get_state, put_state, put_y)


def _mlstm_sample_io(p3, gate3, gate_t3, bias_col, bias_row, g_head, c_all, n_in, m_in, j, first, block):
    bs = MLSTM_SAMPLE_BATCH
    const = lambda shape: pl.BlockSpec(shape, lambda *ids: (0,) * len(shape))
    tok = lambda n: pl.BlockSpec((bs, DEC_SEQ, n), lambda *ids: (block(*ids), 0, 0))
    state_block = (bs, N_HEADS, HEAD_DIM, HEAD_DIM)
    c_spec = pl.BlockSpec((None,) + state_block, lambda *ids: (j, block(*ids), 0, 0, 0))
    c_out = (pl.BlockSpec((c_all.shape[0],) + state_block, lambda *ids: (0, block(*ids), 0, 0, 0))
             if first else c_spec)
    n_spec = pl.BlockSpec((bs, N_HEADS, HEAD_DIM), lambda *ids: (block(*ids), 0, 0))
    m_spec = pl.BlockSpec((bs, GATE_ROWS, LANES), lambda *ids: (block(*ids), 0, 0))
    in_specs = [tok(4 * D_MODEL), tok(LANES),
                pl.BlockSpec((bs, GATE_ROWS, DEC_SEQ), lambda *ids: (block(*ids), 0, 0)),
                const((1, LANES)), const((GATE_ROWS, 1)), const((1, D_MODEL)),
                c_spec, n_spec, m_spec]
    args = [p3, gate3, gate_t3, bias_col, bias_row, g_head, c_all, n_in, m_in]
    out_shapes = [jax.ShapeDtypeStruct((DEC_BATCH, DEC_SEQ, D_MODEL), F32),
                  jax.ShapeDtypeStruct(c_all.shape, F32),
                  jax.ShapeDtypeStruct((DEC_BATCH, N_HEADS, HEAD_DIM), F32),
                  jax.ShapeDtypeStruct((DEC_BATCH, GATE_ROWS, LANES), F32)]
    return in_specs, args, [tok(D_MODEL), c_out, n_spec, m_spec], out_shapes


def _norm_matmul_kernel(x_ref, g_ref, w_ref, o_ref):
    o_ref[...] = _dot(_rms(x_ref[...], g_ref[...]).astype(BF16), w_ref[...])


def _norm_matmul_sample(x, g, w_all, l):
    first = N_PROMPT // ROW_TILE
    return pl.pallas_call(
        _norm_matmul_kernel,
        grid=(N_SAMPLE // ROW_TILE,),
        in_specs=[pl.BlockSpec((ROW_TILE, D_MODEL), lambda r: (first + r, 0)),
                  pl.BlockSpec((1, D_MODEL), lambda r: (0, 0)),
                  _resident((None, D_MODEL, D_MODEL), lambda r: (l, 0, 0))],
        out_specs=pl.BlockSpec((ROW_TILE, D_MODEL), lambda r: (r, 0)),
        out_shape=jax.ShapeDtypeStruct((N_SAMPLE, D_MODEL), F32),
        compiler_params=_params("parallel"),
        name="norm_matmul_sample",
    )(x, g, w_all)


def _matmul_residual_kernel(x_ref, y_ref, w_ref, o_ref):
    o_ref[...] = x_ref[...] + _dot(y_ref[...].astype(BF16), w_ref[...])


def _matmul_residual_sample(x, y, w_all, l):
    first = N_PROMPT // ROW_TILE
    xrow = pl.BlockSpec((ROW_TILE, D_MODEL), lambda r: (first + r, 0))
    return pl.pallas_call(
        _matmul_residual_kernel,
        grid=(N_SAMPLE // ROW_TILE,),
        in_specs=[xrow, pl.BlockSpec((ROW_TILE, D_MODEL), lambda r: (r, 0)),
                  _resident((None, D_MODEL, D_MODEL), lambda r: (l, 0, 0))],
        out_specs=xrow,
        out_shape=jax.ShapeDtypeStruct((N_ROWS, D_MODEL), F32),
        input_output_aliases={0: 0},
        compiler_params=_params("parallel"),
        name="matmul_residual_sample",
    )(x, y, w_all)


MEM_KV_BATCH = 2


def _mem_kv_kernel(m_ref, w_ref, k_ref, v_ref, kb_ref, vb_ref):
    kv = _dot(m_ref[...].astype(BF16), w_ref[...])
    kb_ref[...] = kv[:, :D_MODEL].astype(BF16)
    vb_ref[...] = kv[:, D_MODEL:].astype(BF16)
    for b in range(MEM_KV_BATCH):
        rows = slice(b * N_MEM, (b + 1) * N_MEM)
        for h in range(N_XHEADS):
            lo, hi = h * XHEAD_DIM, (h + 1) * XHEAD_DIM
            k_ref[b, :, h, :] = kv[rows, lo:hi]
            v_ref[b, :, h, :] = kv[rows, D_MODEL + lo:D_MODEL + hi]


def _mem_kv(mem, wkv_all):
    bs = MEM_KV_BATCH
    out = pl.BlockSpec((None, bs, N_MEM, N_XHEADS, XHEAD_DIM), lambda l, r: (l, r, 0, 0, 0))
    shape = jax.ShapeDtypeStruct((DEPTH, BATCH, N_MEM, N_XHEADS, XHEAD_DIM), F32)
    out_b = pl.BlockSpec((None, bs * N_MEM, D_MODEL), lambda l, r: (l, r, 0))
    shape_b = jax.ShapeDtypeStruct((DEPTH, BATCH * N_MEM, D_MODEL), BF16)
    return pl.pallas_call(
        _mem_kv_kernel,
        grid=(DEPTH, BATCH // bs),
        in_specs=[pl.BlockSpec((bs * N_MEM, D_MODEL), lambda l, r: (r, 0)),
                  pl.BlockSpec((None, D_MODEL, 2 * D_MODEL), lambda l, r: (l, 0, 0))],
        out_specs=[out, out, out_b, out_b],
        out_shape=[shape, shape, shape_b, shape_b],
        compiler_params=_params("arbitrary", "arbitrary"),
        name="mem_kv",
    )(mem, wkv_all)


def _softmax_rows(s):
    e = jnp.exp(s - jnp.max(s, axis=1, keepdims=True))
    return e * (1.0 / jnp.sum(e, axis=1, keepdims=True))


N_MLSTM_SAMPLE_IN = 9


def _xattn_prompt_kernel(x_ref, g_ref, wq_ref, k_ref, v_ref, wo_ref, *rest, ride):
    a_sc = rest[-1]
    if ride is None:
        o_ref = rest[0]
    else:
        o_ref, y_ref, co_ref, no_ref, mo_ref = rest[-6:-1]
        _mlstm_sample_step(*rest[:N_MLSTM_SAMPLE_IN], y_ref, co_ref, no_ref, mo_ref, *ride)
    x = x_ref[...]
    q = _dot(_rms(x, g_ref[...]).astype(BF16), wq_ref[...])
    cols = [slice(h * XHEAD_DIM, (h + 1) * XHEAD_DIM) for h in range(N_XHEADS)]
    scores = [lax.dot_general(q[:, c].astype(BF16), k_ref[:, c], _NT,
                              preferred_element_type=F32) * (XHEAD_DIM ** -0.5) for c in cols]
    probs = [_softmax_rows(s).astype(BF16) for s in scores]
    for c, p in zip(cols, probs):
        a_sc[:, c] = _dot(p, v_ref[:, c]).astype(BF16)
    o_ref[...] = x + _dot(a_sc[...], wo_ref[...])


def _xattn_prompt(x, g, wq_all, k_all, v_all, wo_all, l, mlstm=None):
    tile = SEQ_TILE if mlstm is None else ROW_TILE
    nt = SEQ // tile
    xrow = pl.BlockSpec((tile, D_MODEL), lambda b, t: (b * nt + t, 0))
    kv = pl.BlockSpec((None, N_MEM, D_MODEL), lambda b, t: (l, b, 0))
    w = _resident((None, D_MODEL, D_MODEL), lambda b, t: (l, 0, 0))
    in_specs = [xrow, pl.BlockSpec((1, D_MODEL), lambda b, t: (0, 0)), w, kv, kv, w]
    args = [x, g, wq_all, k_all, v_all, wo_all]
    out_specs = [xrow]
    out_shape = [jax.ShapeDtypeStruct((N_ROWS, D_MODEL), F32)]
    aliases = {0: 0}
    ride = None
    if mlstm is not None:
        *step_args, j, c_prev = mlstm
        assert BATCH * nt * MLSTM_SAMPLE_BATCH == DEC_BATCH
        ride = (j, c_prev is None)
        s_in, s_args, s_out, s_shapes = _mlstm_sample_io(*step_args, *ride, lambda b, t: b * nt + t)
        assert len(s_in) == N_MLSTM_SAMPLE_IN
        in_specs += s_in
        args += s_args
        if c_prev is not None:
            in_specs.append(pl.BlockSpec(memory_space=pl.ANY))
            args.append(c_prev)
            aliases[len(args) - 1] = 2
        out_specs += s_out
        out_shape += s_shapes
    results = pl.pallas_call(
        functools.partial(_xattn_prompt_kernel, ride=ride),
        grid=(BATCH, nt),
        in_specs=in_specs,
        out_specs=out_specs,
        out_shape=out_shape,
        scratch_shapes=[pltpu.VMEM((tile, D_MODEL), BF16)],
        input_output_aliases=aliases,
        compiler_params=_params("arbitrary", "arbitrary"),
        name="xattn_prompt" + ("_mlstm" if ride else ""),
    )(*args)
    return results[0] if mlstm is None else results


def _sample_attention(qs, ks, vs):
    n_q, n_kv = N_XHEADS * DEC_SEQ, N_MEM * N_XHEADS
    q_head = lax.broadcasted_iota(jnp.int32, (n_q, n_kv), 0) // DEC_SEQ
    kv_head = lax.broadcasted_iota(jnp.int32, (n_q, n_kv), 1) % N_XHEADS
    same_head = q_head == kv_head
    scores = [lax.dot_general(q.astype(BF16), k.reshape(n_kv, XHEAD_DIM).astype(BF16), _NT,
                              preferred_element_type=F32) * (XHEAD_DIM ** -0.5)
              for q, k in zip(qs, ks)]
    probs = [_softmax_rows(jnp.where(same_head, s, -jnp.inf)).astype(BF16) for s in scores]
    return [_dot(p, v.reshape(n_kv, XHEAD_DIM).astype(BF16)) for p, v in zip(probs, vs)]


def _heads_major(a):
    a = a.reshape(DEC_BATCH, DEC_SEQ, N_XHEADS, XHEAD_DIM)
    return a.transpose(0, 2, 1, 3).reshape(DEC_BATCH, N_XHEADS * DEC_SEQ, XHEAD_DIM)


def _pool_prompt_kernel(x_ref, g_ref, win_ref, wgrp_ref, scale_ref, wout_ref,
                        o_ref, buf_ref, ext_sc, z_sc):
    tile = pl.program_id(1)

    @pl.when(tile == 0)
    def _():
        ext_sc[0:POOL_HALO, :] = jnp.zeros((POOL_HALO, D_MODEL), F32)

    x = x_ref[...]
    u = _dot(_rms(x, g_ref[...]).astype(BF16), win_ref[...])
    ext_sc[POOL_HALO:, :] = u
    pos = tile * SEQ_TILE + lax.broadcasted_iota(jnp.int32, (SEQ_TILE, 1), 0)
    for g, w in enumerate(POOL_WINDOWS):
        lo, hi = g * POOL_GROUP_DIM, (g + 1) * POOL_GROUP_DIM
        assert w & (w - 1) == 0 and w <= POOL_HALO
        win = ext_sc[:, lo:hi]
        span = 1
        while span < w:
            win = win + pltpu.roll(win, span, 0)
            span *= 2
        win = win[POOL_HALO:, :]
        cnt = jnp.minimum(pos + 1, w).astype(F32)
        d = win * (1.0 / cnt) - u[:, lo:hi]
        z_sc[:, lo:hi] = (_dot(d.astype(BF16), wgrp_ref[g]) * scale_ref[:, lo:hi]).astype(BF16)
    o_ref[...] = x + _dot(z_sc[...], wout_ref[...])
    tail = ext_sc[SEQ_TILE:SEQ_TILE + POOL_HALO, :]
    ext_sc[0:POOL_HALO, :] = tail

    @pl.when(tile == pl.num_programs(1) - 1)
    def _():
        buf_ref[...] = tail


def _pool_prompt(x, g, win_all, wgrp_all, scale, wout_all, j):
    nt = SEQ // SEQ_TILE
    xrow = pl.BlockSpec((SEQ_TILE, D_MODEL), lambda b, t: (b * nt + t, 0))
    vec = pl.BlockSpec((1, D_MODEL), lambda b, t: (0, 0))
    w = _resident((None, D_MODEL, D_MODEL), lambda b, t: (j, 0, 0))
    wgrp = _resident((None, len(POOL_WINDOWS), POOL_GROUP_DIM, POOL_GROUP_DIM),
                     lambda b, t: (j, 0, 0, 0))
    return pl.pallas_call(
        _pool_prompt_kernel,
        grid=(BATCH, nt),
        in_specs=[xrow, vec, w, wgrp, vec, w],
        out_specs=[xrow, pl.BlockSpec((None, POOL_HALO, D_MODEL), lambda b, t: (b, 0, 0))],
        out_shape=[jax.ShapeDtypeStruct((N_ROWS, D_MODEL), F32),
                   jax.ShapeDtypeStruct((BATCH, POOL_HALO, D_MODEL), F32)],
        scratch_shapes=[pltpu.VMEM((POOL_HALO + SEQ_TILE, D_MODEL), F32),
                        pltpu.VMEM((SEQ_TILE, D_MODEL), BF16)],
        input_output_aliases={0: 0},
        compiler_params=_params("arbitrary", "arbitrary"),
        name="pool_prompt",
    )(x, g, win_all, wgrp_all, scale, wout_all)


def _pool_sample_kernel(x_ref, buf_ref, g_ref, win_ref, wgrp_ref, scale_ref, *rest, layer, first):
    z_ref, nbuf_ref = rest[-2:]
    nbuf_ref = _own_layer(nbuf_ref, layer, first)
    u = _dot(_rms(x_ref[...], g_ref[...]).astype(BF16), win_ref[...])

    def ext(row, lo, hi):
        if row < POOL_BUF:
            return buf_ref[:, row, lo:hi]
        t = row - POOL_BUF
        return u[t * DEC_BATCH:(t + 1) * DEC_BATCH, lo:hi]

    for g, w in enumerate(POOL_WINDOWS):
        lo, hi = g * POOL_GROUP_DIM, (g + 1) * POOL_GROUP_DIM
        ds = []
        for t in range(DEC_SEQ):
            win = ext(POOL_BUF + t, lo, hi)
            for back in range(1, w):
                win = win + ext(POOL_BUF + t - back, lo, hi)
            cnt = float(min(PAST_LEN + 1 + t, w))
            ds.append(win / cnt - ext(POOL_BUF + t, lo, hi))
        d = jnp.concatenate(ds, axis=0).astype(BF16)
        z_ref[:, lo:hi] = _dot(d, wgrp_ref[g]) * scale_ref[:, lo:hi]
    for row in range(POOL_BUF):
        nbuf_ref[:, row, :] = ext(row + DEC_SEQ, 0, D_MODEL)


def _pool_sample(xs_t, buf_all, g, win_all, wgrp_all, scale, j, buf_prev):
    first = buf_prev is None
    buf_block = (DEC_BATCH, POOL_BUF, D_MODEL)
    once = lambda shape, imap: pl.BlockSpec(shape, imap, pipeline_mode=pl.Buffered(1))
    if first:
        nbuf = once((buf_all.shape[0],) + buf_block, lambda i: (0, 0, 0, 0))
    else:
        nbuf = once((None,) + buf_block, lambda i: (j, 0, 0, 0))
    vec = pl.BlockSpec((1, D_MODEL), lambda i: (0, 0))
    buf = once((None, DEC_BATCH, POOL_BUF, D_MODEL), lambda i: (j, 0, 0, 0))
    in_specs = [pl.BlockSpec((N_SAMPLE, D_MODEL), lambda i: (0, 0)), buf, vec,
                once((None, D_MODEL, D_MODEL), lambda i: (j, 0, 0)),
                once((None, len(POOL_WINDOWS), POOL_GROUP_DIM, POOL_GROUP_DIM), lambda i: (j, 0, 0, 0)),
                vec]
    args = [xs_t, buf_all, g, win_all, wgrp_all, scale]
    aliases = {}
    if buf_prev is not None:
        in_specs.append(pl.BlockSpec(memory_space=pl.ANY))
        args.append(buf_prev)
        aliases[len(args) - 1] = 1
    return pl.pallas_call(
        functools.partial(_pool_sample_kernel, layer=j, first=first),
        grid=(1,),
        in_specs=in_specs,
        out_specs=[pl.BlockSpec((N_SAMPLE, D_MODEL), lambda i: (0, 0)), nbuf],
        out_shape=[jax.ShapeDtypeStruct((N_SAMPLE, D_MODEL), F32),
                   jax.ShapeDtypeStruct(buf_all.shape, F32)],
        input_output_aliases=aliases,
        compiler_params=_params("arbitrary"),
        name="pool_sample",
    )(*args)


def _swap_token_seq(a, lead):
    n = a.shape[0] // lead
    return a.reshape(lead, n, a.shape[1]).transpose(1, 0, 2).reshape(a.shape)


def kernel(x_prompt, x_sample, mem_prompt, cache_mem_k, cache_mem_v, state_mlstm_C, state_mlstm_n,
           state_mlstm_m, state_pool_buf, norm_g, final_g, ffn_w_up, ffn_w_down, mlstm_w_in,
           mlstm_b_i, mlstm_b_f, mlstm_g_head, mlstm_w_out, pool_w_in, pool_w_grp, pool_scale,
           pool_w_out, xattn_w_q, xattn_w_kv, xattn_w_o):
    n_mlstm = mlstm_w_in.shape[0]
    wup = ffn_w_up.astype(BF16)
    wdn = ffn_w_down.astype(BF16)
    w_in = mlstm_w_in.astype(BF16)
    w_gate = jnp.pad(mlstm_w_in[:, :, 4 * D_MODEL:],
                     ((0, 0), (0, 0), (0, LANES - 2 * N_HEADS))).astype(BF16)
    w_mout = mlstm_w_out.astype(BF16)
    gate_bias = jnp.concatenate([mlstm_b_i, mlstm_b_f], axis=1).astype(F32)
    p_win = pool_w_in.astype(BF16)
    p_wgrp = pool_w_grp.astype(BF16)
    p_wout = pool_w_out.astype(BF16)
    wq = xattn_w_q.astype(BF16)
    wkv = xattn_w_kv.astype(BF16)
    wo = xattn_w_o.astype(BF16)

    mem_k, mem_v, mem_kb, mem_vb = _mem_kv(mem_prompt.reshape(BATCH * N_MEM, D_MODEL), wkv)

    x = [x_prompt.reshape(N_PROMPT, D_MODEL), x_sample.reshape(N_SAMPLE, D_MODEL)]
    c_p = c_s = buf_s = None
    n_p, m_p, n_s, m_s, buf_p = [], [], [], [], []
    for l in range(DEPTH):
        g = norm_g[l].reshape(4, 1, D_MODEL)
        j = l // 2
        x = _ffn(x if l == 0 else [x], g[0], wup, wdn, l, 0)
        if l % 2 == 0:
            p_p, gate_p, p_s, gate_s = _mlstm_proj(x, g[1], w_in, w_gate, j)
            bias_col = jnp.pad(gate_bias[j], (0, LANES - GATE_ROWS)).reshape(1, LANES)
            bias_row = gate_bias[j].reshape(GATE_ROWS, 1)
            g_head = mlstm_g_head[j].reshape(1, D_MODEL)
            x, c_p, n1, m1 = _mlstm_prompt(x, p_p, gate_p, gate_p[:, :GATE_ROWS].T, bias_col, bias_row,
                                           g_head, w_mout, j, n_mlstm, c_p)
            gate_s = gate_s.reshape(DEC_BATCH, DEC_SEQ, LANES)
            m_in = jnp.broadcast_to(
                jnp.pad(state_mlstm_m[j], ((0, 0), (0, GATE_ROWS - N_HEADS)))[:, :, None],
                (DEC_BATCH, GATE_ROWS, LANES))
            x, y_s, c_s, n2, m2 = _xattn_prompt(
                x, g[2], wq, mem_kb, mem_vb, wo, l,
                mlstm=(p_s.reshape(DEC_BATCH, DEC_SEQ, 4 * D_MODEL), gate_s,
                       gate_s[:, :, :GATE_ROWS].transpose(0, 2, 1),
                       bias_col, bias_row, g_head, state_mlstm_C, state_mlstm_n[j], m_in, j, c_s))
            x = _matmul_residual_sample(x, y_s.reshape(N_SAMPLE, D_MODEL), w_mout, j)
            n_p.append(n1); m_p.append(m1[:, :N_HEADS, 0])
            n_s.append(n2); m_s.append(m2[:, :N_HEADS, 0])
        else:
            scale = pool_scale[j].reshape(1, D_MODEL)
            x, b1 = _pool_prompt(x, g[1], p_win, p_wgrp, scale, p_wout, j)
            xs_t = _swap_token_seq(x[N_PROMPT:], DEC_BATCH)
            z_t, buf_s = _pool_sample(xs_t, state_pool_buf, g[1], p_win, p_wgrp, scale, j, buf_s)
            x = _matmul_residual_sample(x, _swap_token_seq(z_t, DEC_SEQ), p_wout, j)
            buf_p.append(b1[:, POOL_HALO - POOL_BUF:])
            x = _xattn_prompt(x, g[2], wq, mem_kb, mem_vb, wo, l)
        q_s = _norm_matmul_sample(x, g[2], wq, l)
        x = _ffn([x], g[3], wup, wdn, l, 1,
                 final_g=final_g.reshape(1, D_MODEL) if l == DEPTH - 1 else None,
                 attend=(_heads_major(q_s), cache_mem_k, cache_mem_v, wo))

    y_prompt, y_sample = x
    return (y_prompt.reshape(BATCH, SEQ, D_MODEL), y_sample.reshape(DEC_BATCH, DEC_SEQ, D_MODEL),
            mem_k, mem_v, c_p, jnp.stack(n_p), jnp.stack(m_p),
            c_s, jnp.stack(n_s), jnp.stack(m_s), jnp.stack(buf_p), buf_s)
```

```python
import functools

import jax
import jax.numpy as jnp
from jax import lax
from jax.experimental import pallas as pl
from jax.experimental.pallas import tpu as pltpu

F32 = jnp.float32
BF16 = jnp.bfloat16

D_MODEL = 1024
BATCH = 8
SEQ = 2048
DEPTH = 4
DEC_BATCH = 128
DEC_SEQ = 4
PAST_LEN = 16384
N_HEADS = 4
HEAD_DIM = D_MODEL // N_HEADS
CHUNK = 256
POOL_WINDOWS = (2, 4, 8, 16)
POOL_GROUP_DIM = D_MODEL // len(POOL_WINDOWS)
POOL_BUF = max(POOL_WINDOWS) - 1
POOL_HALO = 16
N_MEM = 256
N_XHEADS = 4
XHEAD_DIM = D_MODEL // N_XHEADS
D_FF = ((8 * D_MODEL // 3 + 127) // 128) * 128
EPS = 1e-6

N_PROMPT = BATCH * SEQ
N_SAMPLE = DEC_BATCH * DEC_SEQ
N_ROWS = N_PROMPT + N_SAMPLE

LANES = 128
GATE_ROWS = 8
ROW_TILE = 512
SEQ_TILE = 1024
PROMPT_TILES = N_PROMPT // ROW_TILE
assert N_SAMPLE == ROW_TILE
VMEM_LIMIT = 56 * 1024 * 1024

_NT = (((1,), (1,)), ((), ()))
_TN = (((0,), (0,)), ((), ()))


def _params(*sem):
    return pltpu.CompilerParams(dimension_semantics=sem, vmem_limit_bytes=VMEM_LIMIT)


def _resident(block_shape, index_map):
    return pl.BlockSpec(block_shape, index_map, pipeline_mode=pl.Buffered(1))


def _rms(x, g):
    return x * lax.rsqrt(jnp.mean(x * x, axis=-1, keepdims=True) + EPS) * g


def _log_sigmoid(x):
    return jnp.minimum(x, 0.0) - jnp.log1p(jnp.exp(-jnp.abs(x)))


def _dot(a, b):
    return jnp.dot(a, b, preferred_element_type=F32)


ATTEND_BATCH = DEC_BATCH // PROMPT_TILES
assert ATTEND_BATCH * PROMPT_TILES == DEC_BATCH


def _ffn_kernel(*refs, first, final, attend, n_cast):
    refs = list(refs)
    step = pl.program_id(0)
    is_prompt = step < PROMPT_TILES
    if first:
        xp_ref, xs_ref = refs[:2]
        del refs[:2]
        x = jnp.where(is_prompt, xp_ref[...], xs_ref[...])
    else:
        x = refs.pop(0)[...]
    g_ref, wup_ref, wdn_ref = refs[:3]
    del refs[:3]
    cast_in = refs[:2 * n_cast]
    del refs[:2 * n_cast]
    if attend:
        q_ref, k_ref, v_ref, wo_ref = refs[:4]
        del refs[:4]
        a_sc = refs.pop()

        def attended():
            y = x
            for h in range(N_XHEADS):
                a = a_sc[h].reshape(N_SAMPLE, XHEAD_DIM).astype(BF16)
                y = y + _dot(a, wo_ref[h * XHEAD_DIM:(h + 1) * XHEAD_DIM, :])
            return y

        x = lax.cond(is_prompt, lambda: x, attended)
        seqs = range(ATTEND_BATCH)
        first_seq = jnp.minimum(step, PROMPT_TILES - 1) * ATTEND_BATCH
        outs = _sample_attention([q_ref[b] for b in seqs], [k_ref[b] for b in seqs],
                                 [v_ref[b] for b in seqs])
        for b, o in enumerate(outs):
            for h in range(N_XHEADS):
                a_sc[h, first_seq + b] = o[h * DEC_SEQ:(h + 1) * DEC_SEQ, :]
    rest = refs
    xn = _rms(x, g_ref[...]).astype(BF16)
    gate = _dot(xn, wup_ref[:, :D_FF])
    up = _dot(xn, wup_ref[:, D_FF:])
    act = (gate * jax.nn.sigmoid(gate) * up).astype(BF16)
    y = x + 0.5 * _dot(act, wdn_ref[...])
    if not final:
        o_ref, *cast_out = rest
        o_ref[...] = y
        for src, dst in zip(cast_in, cast_out):
            dst[...] = src[...].astype(BF16)
        return
    fg_ref, op_ref, os_ref = rest
    y = _rms(y, fg_ref[...])

    @pl.when(is_prompt)
    def _():
        op_ref[...] = y

    @pl.when(jnp.logical_not(is_prompt))
    def _():
        os_ref[...] = y


CAST_UP_ROWS = 32
CAST_DN_ROWS = 128
assert D_MODEL % CAST_UP_ROWS == 0 and D_MODEL // CAST_UP_ROWS <= PROMPT_TILES
assert D_FF % CAST_DN_ROWS == 0 and D_FF // CAST_DN_ROWS <= PROMPT_TILES


def _ffn(xs, g, wup, wdn, l, final_g=None, attend=None, cast=()):
    first, final = len(xs) == 2, final_g is not None
    assert not (cast and (final or attend))
    row = pl.BlockSpec((ROW_TILE, D_MODEL), lambda r: (r, 0))
    prompt_row = pl.BlockSpec((ROW_TILE, D_MODEL), lambda r: (jnp.minimum(r, PROMPT_TILES - 1), 0))
    sample_row = pl.BlockSpec((ROW_TILE, D_MODEL), lambda r: (0, 0))
    vec = pl.BlockSpec((1, D_MODEL), lambda r: (0, 0))
    in_specs = ([prompt_row, sample_row] if first else [row]) + [
        vec, _resident(wup.shape, lambda r: (0, 0)), _resident(wdn.shape, lambda r: (0, 0))]
    args = list(xs) + [g, wup, wdn]
    cast_specs, cast_shapes = [], []
    for w_up_all, w_dn_all, cl, ci in cast:
        up_step = lambda r: jnp.minimum(r, D_MODEL // CAST_UP_ROWS - 1)
        dn_step = lambda r: jnp.minimum(r, D_FF // CAST_DN_ROWS - 1)
        in_specs += [pl.BlockSpec((None, None, CAST_UP_ROWS, 2 * D_FF),
                                  lambda r, cl=cl, ci=ci: (cl, ci, up_step(r), 0)),
                     pl.BlockSpec((None, None, CAST_DN_ROWS, D_MODEL),
                                  lambda r, cl=cl, ci=ci: (cl, ci, dn_step(r), 0))]
        args += [w_up_all, w_dn_all]
        cast_specs += [pl.BlockSpec((CAST_UP_ROWS, 2 * D_FF), lambda r: (up_step(r), 0)),
                       pl.BlockSpec((CAST_DN_ROWS, D_MODEL), lambda r: (dn_step(r), 0))]
        cast_shapes += [jax.ShapeDtypeStruct((D_MODEL, 2 * D_FF), BF16),
                        jax.ShapeDtypeStruct((D_FF, D_MODEL), BF16)]
    scratch = []
    if attend is not None:
        bs = ATTEND_BATCH
        tile = lambda r: jnp.minimum(r, PROMPT_TILES - 1)
        cache = pl.BlockSpec((None, bs, N_MEM, N_XHEADS, XHEAD_DIM), lambda r: (l, tile(r), 0, 0, 0))
        in_specs += [pl.BlockSpec((bs, N_XHEADS * DEC_SEQ, XHEAD_DIM), lambda r: (tile(r), 0, 0)),
                     cache, cache, _resident((None, D_MODEL, D_MODEL), lambda r: (l, 0, 0))]
        args += list(attend)
        scratch = [pltpu.VMEM((N_XHEADS, DEC_BATCH, DEC_SEQ, XHEAD_DIM), F32)]
    if final:
        in_specs.append(vec)
        args.append(final_g)
        out_specs = [prompt_row, sample_row]
        out_shape = [jax.ShapeDtypeStruct((N_PROMPT, D_MODEL), F32),
                     jax.ShapeDtypeStruct((N_SAMPLE, D_MODEL), F32)]
    else:
        out_specs = [row] + cast_specs
        out_shape = [jax.ShapeDtypeStruct((N_ROWS, D_MODEL), F32)] + cast_shapes
    results = pl.pallas_call(
        functools.partial(_ffn_kernel, first=first, final=final, attend=attend is not None,
                          n_cast=len(cast)),
        grid=(N_ROWS // ROW_TILE,),
        in_specs=in_specs,
        out_specs=out_specs,
        out_shape=out_shape,
        scratch_shapes=scratch,
        compiler_params=_params("arbitrary"),
        name=("ffn_final" if final else "ffn_first" if first else "ffn")
        + ("_attend" if scratch else "") + ("_cast" if cast else ""),
    )(*args)
    if final:
        return results
    return results[0], [tuple(results[1 + 2 * k:3 + 2 * k]) for k in range(len(cast))]


def _prompt_rows(n):
    return pl.BlockSpec((ROW_TILE, n), lambda r: (jnp.minimum(r, PROMPT_TILES - 1), 0))


def _sample_rows(n):
    return pl.BlockSpec((ROW_TILE, n), lambda r: (0, 0))


def _mlstm_proj_kernel(x_ref, g_ref, w_ref, wg_ref, pp_ref, gp_ref, ps_ref, gs_ref):
    xn = _rms(x_ref[...], g_ref[...]).astype(BF16)
    is_prompt = pl.program_id(0) < PROMPT_TILES

    @pl.when(is_prompt)
    def _():
        pp_ref[...] = _dot(xn, w_ref[...])
        gp_ref[...] = _dot(xn, wg_ref[...])

    @pl.when(jnp.logical_not(is_prompt))
    def _():
        ps_ref[...] = _dot(xn, w_ref[...])
        gs_ref[...] = _dot(xn, wg_ref[...])


def _mlstm_proj(x, g, w_all, wg_all, j):
    return pl.pallas_call(
        _mlstm_proj_kernel,
        grid=(N_ROWS // ROW_TILE,),
        in_specs=[pl.BlockSpec((ROW_TILE, D_MODEL), lambda r: (r, 0)),
                  pl.BlockSpec((1, D_MODEL), lambda r: (0, 0)),
                  _resident((None, D_MODEL, 4 * D_MODEL), lambda r: (j, 0, 0)),
                  _resident((None, D_MODEL, LANES), lambda r: (j, 0, 0))],
        out_specs=[_prompt_rows(4 * D_MODEL), _prompt_rows(LANES),
                   _sample_rows(4 * D_MODEL), _sample_rows(LANES)],
        out_shape=[jax.ShapeDtypeStruct((N_PROMPT, 4 * D_MODEL), F32),
                   jax.ShapeDtypeStruct((N_PROMPT, LANES), F32),
                   jax.ShapeDtypeStruct((N_SAMPLE, 4 * D_MODEL), F32),
                   jax.ShapeDtypeStruct((N_SAMPLE, LANES), F32)],
        compiler_params=_params("arbitrary"),
        name="mlstm_proj",
    )(x, g, w_all, wg_all)


def _gate_terms(gate_col, gate_row, bias_col, bias_row, length):
    r = lax.broadcasted_iota(jnp.int32, (length, length), 0)
    c = lax.broadcasted_iota(jnp.int32, (length, length), 1)
    mask = r >= c
    lower = mask.astype(F32)
    upper = (r <= c).astype(F32)
    zc = gate_col + bias_col
    lane = lax.broadcasted_iota(jnp.int32, zc.shape, 1)
    log_col = jnp.where(lane < N_HEADS, zc, _log_sigmoid(zc))
    cum_col = jnp.dot(lower, log_col, precision=lax.Precision.HIGHEST,
                      preferred_element_type=F32)
    zr = gate_row + bias_row
    sub = lax.broadcasted_iota(jnp.int32, zr.shape, 0)
    log_row = jnp.where(sub < N_HEADS, zr, _log_sigmoid(zr))
    cum_row = jnp.dot(log_row, upper, precision=lax.Precision.HIGHEST,
                      preferred_element_type=F32)
    return mask, log_col, cum_col, log_row, cum_row


def _mlstm_chunk(p_views, gates, bias_col, bias_row, g_head_ref, length, get_state, put_state, put_y):
    pairs = [(s, h) for s in range(len(p_views)) for h in range(N_HEADS)]
    terms = [_gate_terms(gc, gr, bias_col, bias_row, length) for gc, gr in gates]
    mask = terms[0][0]

    def cols(h):
        return slice(h * HEAD_DIM, (h + 1) * HEAD_DIM)

    def part(s, h, which):
        lo = which * D_MODEL + h * HEAD_DIM
        return p_views[s][:, lo:lo + HEAD_DIM]

    b_col, b_row, i_col, i_row = {}, {}, {}, {}
    for s, h in pairs:
        _, log_col, cum_col, log_row, cum_row = terms[s]
        f = N_HEADS + h
        b_col[s, h], b_row[s, h] = cum_col[:, f:f + 1], cum_row[f:f + 1, :]
        i_col[s, h], i_row[s, h] = log_col[:, h:h + 1], log_row[h:h + 1, :]

    a, dlog, mt, dw, inter = {}, {}, {}, {}, {}
    for sh in pairs:
        a[sh] = b_col[sh] + get_state(*sh, "m")
        dlog[sh] = jnp.where(mask, b_col[sh] - b_row[sh] + i_row[sh], -jnp.inf)
    for sh in pairs:
        mt[sh] = jnp.maximum(a[sh], jnp.max(dlog[sh], axis=1, keepdims=True))
    for sh in pairs:
        dw[sh] = jnp.exp(dlog[sh] - mt[sh])
        inter[sh] = jnp.exp(a[sh] - mt[sh])

    sc, q_c, q_n, vb, k32 = {}, {}, {}, {}, {}
    for s, h in pairs:
        q = part(s, h, 0)
        k32[s, h] = part(s, h, 1) * (HEAD_DIM ** -0.5)
        qb = q.astype(BF16)
        sc[s, h] = lax.dot_general(qb, k32[s, h].astype(BF16), _NT,
                                   preferred_element_type=F32) * dw[s, h]
        q_c[s, h] = _dot(qb, get_state(s, h, "c").astype(BF16))
        q_n[s, h] = jnp.sum(q * get_state(s, h, "n"), axis=1, keepdims=True)
    hid = {}
    for s, h in pairs:
        vb[s, h] = part(s, h, 2).astype(BF16)
        num = inter[s, h] * q_c[s, h] + _dot(sc[s, h].astype(BF16), vb[s, h])
        den = inter[s, h] * q_n[s, h] + jnp.sum(sc[s, h], axis=1, keepdims=True)
        hid[s, h] = num * (1.0 / jnp.maximum(jnp.abs(den), jnp.exp(-mt[s, h])))
    for s, h in pairs:
        hh = hid[s, h]
        hn = hh * lax.rsqrt(jnp.mean(hh * hh, axis=1, keepdims=True) + EPS) * g_head_ref[:, cols(h)]
        put_y(s, h, jax.nn.sigmoid(part(s, h, 3)) * hn)
    for s, h in pairs:
        c_state, n_state, m_state = (get_state(s, h, which) for which in "cnm")
        m_new = mt[s, h][length - 1:length, :]
        b_last = b_col[s, h][length - 1:length, :]
        decay = jnp.exp(b_last + m_state - m_new)
        kw = k32[s, h] * jnp.exp(b_last - b_col[s, h] + i_col[s, h] - m_new)
        c_new = decay * c_state + lax.dot_general(kw.astype(BF16), vb[s, h], _TN,
                                                  preferred_element_type=F32)
        n_new = decay * n_state + jnp.sum(kw, axis=0, keepdims=True)
        put_state(s, h, c_new, n_new, m_new)


def _own_layer(ref, layer, first):
    if not first:
        return ref
    for other in range(ref.shape[0]):
        if other != layer:
            ref[other] = jnp.zeros(ref.shape[1:], ref.dtype)
    return ref.at[layer]


def _mlstm_prompt_kernel(p_ref, gc_ref, gr_ref, x_ref, bc_ref, br_ref, gh_ref, wo_ref, *rest,
                         layer, first):
    xo_ref, co_ref, no_ref, mo_ref, c_sc, n_sc, m_sc, y_sc = rest[-8:]
    chunk = pl.program_id(1)

    @pl.when(chunk == 0)
    def _():
        c_sc[...] = jnp.zeros_like(c_sc)
        n_sc[...] = jnp.zeros_like(n_sc)
        m_sc[...] = jnp.zeros_like(m_sc)

    def get_state(_, h, which):
        return {"c": lambda: c_sc[h], "n": lambda: n_sc[h:h + 1, :],
                "m": lambda: m_sc[h:h + 1, 0:1]}[which]()

    def put_state(_, h, c_new, n_new, m_new):
        c_sc[h] = c_new
        n_sc[h:h + 1, :] = n_new
        m_sc[h:h + 1, :] = jnp.broadcast_to(m_new, (1, LANES))

    def put_y(_, h, y):
        y_sc[:, h * HEAD_DIM:(h + 1) * HEAD_DIM] = y.astype(BF16)

    _mlstm_chunk([p_ref], [(gc_ref[...], gr_ref[...])], bc_ref[...], br_ref[...], gh_ref, CHUNK,
                 get_state, put_state, put_y)
    xo_ref[...] = x_ref[...] + _dot(y_sc[...], wo_ref[...])

    @pl.when(chunk == pl.num_programs(1) - 1)
    def _():
        _own_layer(co_ref, layer, first)[...] = c_sc[...]
        no_ref[...] = n_sc[0:N_HEADS, :]
        mo_ref[...] = m_sc[...]


def _mlstm_prompt(x, p, gate, gate_t, bias_col, bias_row, g_head, wo_all, j, n_layers, c_prev):
    nc = SEQ // CHUNK
    first = c_prev is None
    state_block = (None, N_HEADS, HEAD_DIM, HEAD_DIM)
    if first:
        c_spec = pl.BlockSpec((n_layers,) + state_block, lambda b, c: (0, b, 0, 0, 0))
    else:
        c_spec = pl.BlockSpec((None,) + state_block, lambda b, c: (j, b, 0, 0, 0))
    tok = lambda n: pl.BlockSpec((CHUNK, n), lambda b, c: (b * nc + c, 0))
    const = lambda shape: pl.BlockSpec(shape, lambda b, c: (0,) * len(shape))
    in_specs = [tok(4 * D_MODEL), tok(LANES),
                pl.BlockSpec((GATE_ROWS, CHUNK), lambda b, c: (0, b * nc + c)),
                tok(D_MODEL), const((1, LANES)), const((GATE_ROWS, 1)), const((1, D_MODEL)),
                _resident((None, D_MODEL, D_MODEL), lambda b, c: (j, 0, 0))]
    args = [p, gate, gate_t, x, bias_col, bias_row, g_head, wo_all]
    aliases = {3: 0}
    if c_prev is not None:
        in_specs.append(pl.BlockSpec(memory_space=pl.ANY))
        args.append(c_prev)
        aliases[len(args) - 1] = 1
    return pl.pallas_call(
        functools.partial(_mlstm_prompt_kernel, layer=j, first=first),
        grid=(BATCH, nc),
        in_specs=in_specs,
        out_specs=[tok(D_MODEL), c_spec,
                   pl.BlockSpec((None, N_HEADS, HEAD_DIM), lambda b, c: (b, 0, 0)),
                   pl.BlockSpec((None, GATE_ROWS, LANES), lambda b, c: (b, 0, 0))],
        out_shape=[jax.ShapeDtypeStruct((N_ROWS, D_MODEL), F32),
                   jax.ShapeDtypeStruct((n_layers, BATCH, N_HEADS, HEAD_DIM, HEAD_DIM), F32),
                   jax.ShapeDtypeStruct((BATCH, N_HEADS, HEAD_DIM), F32),
                   jax.ShapeDtypeStruct((BATCH, GATE_ROWS, LANES), F32)],
        scratch_shapes=[pltpu.VMEM((N_HEADS, HEAD_DIM, HEAD_DIM), F32),
                        pltpu.VMEM((GATE_ROWS, HEAD_DIM), F32),
                        pltpu.VMEM((GATE_ROWS, LANES), F32),
                        pltpu.VMEM((CHUNK, D_MODEL), BF16)],
        input_output_aliases=aliases,
        compiler_params=_params("arbitrary", "arbitrary"),
        name="mlstm_prompt",
    )(*args)


MLSTM_SAMPLE_BATCH = 4


def _mlstm_sample_step(p_ref, gc_ref, gr_ref, bc_ref, br_ref, gh_ref, ci_ref, ni_ref, mi_ref,
                       y_ref, co_ref, no_ref, mo_ref, layer, first):
    co_ref = _own_layer(co_ref, layer, first)

    def get_state(b, h, which):
        return {"c": lambda: ci_ref[b, h], "n": lambda: ni_ref[b, h:h + 1, :],
                "m": lambda: mi_ref[b, h:h + 1, 0:1]}[which]()

    def put_state(b, h, c_new, n_new, m_new):
        co_ref[b, h] = c_new
        no_ref[b, h:h + 1, :] = n_new
        mo_ref[b, h:h + 1, :] = jnp.broadcast_to(m_new, (1, LANES))

    def put_y(b, h, y):
        y_ref[b, :, h * HEAD_DIM:(h + 1) * HEAD_DIM] = y

    mo_ref[...] = jnp.zeros(mo_ref.shape, F32)
    seqs = range(MLSTM_SAMPLE_BATCH)
    _mlstm_chunk([p_ref.at[b] for b in seqs], [(gc_ref[b], gr_ref[b]) for b in seqs],
                 bc_ref[...], br_ref[...], gh_ref, DEC_SEQ, get_state, put_state, put_y)


def _mlstm_sample_io(p3, gate3, gate_t3, bias_col, bias_row, g_head, c_all, n_in, m_in, j, first, block):
    bs = MLSTM_SAMPLE_BATCH
    const = lambda shape: pl.BlockSpec(shape, lambda *ids: (0,) * len(shape))
    tok = lambda n: pl.BlockSpec((bs, DEC_SEQ, n), lambda *ids: (block(*ids), 0, 0))
    state_block = (bs, N_HEADS, HEAD_DIM, HEAD_DIM)
    c_spec = pl.BlockSpec((None,) + state_block, lambda *ids: (j, block(*ids), 0, 0, 0))
    c_out = (pl.BlockSpec((c_all.shape[0],) + state_block, lambda *ids: (0, block(*ids), 0, 0, 0))
             if first else c_spec)
    n_spec = pl.BlockSpec((bs, N_HEADS, HEAD_DIM), lambda *ids: (block(*ids), 0, 0))
    m_spec = pl.BlockSpec((bs, GATE_ROWS, LANES), lambda *ids: (block(*ids), 0, 0))
    in_specs = [tok(4 * D_MODEL), tok(LANES),
                pl.BlockSpec((bs, GATE_ROWS, DEC_SEQ), lambda *ids: (block(*ids), 0, 0)),
                const((1, LANES)), const((GATE_ROWS, 1)), const((1, D_MODEL)),
                c_spec, n_spec, m_spec]
    args = [p3, gate3, gate_t3, bias_col, bias_row, g_head, c_all, n_in, m_in]
    out_shapes = [jax.ShapeDtypeStruct((DEC_BATCH, DEC_SEQ, D_MODEL), F32),
                  jax.ShapeDtypeStruct(c_all.shape, F32),
                  jax.ShapeDtypeStruct((DEC_BATCH, N_HEADS, HEAD_DIM), F32),
                  jax.ShapeDtypeStruct((DEC_BATCH, GATE_ROWS, LANES), F32)]
    return in_specs, args, [tok(D_MODEL), c_out, n_spec, m_spec], out_shapes


def _norm_matmul_kernel(x_ref, g_ref, w_ref, o_ref):
    o_ref[...] = _dot(_rms(x_ref[...], g_ref[...]).astype(BF16), w_ref[...])


def _norm_matmul_sample(x, g, w_all, l):
    first = N_PROMPT // ROW_TILE
    return pl.pallas_call(
        _norm_matmul_kernel,
        grid=(N_SAMPLE // ROW_TILE,),
        in_specs=[pl.BlockSpec((ROW_TILE, D_MODEL), lambda r: (first + r, 0)),
                  pl.BlockSpec((1, D_MODEL), lambda r: (0, 0)),
                  _resident((None, D_MODEL, D_MODEL), lambda r: (l, 0, 0))],
        out_specs=pl.BlockSpec((ROW_TILE, D_MODEL), lambda r: (r, 0)),
        out_shape=jax.ShapeDtypeStruct((N_SAMPLE, D_MODEL), F32),
        compiler_params=_params("parallel"),
        name="norm_matmul_sample",
    )(x, g, w_all)


def _matmul_residual_kernel(x_ref, y_ref, w_ref, o_ref):
    o_ref[...] = x_ref[...] + _dot(y_ref[...].astype(BF16), w_ref[...])


def _matmul_residual_sample(x, y, w_all, l):
    first = N_PROMPT // ROW_TILE
    xrow = pl.BlockSpec((ROW_TILE, D_MODEL), lambda r: (first + r, 0))
    return pl.pallas_call(
        _matmul_residual_kernel,
        grid=(N_SAMPLE // ROW_TILE,),
        in_specs=[xrow, pl.BlockSpec((ROW_TILE, D_MODEL), lambda r: (r, 0)),
                  _resident((None, D_MODEL, D_MODEL), lambda r: (l, 0, 0))],
        out_specs=xrow,
        out_shape=jax.ShapeDtypeStruct((N_ROWS, D_MODEL), F32),
        input_output_aliases={0: 0},
        compiler_params=_params("parallel"),
        name="matmul_residual_sample",
    )(x, y, w_all)


MEM_KV_BATCH = 2


def _mem_kv_kernel(m_ref, w_ref, k_ref, v_ref, kb_ref, vb_ref):
    kv = _dot(m_ref[...].astype(BF16), w_ref[...])
    kb_ref[...] = kv[:, :D_MODEL].astype(BF16)
    vb_ref[...] = kv[:, D_MODEL:].astype(BF16)
    for b in range(MEM_KV_BATCH):
        rows = slice(b * N_MEM, (b + 1) * N_MEM)
        for h in range(N_XHEADS):
            lo, hi = h * XHEAD_DIM, (h + 1) * XHEAD_DIM
            k_ref[b, :, h, :] = kv[rows, lo:hi]
            v_ref[b, :, h, :] = kv[rows, D_MODEL + lo:D_MODEL + hi]


def _mem_kv(mem, wkv_all):
    bs = MEM_KV_BATCH
    out = pl.BlockSpec((None, bs, N_MEM, N_XHEADS, XHEAD_DIM), lambda l, r: (l, r, 0, 0, 0))
    shape = jax.ShapeDtypeStruct((DEPTH, BATCH, N_MEM, N_XHEADS, XHEAD_DIM), F32)
    out_b = pl.BlockSpec((None, bs * N_MEM, D_MODEL), lambda l, r: (l, r, 0))
    shape_b = jax.ShapeDtypeStruct((DEPTH, BATCH * N_MEM, D_MODEL), BF16)
    return pl.pallas_call(
        _mem_kv_kernel,
        grid=(DEPTH, BATCH // bs),
        in_specs=[pl.BlockSpec((bs * N_MEM, D_MODEL), lambda l, r: (r, 0)),
                  pl.BlockSpec((None, D_MODEL, 2 * D_MODEL), lambda l, r: (l, 0, 0))],
        out_specs=[out, out, out_b, out_b],
        out_shape=[shape, shape, shape_b, shape_b],
        compiler_params=_params("arbitrary", "arbitrary"),
        name="mem_kv",
    )(mem, wkv_all)


def _softmax_rows(s):
    e = jnp.exp(s - jnp.max(s, axis=1, keepdims=True))
    return e * (1.0 / jnp.sum(e, axis=1, keepdims=True))


N_MLSTM_SAMPLE_IN = 9


def _xattn_prompt_kernel(x_ref, g_ref, wq_ref, k_ref, v_ref, wo_ref, *rest, ride):
    a_sc = rest[-1]
    if ride is None:
        o_ref = rest[0]
    else:
        o_ref, y_ref, co_ref, no_ref, mo_ref = rest[-6:-1]
        _mlstm_sample_step(*rest[:N_MLSTM_SAMPLE_IN], y_ref, co_ref, no_ref, mo_ref, *ride)
    x = x_ref[...]
    q = _dot(_rms(x, g_ref[...]).astype(BF16), wq_ref[...])
    cols = [slice(h * XHEAD_DIM, (h + 1) * XHEAD_DIM) for h in range(N_XHEADS)]
    scores = [lax.dot_general(q[:, c].astype(BF16), k_ref[:, c], _NT,
                              preferred_element_type=F32) * (XHEAD_DIM ** -0.5) for c in cols]
    probs = [_softmax_rows(s).astype(BF16) for s in scores]
    for c, p in zip(cols, probs):
        a_sc[:, c] = _dot(p, v_ref[:, c]).astype(BF16)
    o_ref[...] = x + _dot(a_sc[...], wo_ref[...])


def _xattn_prompt(x, g, wq_all, k_all, v_all, wo_all, l, mlstm=None):
    tile = SEQ_TILE if mlstm is None else ROW_TILE
    nt = SEQ // tile
    xrow = pl.BlockSpec((tile, D_MODEL), lambda b, t: (b * nt + t, 0))
    kv = pl.BlockSpec((None, N_MEM, D_MODEL), lambda b, t: (l, b, 0))
    w = _resident((None, D_MODEL, D_MODEL), lambda b, t: (l, 0, 0))
    in_specs = [xrow, pl.BlockSpec((1, D_MODEL), lambda b, t: (0, 0)), w, kv, kv, w]
    args = [x, g, wq_all, k_all, v_all, wo_all]
    out_specs = [xrow]
    out_shape = [jax.ShapeDtypeStruct((N_ROWS, D_MODEL), F32)]
    aliases = {0: 0}
    ride = None
    if mlstm is not None:
        *step_args, j, c_prev = mlstm
        assert BATCH * nt * MLSTM_SAMPLE_BATCH == DEC_BATCH
        ride = (j, c_prev is None)
        s_in, s_args, s_out, s_shapes = _mlstm_sample_io(*step_args, *ride, lambda b, t: b * nt + t)
        assert len(s_in) == N_MLSTM_SAMPLE_IN
        in_specs += s_in
        args += s_args
        if c_prev is not None:
            in_specs.append(pl.BlockSpec(memory_space=pl.ANY))
            args.append(c_prev)
            aliases[len(args) - 1] = 2
        out_specs += s_out
        out_shape += s_shapes
    results = pl.pallas_call(
        functools.partial(_xattn_prompt_kernel, ride=ride),
        grid=(BATCH, nt),
        in_specs=in_specs,
        out_specs=out_specs,
        out_shape=out_shape,
        scratch_shapes=[pltpu.VMEM((tile, D_MODEL), BF16)],
        input_output_aliases=aliases,
        compiler_params=_params("arbitrary", "arbitrary"),
        name="xattn_prompt" + ("_mlstm" if ride else ""),
    )(*args)
    return results[0] if mlstm is None else results


def _sample_attention(qs, ks, vs):
    n_q, n_kv = N_XHEADS * DEC_SEQ, N_MEM * N_XHEADS
    q_head = lax.broadcasted_iota(jnp.int32, (n_q, n_kv), 0) // DEC_SEQ
    kv_head = lax.broadcasted_iota(jnp.int32, (n_q, n_kv), 1) % N_XHEADS
    same_head = q_head == kv_head
    scores = [lax.dot_general(q.astype(BF16), k.reshape(n_kv, XHEAD_DIM).astype(BF16), _NT,
                              preferred_element_type=F32) * (XHEAD_DIM ** -0.5)
              for q, k in zip(qs, ks)]
    probs = [_softmax_rows(jnp.where(same_head, s, -jnp.inf)).astype(BF16) for s in scores]
    return [_dot(p, v.reshape(n_kv, XHEAD_DIM).astype(BF16)) for p, v in zip(probs, vs)]


def _heads_major(a):
    a = a.reshape(DEC_BATCH, DEC_SEQ, N_XHEADS, XHEAD_DIM)
    return a.transpose(0, 2, 1, 3).reshape(DEC_BATCH, N_XHEADS * DEC_SEQ, XHEAD_DIM)


def _pool_prompt_kernel(x_ref, g_ref, win_ref, wgrp_ref, scale_ref, wout_ref,
                        o_ref, buf_ref, ext_sc, z_sc):
    tile = pl.program_id(1)

    @pl.when(tile == 0)
    def _():
        ext_sc[0:POOL_HALO, :] = jnp.zeros((POOL_HALO, D_MODEL), F32)

    x = x_ref[...]
    u = _dot(_rms(x, g_ref[...]).astype(BF16), win_ref[...])
    ext_sc[POOL_HALO:, :] = u
    pos = tile * SEQ_TILE + lax.broadcasted_iota(jnp.int32, (SEQ_TILE, 1), 0)
    for g, w in enumerate(POOL_WINDOWS):
        lo, hi = g * POOL_GROUP_DIM, (g + 1) * POOL_GROUP_DIM
        assert w & (w - 1) == 0 and w <= POOL_HALO
        win = ext_sc[:, lo:hi]
        span = 1
        while span < w:
            win = win + pltpu.roll(win, span, 0)
            span *= 2
        win = win[POOL_HALO:, :]
        cnt = jnp.minimum(pos + 1, w).astype(F32)
        d = win * (1.0 / cnt) - u[:, lo:hi]
        z_sc[:, lo:hi] = (_dot(d.astype(BF16), wgrp_ref[g]) * scale_ref[:, lo:hi]).astype(BF16)
    o_ref[...] = x + _dot(z_sc[...], wout_ref[...])
    tail = ext_sc[SEQ_TILE:SEQ_TILE + POOL_HALO, :]
    ext_sc[0:POOL_HALO, :] = tail

    @pl.when(tile == pl.num_programs(1) - 1)
    def _():
        buf_ref[...] = tail


def _pool_prompt(x, g, win_all, wgrp_all, scale, wout_all, j):
    nt = SEQ // SEQ_TILE
    xrow = pl.BlockSpec((SEQ_TILE, D_MODEL), lambda b, t: (b * nt + t, 0))
    vec = pl.BlockSpec((1, D_MODEL), lambda b, t: (0, 0))
    w = _resident((None, D_MODEL, D_MODEL), lambda b, t: (j, 0, 0))
    wgrp = _resident((None, len(POOL_WINDOWS), POOL_GROUP_DIM, POOL_GROUP_DIM),
                     lambda b, t: (j, 0, 0, 0))
    return pl.pallas_call(
        _pool_prompt_kernel,
        grid=(BATCH, nt),
        in_specs=[xrow, vec, w, wgrp, vec, w],
        out_specs=[xrow, pl.BlockSpec((None, POOL_HALO, D_MODEL), lambda b, t: (b, 0, 0))],
        out_shape=[jax.ShapeDtypeStruct((N_ROWS, D_MODEL), F32),
                   jax.ShapeDtypeStruct((BATCH, POOL_HALO, D_MODEL), F32)],
        scratch_shapes=[pltpu.VMEM((POOL_HALO + SEQ_TILE, D_MODEL), F32),
                        pltpu.VMEM((SEQ_TILE, D_MODEL), BF16)],
        input_output_aliases={0: 0},
        compiler_params=_params("arbitrary", "arbitrary"),
        name="pool_prompt",
    )(x, g, win_all, wgrp_all, scale, wout_all)


def _pool_sample_kernel(x_ref, buf_ref, g_ref, win_ref, wgrp_ref, scale_ref, *rest, layer, first):
    z_ref, nbuf_ref = rest[-2:]
    nbuf_ref = _own_layer(nbuf_ref, layer, first)
    u = _dot(_rms(x_ref[...], g_ref[...]).astype(BF16), win_ref[...])

    def ext(row, lo, hi):
        if row < POOL_BUF:
            return buf_ref[:, row, lo:hi]
        t = row - POOL_BUF
        return u[t * DEC_BATCH:(t + 1) * DEC_BATCH, lo:hi]

    for g, w in enumerate(POOL_WINDOWS):
        lo, hi = g * POOL_GROUP_DIM, (g + 1) * POOL_GROUP_DIM
        ds = []
        for t in range(DEC_SEQ):
            win = ext(POOL_BUF + t, lo, hi)
            for back in range(1, w):
                win = win + ext(POOL_BUF + t - back, lo, hi)
            cnt = float(min(PAST_LEN + 1 + t, w))
            ds.append(win / cnt - ext(POOL_BUF + t, lo, hi))
        d = jnp.concatenate(ds, axis=0).astype(BF16)
        z_ref[:, lo:hi] = _dot(d, wgrp_ref[g]) * scale_ref[:, lo:hi]
    for row in range(POOL_BUF):
        nbuf_ref[:, row, :] = ext(row + DEC_SEQ, 0, D_MODEL)


def _pool_sample(xs_t, buf_all, g, win_all, wgrp_all, scale, j, buf_prev):
    first = buf_prev is None
    buf_block = (DEC_BATCH, POOL_BUF, D_MODEL)
    once = lambda shape, imap: pl.BlockSpec(shape, imap, pipeline_mode=pl.Buffered(1))
    if first:
        nbuf = once((buf_all.shape[0],) + buf_block, lambda i: (0, 0, 0, 0))
    else:
        nbuf = once((None,) + buf_block, lambda i: (j, 0, 0, 0))
    vec = pl.BlockSpec((1, D_MODEL), lambda i: (0, 0))
    buf = once((None, DEC_BATCH, POOL_BUF, D_MODEL), lambda i: (j, 0, 0, 0))
    in_specs = [pl.BlockSpec((N_SAMPLE, D_MODEL), lambda i: (0, 0)), buf, vec,
                once((None, D_MODEL, D_MODEL), lambda i: (j, 0, 0)),
                once((None, len(POOL_WINDOWS), POOL_GROUP_DIM, POOL_GROUP_DIM), lambda i: (j, 0, 0, 0)),
                vec]
    args = [xs_t, buf_all, g, win_all, wgrp_all, scale]
    aliases = {}
    if buf_prev is not None:
        in_specs.append(pl.BlockSpec(memory_space=pl.ANY))
        args.append(buf_prev)
        aliases[len(args) - 1] = 1
    return pl.pallas_call(
        functools.partial(_pool_sample_kernel, layer=j, first=first),
        grid=(1,),
        in_specs=in_specs,
        out_specs=[pl.BlockSpec((N_SAMPLE, D_MODEL), lambda i: (0, 0)), nbuf],
        out_shape=[jax.ShapeDtypeStruct((N_SAMPLE, D_MODEL), F32),
                   jax.ShapeDtypeStruct(buf_all.shape, F32)],
        input_output_aliases=aliases,
        compiler_params=_params("arbitrary"),
        name="pool_sample",
    )(*args)


def _swap_token_seq(a, lead):
    n = a.shape[0] // lead
    return a.reshape(lead, n, a.shape[1]).transpose(1, 0, 2).reshape(a.shape)


def kernel(x_prompt, x_sample, mem_prompt, cache_mem_k, cache_mem_v, state_mlstm_C, state_mlstm_n,
           state_mlstm_m, state_pool_buf, norm_g, final_g, ffn_w_up, ffn_w_down, mlstm_w_in,
           mlstm_b_i, mlstm_b_f, mlstm_g_head, mlstm_w_out, pool_w_in, pool_w_grp, pool_scale,
           pool_w_out, xattn_w_q, xattn_w_kv, xattn_w_o):
    n_mlstm = mlstm_w_in.shape[0]
    ffn_w = {(0, 0): (ffn_w_up[0, 0].astype(BF16), ffn_w_down[0, 0].astype(BF16))}
    w_in = mlstm_w_in.astype(BF16)
    w_gate = jnp.pad(mlstm_w_in[:, :, 4 * D_MODEL:],
                     ((0, 0), (0, 0), (0, LANES - 2 * N_HEADS))).astype(BF16)
    w_mout = mlstm_w_out.astype(BF16)
    gate_bias = jnp.concatenate([mlstm_b_i, mlstm_b_f], axis=1).astype(F32)
    p_win = pool_w_in.astype(BF16)
    p_wgrp = pool_w_grp.astype(BF16)
    p_wout = pool_w_out.astype(BF16)
    wq = xattn_w_q.astype(BF16)
    wkv = xattn_w_kv.astype(BF16)
    wo = xattn_w_o.astype(BF16)

    mem_k, mem_v, mem_kb, mem_vb = _mem_kv(mem_prompt.reshape(BATCH * N_MEM, D_MODEL), wkv)

    x = [x_prompt.reshape(N_PROMPT, D_MODEL), x_sample.reshape(N_SAMPLE, D_MODEL)]
    c_p = c_s = buf_s = None
    n_p, m_p, n_s, m_s, buf_p = [], [], [], [], []
    for l in range(DEPTH):
        g = norm_g[l].reshape(4, 1, D_MODEL)
        j = l // 2
        later = [(l, 1)] + ([(l + 1, 0)] if l + 1 < DEPTH else [])
        x, cast = _ffn(x if l == 0 else [x], g[0], *ffn_w[l, 0], l,
                       cast=[(ffn_w_up, ffn_w_down) + key for key in later])
        ffn_w.update(zip(later, cast))
        if l % 2 == 0:
            p_p, gate_p, p_s, gate_s = _mlstm_proj(x, g[1], w_in, w_gate, j)
            bias_col = jnp.pad(gate_bias[j], (0, LANES - GATE_ROWS)).reshape(1, LANES)
            bias_row = gate_bias[j].reshape(GATE_ROWS, 1)
            g_head = mlstm_g_head[j].reshape(1, D_MODEL)
            x, c_p, n1, m1 = _mlstm_prompt(x, p_p, gate_p, gate_p[:, :GATE_ROWS].T, bias_col, bias_row,
                                           g_head, w_mout, j, n_mlstm, c_p)
            gate_s = gate_s.reshape(DEC_BATCH, DEC_SEQ, LANES)
            m_in = jnp.broadcast_to(
                jnp.pad(state_mlstm_m[j], ((0, 0), (0, GATE_ROWS - N_HEADS)))[:, :, None],
                (DEC_BATCH, GATE_ROWS, LANES))
            x, y_s, c_s, n2, m2 = _xattn_prompt(
                x, g[2], wq, mem_kb, mem_vb, wo, l,
                mlstm=(p_s.reshape(DEC_BATCH, DEC_SEQ, 4 * D_MODEL), gate_s,
                       gate_s[:, :, :GATE_ROWS].transpose(0, 2, 1),
                       bias_col, bias_row, g_head, state_mlstm_C, state_mlstm_n[j], m_in, j, c_s))
            x = _matmul_residual_sample(x, y_s.reshape(N_SAMPLE, D_MODEL), w_mout, j)
            n_p.append(n1); m_p.append(m1[:, :N_HEADS, 0])
            n_s.append(n2); m_s.append(m2[:, :N_HEADS, 0])
        else:
            scale = pool_scale[j].reshape(1, D_MODEL)
            x, b1 = _pool_prompt(x, g[1], p_win, p_wgrp, scale, p_wout, j)
            xs_t = _swap_token_seq(x[N_PROMPT:], DEC_BATCH)
            z_t, buf_s = _pool_sample(xs_t, state_pool_buf, g[1], p_win, p_wgrp, scale, j, buf_s)
            x = _matmul_residual_sample(x, _swap_token_seq(z_t, DEC_SEQ), p_wout, j)
            buf_p.append(b1[:, POOL_HALO - POOL_BUF:])
            x = _xattn_prompt(x, g[2], wq, mem_kb, mem_vb, wo, l)
        q_s = _norm_matmul_sample(x, g[2], wq, l)
        last = l == DEPTH - 1
        out = _ffn([x], g[3], *ffn_w[l, 1], l, final_g=final_g.reshape(1, D_MODEL) if last else None,
                   attend=(_heads_major(q_s), cache_mem_k, cache_mem_v, wo))
        x = out if last else out[0]

    y_prompt, y_sample = x
    return (y_prompt.reshape(BATCH, SEQ, D_MODEL), y_sample.reshape(DEC_BATCH, DEC_SEQ, D_MODEL),
            mem_k, mem_v, c_p, jnp.stack(n_p), jnp.stack(m_p),
            c_s, jnp.stack(n_s), jnp.stack(m_s), jnp.stack(buf_p), buf_s)
```

```python
import functools

import jax
import jax.numpy as jnp
from jax import lax
from jax.experimental import pallas as pl
from jax.experimental.pallas import tpu as pltpu

F32 = jnp.float32
BF16 = jnp.bfloat16

D_MODEL = 1024
BATCH = 8
SEQ = 2048
DEPTH = 4
DEC_BATCH = 128
DEC_SEQ = 4
PAST_LEN = 16384
N_HEADS = 4
HEAD_DIM = D_MODEL // N_HEADS
CHUNK = 256
POOL_WINDOWS = (2, 4, 8, 16)
POOL_GROUP_DIM = D_MODEL // len(POOL_WINDOWS)
POOL_BUF = max(POOL_WINDOWS) - 1
POOL_HALO = 16
N_MEM = 256
N_XHEADS = 4
XHEAD_DIM = D_MODEL // N_XHEADS
D_FF = ((8 * D_MODEL // 3 + 127) // 128) * 128
EPS = 1e-6

N_PROMPT = BATCH * SEQ
N_SAMPLE = DEC_BATCH * DEC_SEQ
N_ROWS = N_PROMPT + N_SAMPLE

LANES = 128
GATE_ROWS = 8
ROW_TILE = 512
SEQ_TILE = 1024
PROMPT_TILES = N_PROMPT // ROW_TILE
assert N_SAMPLE == ROW_TILE
VMEM_LIMIT = 56 * 1024 * 1024

_NT = (((1,), (1,)), ((), ()))
_TN = (((0,), (0,)), ((), ()))


def _params(*sem):
    return pltpu.CompilerParams(dimension_semantics=sem, vmem_limit_bytes=VMEM_LIMIT)


def _resident(block_shape, index_map):
    return pl.BlockSpec(block_shape, index_map, pipeline_mode=pl.Buffered(1))


def _rms(x, g):
    return x * lax.rsqrt(jnp.mean(x * x, axis=-1, keepdims=True) + EPS) * g


def _log_sigmoid(x):
    return jnp.minimum(x, 0.0) - jnp.log1p(jnp.exp(-jnp.abs(x)))


def _dot(a, b):
    return jnp.dot(a, b, preferred_element_type=F32)


ATTEND_BATCH = DEC_BATCH // PROMPT_TILES
assert ATTEND_BATCH * PROMPT_TILES == DEC_BATCH


def _ffn_kernel(*refs, first, final, attend, n_cast):
    refs = list(refs)
    step = pl.program_id(0)
    is_prompt = step < PROMPT_TILES
    if first:
        xp_ref, xs_ref = refs[:2]
        del refs[:2]
        x = jnp.where(is_prompt, xp_ref[...], xs_ref[...])
    else:
        x = refs.pop(0)[...]
    g_ref, wup_ref, wdn_ref = refs[:3]
    del refs[:3]
    cast_in = refs[:n_cast]
    del refs[:n_cast]
    if attend:
        q_ref, k_ref, v_ref, wo_ref = refs[:4]
        del refs[:4]
        a_sc = refs.pop()

        def attended():
            y = x
            for h in range(N_XHEADS):
                a = a_sc[h].reshape(N_SAMPLE, XHEAD_DIM).astype(BF16)
                y = y + _dot(a, wo_ref[h * XHEAD_DIM:(h + 1) * XHEAD_DIM, :])
            return y

        x = lax.cond(is_prompt, lambda: x, attended)
        seqs = range(ATTEND_BATCH)
        first_seq = jnp.minimum(step, PROMPT_TILES - 1) * ATTEND_BATCH
        outs = _sample_attention([q_ref[b] for b in seqs], [k_ref[b] for b in seqs],
                                 [v_ref[b] for b in seqs])
        for b, o in enumerate(outs):
            for h in range(N_XHEADS):
                a_sc[h, first_seq + b] = o[h * DEC_SEQ:(h + 1) * DEC_SEQ, :]
    rest = refs
    xn = _rms(x, g_ref[...]).astype(BF16)
    gate = _dot(xn, wup_ref[:, :D_FF])
    up = _dot(xn, wup_ref[:, D_FF:])
    act = (gate * jax.nn.sigmoid(gate) * up).astype(BF16)
    y = x + 0.5 * _dot(act, wdn_ref[...])
    if not final:
        o_ref, *cast_out = rest
        o_ref[...] = y
        for src, dst in zip(cast_in, cast_out):
            dst[...] = src[...].astype(BF16)
        return
    fg_ref, op_ref, os_ref = rest
    y = _rms(y, fg_ref[...])

    @pl.when(is_prompt)
    def _():
        op_ref[...] = y

    @pl.when(jnp.logical_not(is_prompt))
    def _():
        os_ref[...] = y


BF16_ROWS = 16


def _cast_job(w, part=0, parts=1, slabs=PROMPT_TILES):
    w = w.reshape(-1, w.shape[-1])
    rows = w.shape[0] // parts
    assert rows * parts == w.shape[0] and rows % slabs == 0
    return w, rows // slabs, part * slabs, slabs


def _ffn(xs, g, wup, wdn, l, final_g=None, attend=None, cast=()):
    first, final = len(xs) == 2, final_g is not None
    assert not (cast and (final or attend))
    row = pl.BlockSpec((ROW_TILE, D_MODEL), lambda r: (r, 0))
    prompt_row = pl.BlockSpec((ROW_TILE, D_MODEL), lambda r: (jnp.minimum(r, PROMPT_TILES - 1), 0))
    sample_row = pl.BlockSpec((ROW_TILE, D_MODEL), lambda r: (0, 0))
    vec = pl.BlockSpec((1, D_MODEL), lambda r: (0, 0))
    in_specs = ([prompt_row, sample_row] if first else [row]) + [
        vec, _resident(wup.shape, lambda r: (0, 0)), _resident(wdn.shape, lambda r: (0, 0))]
    args = list(xs) + [g, wup, wdn]
    cast_specs, cast_shapes = [], []
    for w, rows, first_slab, n_slabs in cast:
        assert rows % BF16_ROWS == 0 and n_slabs <= PROMPT_TILES
        slab = lambda r, n_slabs=n_slabs: jnp.minimum(r, n_slabs - 1)
        in_specs.append(pl.BlockSpec((rows, w.shape[1]),
                                     lambda r, slab=slab, first_slab=first_slab: (first_slab + slab(r), 0)))
        args.append(w)
        cast_specs.append(pl.BlockSpec((rows, w.shape[1]), lambda r, slab=slab: (slab(r), 0)))
        cast_shapes.append(jax.ShapeDtypeStruct((rows * n_slabs, w.shape[1]), BF16))
    scratch = []
    if attend is not None:
        bs = ATTEND_BATCH
        tile = lambda r: jnp.minimum(r, PROMPT_TILES - 1)
        cache = pl.BlockSpec((None, bs, N_MEM, N_XHEADS, XHEAD_DIM), lambda r: (l, tile(r), 0, 0, 0))
        in_specs += [pl.BlockSpec((bs, N_XHEADS * DEC_SEQ, XHEAD_DIM), lambda r: (tile(r), 0, 0)),
                     cache, cache, _resident((None, D_MODEL, D_MODEL), lambda r: (l, 0, 0))]
        args += list(attend)
        scratch = [pltpu.VMEM((N_XHEADS, DEC_BATCH, DEC_SEQ, XHEAD_DIM), F32)]
    if final:
        in_specs.append(vec)
        args.append(final_g)
        out_specs = [prompt_row, sample_row]
        out_shape = [jax.ShapeDtypeStruct((N_PROMPT, D_MODEL), F32),
                     jax.ShapeDtypeStruct((N_SAMPLE, D_MODEL), F32)]
    else:
        out_specs = [row] + cast_specs
        out_shape = [jax.ShapeDtypeStruct((N_ROWS, D_MODEL), F32)] + cast_shapes
    results = pl.pallas_call(
        functools.partial(_ffn_kernel, first=first, final=final, attend=attend is not None,
                          n_cast=len(cast)),
        grid=(N_ROWS // ROW_TILE,),
        in_specs=in_specs,
        out_specs=out_specs,
        out_shape=out_shape,
        scratch_shapes=scratch,
        compiler_params=_params("arbitrary"),
        name=("ffn_final" if final else "ffn_first" if first else "ffn")
        + ("_attend" if scratch else "") + ("_cast" if cast else ""),
    )(*args)
    if final:
        return results
    return results[0], list(results[1:])


def _prompt_rows(n):
    return pl.BlockSpec((ROW_TILE, n), lambda r: (jnp.minimum(r, PROMPT_TILES - 1), 0))


def _sample_rows(n):
    return pl.BlockSpec((ROW_TILE, n), lambda r: (0, 0))


def _mlstm_proj_kernel(x_ref, g_ref, w_ref, wg_ref, pp_ref, gp_ref, ps_ref, gs_ref):
    xn = _rms(x_ref[...], g_ref[...]).astype(BF16)
    is_prompt = pl.program_id(0) < PROMPT_TILES

    @pl.when(is_prompt)
    def _():
        pp_ref[...] = _dot(xn, w_ref[...])
        gp_ref[...] = _dot(xn, wg_ref[...])

    @pl.when(jnp.logical_not(is_prompt))
    def _():
        ps_ref[...] = _dot(xn, w_ref[...])
        gs_ref[...] = _dot(xn, wg_ref[...])


def _mlstm_proj(x, g, w_all, wg_all, j):
    return pl.pallas_call(
        _mlstm_proj_kernel,
        grid=(N_ROWS // ROW_TILE,),
        in_specs=[pl.BlockSpec((ROW_TILE, D_MODEL), lambda r: (r, 0)),
                  pl.BlockSpec((1, D_MODEL), lambda r: (0, 0)),
                  _resident((None, D_MODEL, 4 * D_MODEL), lambda r: (j, 0, 0)),
                  _resident((None, D_MODEL, LANES), lambda r: (j, 0, 0))],
        out_specs=[_prompt_rows(4 * D_MODEL), _prompt_rows(LANES),
                   _sample_rows(4 * D_MODEL), _sample_rows(LANES)],
        out_shape=[jax.ShapeDtypeStruct((N_PROMPT, 4 * D_MODEL), F32),
                   jax.ShapeDtypeStruct((N_PROMPT, LANES), F32),
                   jax.ShapeDtypeStruct((N_SAMPLE, 4 * D_MODEL), F32),
                   jax.ShapeDtypeStruct((N_SAMPLE, LANES), F32)],
        compiler_params=_params("arbitrary"),
        name="mlstm_proj",
    )(x, g, w_all, wg_all)


def _gate_terms(gate_col, gate_row, bias_col, bias_row, length):
    r = lax.broadcasted_iota(jnp.int32, (length, length), 0)
    c = lax.broadcasted_iota(jnp.int32, (length, length), 1)
    mask = r >= c
    lower = mask.astype(F32)
    upper = (r <= c).astype(F32)
    zc = gate_col + bias_col
    lane = lax.broadcasted_iota(jnp.int32, zc.shape, 1)
    log_col = jnp.where(lane < N_HEADS, zc, _log_sigmoid(zc))
    cum_col = jnp.dot(lower, log_col, precision=lax.Precision.HIGHEST,
                      preferred_element_type=F32)
    zr = gate_row + bias_row
    sub = lax.broadcasted_iota(jnp.int32, zr.shape, 0)
    log_row = jnp.where(sub < N_HEADS, zr, _log_sigmoid(zr))
    cum_row = jnp.dot(log_row, upper, precision=lax.Precision.HIGHEST,
                      preferred_element_type=F32)
    return mask, log_col, cum_col, log_row, cum_row


def _mlstm_chunk(p_views, gates, bias_col, bias_row, g_head_ref, length, get_state, put_state, put_y):
    pairs = [(s, h) for s in range(len(p_views)) for h in range(N_HEADS)]
    terms = [_gate_terms(gc, gr, bias_col, bias_row, length) for gc, gr in gates]
    mask = terms[0][0]

    def cols(h):
        return slice(h * HEAD_DIM, (h + 1) * HEAD_DIM)

    def part(s, h, which):
        lo = which * D_MODEL + h * HEAD_DIM
        return p_views[s][:, lo:lo + HEAD_DIM]

    b_col, b_row, i_col, i_row = {}, {}, {}, {}
    for s, h in pairs:
        _, log_col, cum_col, log_row, cum_row = terms[s]
        f = N_HEADS + h
        b_col[s, h], b_row[s, h] = cum_col[:, f:f + 1], cum_row[f:f + 1, :]
        i_col[s, h], i_row[s, h] = log_col[:, h:h + 1], log_row[h:h + 1, :]

    a, dlog, mt, dw, inter = {}, {}, {}, {}, {}
    for sh in pairs:
        a[sh] = b_col[sh] + get_state(*sh, "m")
        dlog[sh] = jnp.where(mask, b_col[sh] - b_row[sh] + i_row[sh], -jnp.inf)
    for sh in pairs:
        mt[sh] = jnp.maximum(a[sh], jnp.max(dlog[sh], axis=1, keepdims=True))
    for sh in pairs:
        dw[sh] = jnp.exp(dlog[sh] - mt[sh])
        inter[sh] = jnp.exp(a[sh] - mt[sh])

    sc, q_c, q_n, vb, k32 = {}, {}, {}, {}, {}
    for s, h in pairs:
        q = part(s, h, 0)
        k32[s, h] = part(s, h, 1) * (HEAD_DIM ** -0.5)
        qb = q.astype(BF16)
        sc[s, h] = lax.dot_general(qb, k32[s, h].astype(BF16), _NT,
                                   preferred_element_type=F32) * dw[s, h]
        q_c[s, h] = _dot(qb, get_state(s, h, "c").astype(BF16))
        q_n[s, h] = jnp.sum(q * get_state(s, h, "n"), axis=1, keepdims=True)
    hid = {}
    for s, h in pairs:
        vb[s, h] = part(s, h, 2).astype(BF16)
        num = inter[s, h] * q_c[s, h] + _dot(sc[s, h].astype(BF16), vb[s, h])
        den = inter[s, h] * q_n[s, h] + jnp.sum(sc[s, h], axis=1, keepdims=True)
        hid[s, h] = num * (1.0 / jnp.maximum(jnp.abs(den), jnp.exp(-mt[s, h])))
    for s, h in pairs:
        hh = hid[s, h]
        hn = hh * lax.rsqrt(jnp.mean(hh * hh, axis=1, keepdims=True) + EPS) * g_head_ref[:, cols(h)]
        put_y(s, h, jax.nn.sigmoid(part(s, h, 3)) * hn)
    for s, h in pairs:
        c_state, n_state, m_state = (get_state(s, h, which) for which in "cnm")
        m_new = mt[s, h][length - 1:length, :]
        b_last = b_col[s, h][length - 1:length, :]
        decay = jnp.exp(b_last + m_state - m_new)
        kw = k32[s, h] * jnp.exp(b_last - b_col[s, h] + i_col[s, h] - m_new)
        c_new = decay * c_state + lax.dot_general(kw.astype(BF16), vb[s, h], _TN,
                                                  preferred_element_type=F32)
        n_new = decay * n_state + jnp.sum(kw, axis=0, keepdims=True)
        put_state(s, h, c_new, n_new, m_new)


def _own_layer(ref, layer, first):
    if not first:
        return ref
    for other in range(ref.shape[0]):
        if other != layer:
            ref[other] = jnp.zeros(ref.shape[1:], ref.dtype)
    return ref.at[layer]


def _mlstm_prompt_kernel(p_ref, gc_ref, gr_ref, x_ref, bc_ref, br_ref, gh_ref, wo_ref, *rest,
                         layer, first):
    xo_ref, co_ref, no_ref, mo_ref, c_sc, n_sc, m_sc, y_sc = rest[-8:]
    chunk = pl.program_id(1)

    @pl.when(chunk == 0)
    def _():
        c_sc[...] = jnp.zeros_like(c_sc)
        n_sc[...] = jnp.zeros_like(n_sc)
        m_sc[...] = jnp.zeros_like(m_sc)

    def get_state(_, h, which):
        return {"c": lambda: c_sc[h], "n": lambda: n_sc[h:h + 1, :],
                "m": lambda: m_sc[h:h + 1, 0:1]}[which]()

    def put_state(_, h, c_new, n_new, m_new):
        c_sc[h] = c_new
        n_sc[h:h + 1, :] = n_new
        m_sc[h:h + 1, :] = jnp.broadcast_to(m_new, (1, LANES))

    def put_y(_, h, y):
        y_sc[:, h * HEAD_DIM:(h + 1) * HEAD_DIM] = y.astype(BF16)

    _mlstm_chunk([p_ref], [(gc_ref[...], gr_ref[...])], bc_ref[...], br_ref[...], gh_ref, CHUNK,
                 get_state, put_state, put_y)
    xo_ref[...] = x_ref[...] + _dot(y_sc[...], wo_ref[...])

    @pl.when(chunk == pl.num_programs(1) - 1)
    def _():
        _own_layer(co_ref, layer, first)[...] = c_sc[...]
        no_ref[...] = n_sc[0:N_HEADS, :]
        mo_ref[...] = m_sc[...]


def _mlstm_prompt(x, p, gate, gate_t, bias_col, bias_row, g_head, wo_all, j, n_layers, c_prev):
    nc = SEQ // CHUNK
    first = c_prev is None
    state_block = (None, N_HEADS, HEAD_DIM, HEAD_DIM)
    if first:
        c_spec = pl.BlockSpec((n_layers,) + state_block, lambda b, c: (0, b, 0, 0, 0))
    else:
        c_spec = pl.BlockSpec((None,) + state_block, lambda b, c: (j, b, 0, 0, 0))
    tok = lambda n: pl.BlockSpec((CHUNK, n), lambda b, c: (b * nc + c, 0))
    const = lambda shape: pl.BlockSpec(shape, lambda b, c: (0,) * len(shape))
    in_specs = [tok(4 * D_MODEL), tok(LANES),
                pl.BlockSpec((GATE_ROWS, CHUNK), lambda b, c: (0, b * nc + c)),
                tok(D_MODEL), const((1, LANES)), const((GATE_ROWS, 1)), const((1, D_MODEL)),
                _resident((None, D_MODEL, D_MODEL), lambda b, c: (j, 0, 0))]
    args = [p, gate, gate_t, x, bias_col, bias_row, g_head, wo_all]
    aliases = {3: 0}
    if c_prev is not None:
        in_specs.append(pl.BlockSpec(memory_space=pl.ANY))
        args.append(c_prev)
        aliases[len(args) - 1] = 1
    return pl.pallas_call(
        functools.partial(_mlstm_prompt_kernel, layer=j, first=first),
        grid=(BATCH, nc),
        in_specs=in_specs,
        out_specs=[tok(D_MODEL), c_spec,
                   pl.BlockSpec((None, N_HEADS, HEAD_DIM), lambda b, c: (b, 0, 0)),
                   pl.BlockSpec((None, GATE_ROWS, LANES), lambda b, c: (b, 0, 0))],
        out_shape=[jax.ShapeDtypeStruct((N_ROWS, D_MODEL), F32),
                   jax.ShapeDtypeStruct((n_layers, BATCH, N_HEADS, HEAD_DIM, HEAD_DIM), F32),
                   jax.ShapeDtypeStruct((BATCH, N_HEADS, HEAD_DIM), F32),
                   jax.ShapeDtypeStruct((BATCH, GATE_ROWS, LANES), F32)],
        scratch_shapes=[pltpu.VMEM((N_HEADS, HEAD_DIM, HEAD_DIM), F32),
                        pltpu.VMEM((GATE_ROWS, HEAD_DIM), F32),
                        pltpu.VMEM((GATE_ROWS, LANES), F32),
                        pltpu.VMEM((CHUNK, D_MODEL), BF16)],
        input_output_aliases=aliases,
        compiler_params=_params("arbitrary", "arbitrary"),
        name="mlstm_prompt",
    )(*args)


MLSTM_SAMPLE_BATCH = 4


def _mlstm_sample_step(p_ref, gc_ref, gr_ref, bc_ref, br_ref, gh_ref, ci_ref, ni_ref, mi_ref,
                       y_ref, co_ref, no_ref, mo_ref, layer, first):
    co_ref = _own_layer(co_ref, layer, first)

    def get_state(b, h, which):
        return {"c": lambda: ci_ref[b, h], "n": lambda: ni_ref[b, h:h + 1, :],
                "m": lambda: mi_ref[b, h:h + 1, 0:1]}[which]()

    def put_state(b, h, c_new, n_new, m_new):
        co_ref[b, h] = c_new
        no_ref[b, h:h + 1, :] = n_new
        mo_ref[b, h:h + 1, :] = jnp.broadcast_to(m_new, (1, LANES))

    def put_y(b, h, y):
        y_ref[b, :, h * HEAD_DIM:(h + 1) * HEAD_DIM] = y

    mo_ref[...] = jnp.zeros(mo_ref.shape, F32)
    seqs = range(MLSTM_SAMPLE_BATCH)
    _mlstm_chunk([p_ref.at[b] for b in seqs], [(gc_ref[b], gr_ref[b]) for b in seqs],
                 bc_ref[...], br_ref[...], gh_ref, DEC_SEQ, get_state, put_state, put_y)


def _mlstm_sample_io(p3, gate3, gate_t3, bias_col, bias_row, g_head, c_all, n_in, m_in, j, first, block):
    bs = MLSTM_SAMPLE_BATCH
    const = lambda shape: pl.BlockSpec(shape, lambda *ids: (0,) * len(shape))
    tok = lambda n: pl.BlockSpec((bs, DEC_SEQ, n), lambda *ids: (block(*ids), 0, 0))
    state_block = (bs, N_HEADS, HEAD_DIM, HEAD_DIM)
    c_spec = pl.BlockSpec((None,) + state_block, lambda *ids: (j, block(*ids), 0, 0, 0))
    c_out = (pl.BlockSpec((c_all.shape[0],) + state_block, lambda *ids: (0, block(*ids), 0, 0, 0))
             if first else c_spec)
    n_spec = pl.BlockSpec((bs, N_HEADS, HEAD_DIM), lambda *ids: (block(*ids), 0, 0))
    m_spec = pl.BlockSpec((bs, GATE_ROWS, LANES), lambda *ids: (block(*ids), 0, 0))
    in_specs = [tok(4 * D_MODEL), tok(LANES),
                pl.BlockSpec((bs, GATE_ROWS, DEC_SEQ), lambda *ids: (block(*ids), 0, 0)),
                const((1, LANES)), const((GATE_ROWS, 1)), const((1, D_MODEL)),
                c_spec, n_spec, m_spec]
    args = [p3, gate3, gate_t3, bias_col, bias_row, g_head, c_all, n_in, m_in]
    out_shapes = [jax.ShapeDtypeStruct((DEC_BATCH, DEC_SEQ, D_MODEL), F32),
                  jax.ShapeDtypeStruct(c_all.shape, F32),
                  jax.ShapeDtypeStruct((DEC_BATCH, N_HEADS, HEAD_DIM), F32),
                  jax.ShapeDtypeStruct((DEC_BATCH, GATE_ROWS, LANES), F32)]
    return in_specs, args, [tok(D_MODEL), c_out, n_spec, m_spec], out_shapes


def _norm_matmul_kernel(x_ref, g_ref, w_ref, o_ref):
    o_ref[...] = _dot(_rms(x_ref[...], g_ref[...]).astype(BF16), w_ref[...])


def _norm_matmul_sample(x, g, w_all, l):
    first = N_PROMPT // ROW_TILE
    return pl.pallas_call(
        _norm_matmul_kernel,
        grid=(N_SAMPLE // ROW_TILE,),
        in_specs=[pl.BlockSpec((ROW_TILE, D_MODEL), lambda r: (first + r, 0)),
                  pl.BlockSpec((1, D_MODEL), lambda r: (0, 0)),
                  _resident((None, D_MODEL, D_MODEL), lambda r: (l, 0, 0))],
        out_specs=pl.BlockSpec((ROW_TILE, D_MODEL), lambda r: (r, 0)),
        out_shape=jax.ShapeDtypeStruct((N_SAMPLE, D_MODEL), F32),
        compiler_params=_params("parallel"),
        name="norm_matmul_sample",
    )(x, g, w_all)


def _matmul_residual_kernel(x_ref, y_ref, w_ref, o_ref):
    o_ref[...] = x_ref[...] + _dot(y_ref[...].astype(BF16), w_ref[...])


def _matmul_residual_sample(x, y, w_all, l):
    first = N_PROMPT // ROW_TILE
    xrow = pl.BlockSpec((ROW_TILE, D_MODEL), lambda r: (first + r, 0))
    return pl.pallas_call(
        _matmul_residual_kernel,
        grid=(N_SAMPLE // ROW_TILE,),
        in_specs=[xrow, pl.BlockSpec((ROW_TILE, D_MODEL), lambda r: (r, 0)),
                  _resident((None, D_MODEL, D_MODEL), lambda r: (l, 0, 0))],
        out_specs=xrow,
        out_shape=jax.ShapeDtypeStruct((N_ROWS, D_MODEL), F32),
        input_output_aliases={0: 0},
        compiler_params=_params("parallel"),
        name="matmul_residual_sample",
    )(x, y, w_all)


MEM_KV_BATCH = 2


def _mem_kv_kernel(m_ref, w_ref, k_ref, v_ref, kb_ref, vb_ref):
    kv = _dot(m_ref[...].astype(BF16), w_ref[...])
    kb_ref[...] = kv[:, :D_MODEL].astype(BF16)
    vb_ref[...] = kv[:, D_MODEL:].astype(BF16)
    for b in range(MEM_KV_BATCH):
        rows = slice(b * N_MEM, (b + 1) * N_MEM)
        for h in range(N_XHEADS):
            lo, hi = h * XHEAD_DIM, (h + 1) * XHEAD_DIM
            k_ref[b, :, h, :] = kv[rows, lo:hi]
            v_ref[b, :, h, :] = kv[rows, D_MODEL + lo:D_MODEL + hi]


def _mem_kv(mem, wkv_all):
    bs = MEM_KV_BATCH
    out = pl.BlockSpec((None, bs, N_MEM, N_XHEADS, XHEAD_DIM), lambda l, r: (l, r, 0, 0, 0))
    shape = jax.ShapeDtypeStruct((DEPTH, BATCH, N_MEM, N_XHEADS, XHEAD_DIM), F32)
    out_b = pl.BlockSpec((None, bs * N_MEM, D_MODEL), lambda l, r: (l, r, 0))
    shape_b = jax.ShapeDtypeStruct((DEPTH, BATCH * N_MEM, D_MODEL), BF16)
    return pl.pallas_call(
        _mem_kv_kernel,
        grid=(DEPTH, BATCH // bs),
        in_specs=[pl.BlockSpec((bs * N_MEM, D_MODEL), lambda l, r: (r, 0)),
                  pl.BlockSpec((None, D_MODEL, 2 * D_MODEL), lambda l, r: (l, 0, 0))],
        out_specs=[out, out, out_b, out_b],
        out_shape=[shape, shape, shape_b, shape_b],
        compiler_params=_params("arbitrary", "arbitrary"),
        name="mem_kv",
    )(mem, wkv_all)


def _softmax_rows(s):
    e = jnp.exp(s - jnp.max(s, axis=1, keepdims=True))
    return e * (1.0 / jnp.sum(e, axis=1, keepdims=True))


N_MLSTM_SAMPLE_IN = 9


def _xattn_prompt_kernel(x_ref, g_ref, wq_ref, k_ref, v_ref, wo_ref, *rest, ride):
    a_sc = rest[-1]
    if ride is None:
        o_ref = rest[0]
    else:
        o_ref, y_ref, co_ref, no_ref, mo_ref = rest[-6:-1]
        _mlstm_sample_step(*rest[:N_MLSTM_SAMPLE_IN], y_ref, co_ref, no_ref, mo_ref, *ride)
    x = x_ref[...]
    q = _dot(_rms(x, g_ref[...]).astype(BF16), wq_ref[...])
    cols = [slice(h * XHEAD_DIM, (h + 1) * XHEAD_DIM) for h in range(N_XHEADS)]
    scores = [lax.dot_general(q[:, c].astype(BF16), k_ref[:, c], _NT,
                              preferred_element_type=F32) * (XHEAD_DIM ** -0.5) for c in cols]
    probs = [_softmax_rows(s).astype(BF16) for s in scores]
    for c, p in zip(cols, probs):
        a_sc[:, c] = _dot(p, v_ref[:, c]).astype(BF16)
    o_ref[...] = x + _dot(a_sc[...], wo_ref[...])


def _xattn_prompt(x, g, wq_all, k_all, v_all, wo_all, l, mlstm=None):
    tile = SEQ_TILE if mlstm is None else ROW_TILE
    nt = SEQ // tile
    xrow = pl.BlockSpec((tile, D_MODEL), lambda b, t: (b * nt + t, 0))
    kv = pl.BlockSpec((None, N_MEM, D_MODEL), lambda b, t: (l, b, 0))
    w = _resident((None, D_MODEL, D_MODEL), lambda b, t: (l, 0, 0))
    in_specs = [xrow, pl.BlockSpec((1, D_MODEL), lambda b, t: (0, 0)), w, kv, kv, w]
    args = [x, g, wq_all, k_all, v_all, wo_all]
    out_specs = [xrow]
    out_shape = [jax.ShapeDtypeStruct((N_ROWS, D_MODEL), F32)]
    aliases = {0: 0}
    ride = None
    if mlstm is not None:
        *step_args, j, c_prev = mlstm
        assert BATCH * nt * MLSTM_SAMPLE_BATCH == DEC_BATCH
        ride = (j, c_prev is None)
        s_in, s_args, s_out, s_shapes = _mlstm_sample_io(*step_args, *ride, lambda b, t: b * nt + t)
        assert len(s_in) == N_MLSTM_SAMPLE_IN
        in_specs += s_in
        args += s_args
        if c_prev is not None:
            in_specs.append(pl.BlockSpec(memory_space=pl.ANY))
            args.append(c_prev)
            aliases[len(args) - 1] = 2
        out_specs += s_out
        out_shape += s_shapes
    results = pl.pallas_call(
        functools.partial(_xattn_prompt_kernel, ride=ride),
        grid=(BATCH, nt),
        in_specs=in_specs,
        out_specs=out_specs,
        out_shape=out_shape,
        scratch_shapes=[pltpu.VMEM((tile, D_MODEL), BF16)],
        input_output_aliases=aliases,
        compiler_params=_params("arbitrary", "arbitrary"),
        name="xattn_prompt" + ("_mlstm" if ride else ""),
    )(*args)
    return results[0] if mlstm is None else results


def _sample_attention(qs, ks, vs):
    n_q, n_kv = N_XHEADS * DEC_SEQ, N_MEM * N_XHEADS
    q_head = lax.broadcasted_iota(jnp.int32, (n_q, n_kv), 0) // DEC_SEQ
    kv_head = lax.broadcasted_iota(jnp.int32, (n_q, n_kv), 1) % N_XHEADS
    same_head = q_head == kv_head
    scores = [lax.dot_general(q.astype(BF16), k.reshape(n_kv, XHEAD_DIM).astype(BF16), _NT,
                              preferred_element_type=F32) * (XHEAD_DIM ** -0.5)
              for q, k in zip(qs, ks)]
    probs = [_softmax_rows(jnp.where(same_head, s, -jnp.inf)).astype(BF16) for s in scores]
    return [_dot(p, v.reshape(n_kv, XHEAD_DIM).astype(BF16)) for p, v in zip(probs, vs)]


def _heads_major(a):
    a = a.reshape(DEC_BATCH, DEC_SEQ, N_XHEADS, XHEAD_DIM)
    return a.transpose(0, 2, 1, 3).reshape(DEC_BATCH, N_XHEADS * DEC_SEQ, XHEAD_DIM)


def _pool_prompt_kernel(x_ref, g_ref, win_ref, wgrp_ref, scale_ref, wout_ref,
                        o_ref, buf_ref, ext_sc, z_sc):
    tile = pl.program_id(1)

    @pl.when(tile == 0)
    def _():
        ext_sc[0:POOL_HALO, :] = jnp.zeros((POOL_HALO, D_MODEL), F32)

    x = x_ref[...]
    u = _dot(_rms(x, g_ref[...]).astype(BF16), win_ref[...])
    ext_sc[POOL_HALO:, :] = u
    pos = tile * SEQ_TILE + lax.broadcasted_iota(jnp.int32, (SEQ_TILE, 1), 0)
    for g, w in enumerate(POOL_WINDOWS):
        lo, hi = g * POOL_GROUP_DIM, (g + 1) * POOL_GROUP_DIM
        assert w & (w - 1) == 0 and w <= POOL_HALO
        win = ext_sc[:, lo:hi]
        span = 1
        while span < w:
            win = win + pltpu.roll(win, span, 0)
            span *= 2
        win = win[POOL_HALO:, :]
        cnt = jnp.minimum(pos + 1, w).astype(F32)
        d = win * (1.0 / cnt) - u[:, lo:hi]
        z_sc[:, lo:hi] = (_dot(d.astype(BF16), wgrp_ref[g]) * scale_ref[:, lo:hi]).astype(BF16)
    o_ref[...] = x + _dot(z_sc[...], wout_ref[...])
    tail = ext_sc[SEQ_TILE:SEQ_TILE + POOL_HALO, :]
    ext_sc[0:POOL_HALO, :] = tail

    @pl.when(tile == pl.num_programs(1) - 1)
    def _():
        buf_ref[...] = tail


def _pool_prompt(x, g, win_all, wgrp_all, scale, wout_all, j):
    nt = SEQ // SEQ_TILE
    xrow = pl.BlockSpec((SEQ_TILE, D_MODEL), lambda b, t: (b * nt + t, 0))
    vec = pl.BlockSpec((1, D_MODEL), lambda b, t: (0, 0))
    w = _resident((None, D_MODEL, D_MODEL), lambda b, t: (j, 0, 0))
    wgrp = _resident((None, len(POOL_WINDOWS), POOL_GROUP_DIM, POOL_GROUP_DIM),
                     lambda b, t: (j, 0, 0, 0))
    return pl.pallas_call(
        _pool_prompt_kernel,
        grid=(BATCH, nt),
        in_specs=[xrow, vec, w, wgrp, vec, w],
        out_specs=[xrow, pl.BlockSpec((None, POOL_HALO, D_MODEL), lambda b, t: (b, 0, 0))],
        out_shape=[jax.ShapeDtypeStruct((N_ROWS, D_MODEL), F32),
                   jax.ShapeDtypeStruct((BATCH, POOL_HALO, D_MODEL), F32)],
        scratch_shapes=[pltpu.VMEM((POOL_HALO + SEQ_TILE, D_MODEL), F32),
                        pltpu.VMEM((SEQ_TILE, D_MODEL), BF16)],
        input_output_aliases={0: 0},
        compiler_params=_params("arbitrary", "arbitrary"),
        name="pool_prompt",
    )(x, g, win_all, wgrp_all, scale, wout_all)


def _pool_sample_kernel(x_ref, buf_ref, g_ref, win_ref, wgrp_ref, scale_ref, *rest, layer, first):
    z_ref, nbuf_ref = rest[-2:]
    nbuf_ref = _own_layer(nbuf_ref, layer, first)
    u = _dot(_rms(x_ref[...], g_ref[...]).astype(BF16), win_ref[...])

    def ext(row, lo, hi):
        if row < POOL_BUF:
            return buf_ref[:, row, lo:hi]
        t = row - POOL_BUF
        return u[t * DEC_BATCH:(t + 1) * DEC_BATCH, lo:hi]

    for g, w in enumerate(POOL_WINDOWS):
        lo, hi = g * POOL_GROUP_DIM, (g + 1) * POOL_GROUP_DIM
        ds = []
        for t in range(DEC_SEQ):
            win = ext(POOL_BUF + t, lo, hi)
            for back in range(1, w):
                win = win + ext(POOL_BUF + t - back, lo, hi)
            cnt = float(min(PAST_LEN + 1 + t, w))
            ds.append(win / cnt - ext(POOL_BUF + t, lo, hi))
        d = jnp.concatenate(ds, axis=0).astype(BF16)
        z_ref[:, lo:hi] = _dot(d, wgrp_ref[g]) * scale_ref[:, lo:hi]
    for row in range(POOL_BUF):
        nbuf_ref[:, row, :] = ext(row + DEC_SEQ, 0, D_MODEL)


def _pool_sample(xs_t, buf_all, g, win_all, wgrp_all, scale, j, buf_prev):
    first = buf_prev is None
    buf_block = (DEC_BATCH, POOL_BUF, D_MODEL)
    once = lambda shape, imap: pl.BlockSpec(shape, imap, pipeline_mode=pl.Buffered(1))
    if first:
        nbuf = once((buf_all.shape[0],) + buf_block, lambda i: (0, 0, 0, 0))
    else:
        nbuf = once((None,) + buf_block, lambda i: (j, 0, 0, 0))
    vec = pl.BlockSpec((1, D_MODEL), lambda i: (0, 0))
    buf = once((None, DEC_BATCH, POOL_BUF, D_MODEL), lambda i: (j, 0, 0, 0))
    in_specs = [pl.BlockSpec((N_SAMPLE, D_MODEL), lambda i: (0, 0)), buf, vec,
                once((None, D_MODEL, D_MODEL), lambda i: (j, 0, 0)),
                once((None, len(POOL_WINDOWS), POOL_GROUP_DIM, POOL_GROUP_DIM), lambda i: (j, 0, 0, 0)),
                vec]
    args = [xs_t, buf_all, g, win_all, wgrp_all, scale]
    aliases = {}
    if buf_prev is not None:
        in_specs.append(pl.BlockSpec(memory_space=pl.ANY))
        args.append(buf_prev)
        aliases[len(args) - 1] = 1
    return pl.pallas_call(
        functools.partial(_pool_sample_kernel, layer=j, first=first),
        grid=(1,),
        in_specs=in_specs,
        out_specs=[pl.BlockSpec((N_SAMPLE, D_MODEL), lambda i: (0, 0)), nbuf],
        out_shape=[jax.ShapeDtypeStruct((N_SAMPLE, D_MODEL), F32),
                   jax.ShapeDtypeStruct(buf_all.shape, F32)],
        input_output_aliases=aliases,
        compiler_params=_params("arbitrary"),
        name="pool_sample",
    )(*args)


def _swap_token_seq(a, lead):
    n = a.shape[0] // lead
    return a.reshape(lead, n, a.shape[1]).transpose(1, 0, 2).reshape(a.shape)


def kernel(x_prompt, x_sample, mem_prompt, cache_mem_k, cache_mem_v, state_mlstm_C, state_mlstm_n,
           state_mlstm_m, state_pool_buf, norm_g, final_g, ffn_w_up, ffn_w_down, mlstm_w_in,
           mlstm_b_i, mlstm_b_f, mlstm_g_head, mlstm_w_out, pool_w_in, pool_w_grp, pool_scale,
           pool_w_out, xattn_w_q, xattn_w_kv, xattn_w_o):
    n_mlstm = mlstm_w_in.shape[0]
    ffn_w = {(0, 0): (ffn_w_up[0, 0].astype(BF16), ffn_w_down[0, 0].astype(BF16))}
    others = (mlstm_w_in, mlstm_w_out, pool_w_in, pool_w_grp, pool_w_out, xattn_w_q, xattn_w_kv,
              xattn_w_o)
    w_gate = jnp.pad(mlstm_w_in[:, :, 4 * D_MODEL:],
                     ((0, 0), (0, 0), (0, LANES - 2 * N_HEADS))).astype(BF16)
    gate_bias = jnp.concatenate([mlstm_b_i, mlstm_b_f], axis=1).astype(F32)

    x = [x_prompt.reshape(N_PROMPT, D_MODEL), x_sample.reshape(N_SAMPLE, D_MODEL)]
    c_p = c_s = buf_s = None
    n_p, m_p, n_s, m_s, buf_p = [], [], [], [], []
    for l in range(DEPTH):
        g = norm_g[l].reshape(4, 1, D_MODEL)
        j = l // 2
        later = [(l, 1)] + ([(l + 1, 0)] if l + 1 < DEPTH else [])
        jobs = []
        for cl, ci in later:
            part = dict(part=cl * ffn_w_up.shape[1] + ci, parts=DEPTH * ffn_w_up.shape[1])
            jobs += [_cast_job(ffn_w_up, **part),
                     _cast_job(ffn_w_down, slabs=D_FF // LANES, **part)]
        if l == 0:
            jobs += [_cast_job(w) for w in others]
        x, cast = _ffn(x if l == 0 else [x], g[0], *ffn_w[l, 0], l, cast=jobs)
        ffn_w.update((key, tuple(cast[2 * k:2 * k + 2])) for k, key in enumerate(later))
        if l == 0:
            w_in, w_mout, p_win, p_wgrp, p_wout, wq, wkv, wo = (
                c.reshape(w.shape) for c, w in zip(cast[2 * len(later):], others))
            mem_k, mem_v, mem_kb, mem_vb = _mem_kv(mem_prompt.reshape(BATCH * N_MEM, D_MODEL), wkv)
        if l % 2 == 0:
            p_p, gate_p, p_s, gate_s = _mlstm_proj(x, g[1], w_in, w_gate, j)
            bias_col = jnp.pad(gate_bias[j], (0, LANES - GATE_ROWS)).reshape(1, LANES)
            bias_row = gate_bias[j].reshape(GATE_ROWS, 1)
            g_head = mlstm_g_head[j].reshape(1, D_MODEL)
            x, c_p, n1, m1 = _mlstm_prompt(x, p_p, gate_p, gate_p[:, :GATE_ROWS].T, bias_col, bias_row,
                                           g_head, w_mout, j, n_mlstm, c_p)
            gate_s = gate_s.reshape(DEC_BATCH, DEC_SEQ, LANES)
            m_in = jnp.broadcast_to(
                jnp.pad(state_mlstm_m[j], ((0, 0), (0, GATE_ROWS - N_HEADS)))[:, :, None],
                (DEC_BATCH, GATE_ROWS, LANES))
            x, y_s, c_s, n2, m2 = _xattn_prompt(
                x, g[2], wq, mem_kb, mem_vb, wo, l,
                mlstm=(p_s.reshape(DEC_BATCH, DEC_SEQ, 4 * D_MODEL), gate_s,
                       gate_s[:, :, :GATE_ROWS].transpose(0, 2, 1),
                       bias_col, bias_row, g_head, state_mlstm_C, state_mlstm_n[j], m_in, j, c_s))
            x = _matmul_residual_sample(x, y_s.reshape(N_SAMPLE, D_MODEL), w_mout, j)
            n_p.append(n1); m_p.append(m1[:, :N_HEADS, 0])
            n_s.append(n2); m_s.append(m2[:, :N_HEADS, 0])
        else:
            scale = pool_scale[j].reshape(1, D_MODEL)
            x, b1 = _pool_prompt(x, g[1], p_win, p_wgrp, scale, p_wout, j)
            xs_t = _swap_token_seq(x[N_PROMPT:], DEC_BATCH)
            z_t, buf_s = _pool_sample(xs_t, state_pool_buf, g[1], p_win, p_wgrp, scale, j, buf_s)
            x = _matmul_residual_sample(x, _swap_token_seq(z_t, DEC_SEQ), p_wout, j)
            buf_p.append(b1[:, POOL_HALO - POOL_BUF:])
            x = _xattn_prompt(x, g[2], wq, mem_kb, mem_vb, wo, l)
        q_s = _norm_matmul_sample(x, g[2], wq, l)
        last = l == DEPTH - 1
        out = _ffn([x], g[3], *ffn_w[l, 1], l, final_g=final_g.reshape(1, D_MODEL) if last else None,
                   attend=(_heads_major(q_s), cache_mem_k, cache_mem_v, wo))
        x = out if last else out[0]

    y_prompt, y_sample = x
    return (y_prompt.reshape(BATCH, SEQ, D_MODEL), y_sample.reshape(DEC_BATCH, DEC_SEQ, D_MODEL),
            mem_k, mem_v, c_p, jnp.stack(n_p), jnp.stack(m_p),
            c_s, jnp.stack(n_s), jnp.stack(m_s), jnp.stack(buf_p), buf_s)
```

```python
import functools

import jax
import jax.numpy as jnp
from jax import lax
from jax.experimental import pallas as pl
from jax.experimental.pallas import tpu as pltpu

F32 = jnp.float32
BF16 = jnp.bfloat16

D_MODEL = 1024
BATCH = 8
SEQ = 2048
DEPTH = 4
DEC_BATCH = 128
DEC_SEQ = 4
PAST_LEN = 16384
N_HEADS = 4
HEAD_DIM = D_MODEL // N_HEADS
CHUNK = 256
POOL_WINDOWS = (2, 4, 8, 16)
POOL_GROUP_DIM = D_MODEL // len(POOL_WINDOWS)
POOL_BUF = max(POOL_WINDOWS) - 1
POOL_HALO = 16
N_MEM = 256
N_XHEADS = 4
XHEAD_DIM = D_MODEL // N_XHEADS
D_FF = ((8 * D_MODEL // 3 + 127) // 128) * 128
EPS = 1e-6

N_PROMPT = BATCH * SEQ
N_SAMPLE = DEC_BATCH * DEC_SEQ
N_ROWS = N_PROMPT + N_SAMPLE

LANES = 128
GATE_ROWS = 8
ROW_TILE = 512
SEQ_TILE = 1024
PROMPT_TILES = N_PROMPT // ROW_TILE
assert N_SAMPLE == ROW_TILE
VMEM_LIMIT = 58 * 1024 * 1024

_NT = (((1,), (1,)), ((), ()))
_TN = (((0,), (0,)), ((), ()))


def _params(*sem):
    return pltpu.CompilerParams(dimension_semantics=sem, vmem_limit_bytes=VMEM_LIMIT)


def _resident(block_shape, index_map):
    return pl.BlockSpec(block_shape, index_map, pipeline_mode=pl.Buffered(1))


def _rms(x, g):
    return x * lax.rsqrt(jnp.mean(x * x, axis=-1, keepdims=True) + EPS) * g


def _log_sigmoid(x):
    return jnp.minimum(x, 0.0) - jnp.log1p(jnp.exp(-jnp.abs(x)))


def _dot(a, b):
    return jnp.dot(a, b, preferred_element_type=F32)


ATTEND_BATCH = DEC_BATCH // PROMPT_TILES
assert ATTEND_BATCH * PROMPT_TILES == DEC_BATCH


def _ffn_kernel(*refs, first, final, attend, n_cast):
    refs = list(refs)
    step = pl.program_id(0)
    is_prompt = step < PROMPT_TILES
    if first:
        xp_ref, xs_ref = refs[:2]
        del refs[:2]
        x = jnp.where(is_prompt, xp_ref[...], xs_ref[...])
    else:
        x = refs.pop(0)[...]
    g_ref, wup_ref, wdn_ref = refs[:3]
    del refs[:3]
    cast_in = refs[:n_cast]
    del refs[:n_cast]
    if attend:
        q_ref, k_ref, v_ref, wo_ref = refs[:4]
        del refs[:4]
        a_sc = refs.pop()

        def attended():
            y = x
            for h in range(N_XHEADS):
                a = a_sc[h].reshape(N_SAMPLE, XHEAD_DIM).astype(BF16)
                y = y + _dot(a, wo_ref[h * XHEAD_DIM:(h + 1) * XHEAD_DIM, :])
            return y

        x = lax.cond(is_prompt, lambda: x, attended)
        seqs = range(ATTEND_BATCH)
        first_seq = jnp.minimum(step, PROMPT_TILES - 1) * ATTEND_BATCH
        outs = _sample_attention([q_ref[b] for b in seqs], [k_ref[b] for b in seqs],
                                 [v_ref[b] for b in seqs])
        for b, o in enumerate(outs):
            for h in range(N_XHEADS):
                a_sc[h, first_seq + b] = o[h * DEC_SEQ:(h + 1) * DEC_SEQ, :]
    rest = refs
    halves = []
    for half in range(2):
        xh = x[half * (ROW_TILE // 2):(half + 1) * (ROW_TILE // 2)]
        xn = _rms(xh, g_ref[...]).astype(BF16)
        gate = _dot(xn, wup_ref[:, :D_FF])
        up = _dot(xn, wup_ref[:, D_FF:])
        act = (gate * jax.nn.sigmoid(gate) * up).astype(BF16)
        halves.append(xh + 0.5 * _dot(act, wdn_ref[...]))
    y = jnp.concatenate(halves, axis=0)
    if not final:
        o_ref, *cast_out = rest
        o_ref[...] = y
        for src, dst in zip(cast_in, cast_out):
            dst[...] = src[...].astype(BF16)
        return
    fg_ref, op_ref, os_ref = rest
    y = _rms(y, fg_ref[...])

    @pl.when(is_prompt)
    def _():
        op_ref[...] = y

    @pl.when(jnp.logical_not(is_prompt))
    def _():
        os_ref[...] = y


BF16_ROWS = 16


def _cast_job(w, part=0, parts=1, slabs=PROMPT_TILES):
    w = w.reshape(-1, w.shape[-1])
    rows = w.shape[0] // parts
    assert rows * parts == w.shape[0] and rows % slabs == 0
    return w, rows // slabs, part * slabs, slabs


def _ffn(xs, g, wup, wdn, l, final_g=None, attend=None, cast=()):
    first, final = len(xs) == 2, final_g is not None
    assert not (cast and (final or attend))
    row = pl.BlockSpec((ROW_TILE, D_MODEL), lambda r: (r, 0))
    prompt_row = pl.BlockSpec((ROW_TILE, D_MODEL), lambda r: (jnp.minimum(r, PROMPT_TILES - 1), 0))
    sample_row = pl.BlockSpec((ROW_TILE, D_MODEL), lambda r: (0, 0))
    vec = pl.BlockSpec((1, D_MODEL), lambda r: (0, 0))
    in_specs = ([prompt_row, sample_row] if first else [row]) + [
        vec, _resident(wup.shape, lambda r: (0, 0)), _resident(wdn.shape, lambda r: (0, 0))]
    args = list(xs) + [g, wup, wdn]
    cast_specs, cast_shapes = [], []
    for w, rows, first_slab, n_slabs in cast:
        assert rows % BF16_ROWS == 0 and n_slabs <= PROMPT_TILES
        slab = lambda r, n_slabs=n_slabs: jnp.minimum(r, n_slabs - 1)
        in_specs.append(pl.BlockSpec((rows, w.shape[1]),
                                     lambda r, slab=slab, first_slab=first_slab: (first_slab + slab(r), 0)))
        args.append(w)
        cast_specs.append(pl.BlockSpec((rows, w.shape[1]), lambda r, slab=slab: (slab(r), 0)))
        cast_shapes.append(jax.ShapeDtypeStruct((rows * n_slabs, w.shape[1]), BF16))
    scratch = []
    if attend is not None:
        bs = ATTEND_BATCH
        tile = lambda r: jnp.minimum(r, PROMPT_TILES - 1)
        cache = pl.BlockSpec((None, bs, N_MEM, N_XHEADS, XHEAD_DIM), lambda r: (l, tile(r), 0, 0, 0))
        in_specs += [pl.BlockSpec((bs, N_XHEADS * DEC_SEQ, XHEAD_DIM), lambda r: (tile(r), 0, 0)),
                     cache, cache, _resident((None, D_MODEL, D_MODEL), lambda r: (l, 0, 0))]
        args += list(attend)
        scratch = [pltpu.VMEM((N_XHEADS, DEC_BATCH, DEC_SEQ, XHEAD_DIM), F32)]
    if final:
        in_specs.append(vec)
        args.append(final_g)
        out_specs = [prompt_row, sample_row]
        out_shape = [jax.ShapeDtypeStruct((N_PROMPT, D_MODEL), F32),
                     jax.ShapeDtypeStruct((N_SAMPLE, D_MODEL), F32)]
    else:
        out_specs = [row] + cast_specs
        out_shape = [jax.ShapeDtypeStruct((N_ROWS, D_MODEL), F32)] + cast_shapes
    results = pl.pallas_call(
        functools.partial(_ffn_kernel, first=first, final=final, attend=attend is not None,
                          n_cast=len(cast)),
        grid=(N_ROWS // ROW_TILE,),
        in_specs=in_specs,
        out_specs=out_specs,
        out_shape=out_shape,
        scratch_shapes=scratch,
        compiler_params=_params("arbitrary"),
        name=("ffn_final" if final else "ffn_first" if first else "ffn")
        + ("_attend" if scratch else "") + ("_cast" if cast else ""),
    )(*args)
    if final:
        return results
    return results[0], list(results[1:])


def _prompt_rows(n):
    return pl.BlockSpec((ROW_TILE, n), lambda r: (jnp.minimum(r, PROMPT_TILES - 1), 0))


def _sample_rows(n):
    return pl.BlockSpec((ROW_TILE, n), lambda r: (0, 0))


def _mlstm_proj_kernel(x_ref, g_ref, w_ref, wg_ref, pp_ref, gp_ref, ps_ref, gs_ref):
    xn = _rms(x_ref[...], g_ref[...]).astype(BF16)
    is_prompt = pl.program_id(0) < PROMPT_TILES

    @pl.when(is_prompt)
    def _():
        pp_ref[...] = _dot(xn, w_ref[...])
        gp_ref[...] = _dot(xn, wg_ref[...])

    @pl.when(jnp.logical_not(is_prompt))
    def _():
        ps_ref[...] = _dot(xn, w_ref[...])
        gs_ref[...] = _dot(xn, wg_ref[...])


def _mlstm_proj(x, g, w_all, wg_all, j):
    return pl.pallas_call(
        _mlstm_proj_kernel,
        grid=(N_ROWS // ROW_TILE,),
        in_specs=[pl.BlockSpec((ROW_TILE, D_MODEL), lambda r: (r, 0)),
                  pl.BlockSpec((1, D_MODEL), lambda r: (0, 0)),
                  _resident((None, D_MODEL, 4 * D_MODEL), lambda r: (j, 0, 0)),
                  _resident((None, D_MODEL, LANES), lambda r: (j, 0, 0))],
        out_specs=[_prompt_rows(4 * D_MODEL), _prompt_rows(LANES),
                   _sample_rows(4 * D_MODEL), _sample_rows(LANES)],
        out_shape=[jax.ShapeDtypeStruct((N_PROMPT, 4 * D_MODEL), F32),
                   jax.ShapeDtypeStruct((N_PROMPT, LANES), F32),
                   jax.ShapeDtypeStruct((N_SAMPLE, 4 * D_MODEL), F32),
                   jax.ShapeDtypeStruct((N_SAMPLE, LANES), F32)],
        compiler_params=_params("arbitrary"),
        name="mlstm_proj",
    )(x, g, w_all, wg_all)


def _gate_terms(gate_col, gate_row, bias_col, bias_row, length):
    r = lax.broadcasted_iota(jnp.int32, (length, length), 0)
    c = lax.broadcasted_iota(jnp.int32, (length, length), 1)
    mask = r >= c
    lower = mask.astype(F32)
    upper = (r <= c).astype(F32)
    zc = gate_col + bias_col
    lane = lax.broadcasted_iota(jnp.int32, zc.shape, 1)
    log_col = jnp.where(lane < N_HEADS, zc, _log_sigmoid(zc))
    cum_col = jnp.dot(lower, log_col, precision=lax.Precision.HIGHEST,
                      preferred_element_type=F32)
    zr = gate_row + bias_row
    sub = lax.broadcasted_iota(jnp.int32, zr.shape, 0)
    log_row = jnp.where(sub < N_HEADS, zr, _log_sigmoid(zr))
    cum_row = jnp.dot(log_row, upper, precision=lax.Precision.HIGHEST,
                      preferred_element_type=F32)
    return mask, log_col, cum_col, log_row, cum_row


def _mlstm_chunk(p_views, gates, bias_col, bias_row, g_head_ref, length, get_state, put_state, put_y):
    pairs = [(s, h) for s in range(len(p_views)) for h in range(N_HEADS)]
    terms = [_gate_terms(gc, gr, bias_col, bias_row, length) for gc, gr in gates]
    mask = terms[0][0]

    def cols(h):
        return slice(h * HEAD_DIM, (h + 1) * HEAD_DIM)

    def part(s, h, which):
        lo = which * D_MODEL + h * HEAD_DIM
        return p_views[s][:, lo:lo + HEAD_DIM]

    b_col, b_row, i_col, i_row = {}, {}, {}, {}
    for s, h in pairs:
        _, log_col, cum_col, log_row, cum_row = terms[s]
        f = N_HEADS + h
        b_col[s, h], b_row[s, h] = cum_col[:, f:f + 1], cum_row[f:f + 1, :]
        i_col[s, h], i_row[s, h] = log_col[:, h:h + 1], log_row[h:h + 1, :]

    a, dlog, mt, dw, inter = {}, {}, {}, {}, {}
    for sh in pairs:
        a[sh] = b_col[sh] + get_state(*sh, "m")
        dlog[sh] = jnp.where(mask, b_col[sh] - b_row[sh] + i_row[sh], -jnp.inf)
    for sh in pairs:
        mt[sh] = jnp.maximum(a[sh], jnp.max(dlog[sh], axis=1, keepdims=True))
    for sh in pairs:
        dw[sh] = jnp.exp(dlog[sh] - mt[sh])
        inter[sh] = jnp.exp(a[sh] - mt[sh])

    sc, q_c, q_n, vb, k32 = {}, {}, {}, {}, {}
    for s, h in pairs:
        q = part(s, h, 0)
        k32[s, h] = part(s, h, 1) * (HEAD_DIM ** -0.5)
        qb = q.astype(BF16)
        sc[s, h] = lax.dot_general(qb, k32[s, h].astype(BF16), _NT,
                                   preferred_element_type=F32) * dw[s, h]
        q_c[s, h] = _dot(qb, get_state(s, h, "c").astype(BF16))
        q_n[s, h] = jnp.sum(q * get_state(s, h, "n"), axis=1, keepdims=True)
    hid = {}
    for s, h in pairs:
        vb[s, h] = part(s, h, 2).astype(BF16)
        num = inter[s, h] * q_c[s, h] + _dot(sc[s, h].astype(BF16), vb[s, h])
        den = inter[s, h] * q_n[s, h] + jnp.sum(sc[s, h], axis=1, keepdims=True)
        hid[s, h] = num * (1.0 / jnp.maximum(jnp.abs(den), jnp.exp(-mt[s, h])))
    for s, h in pairs:
        hh = hid[s, h]
        hn = hh * lax.rsqrt(jnp.mean(hh * hh, axis=1, keepdims=True) + EPS) * g_head_ref[:, cols(h)]
        put_y(s, h, jax.nn.sigmoid(part(s, h, 3)) * hn)
    for s, h in pairs:
        c_state, n_state, m_state = (get_state(s, h, which) for which in "cnm")
        m_new = mt[s, h][length - 1:length, :]
        b_last = b_col[s, h][length - 1:length, :]
        decay = jnp.exp(b_last + m_state - m_new)
        kw = k32[s, h] * jnp.exp(b_last - b_col[s, h] + i_col[s, h] - m_new)
        c_new = decay * c_state + lax.dot_general(kw.astype(BF16), vb[s, h], _TN,
                                                  preferred_element_type=F32)
        n_new = decay * n_state + jnp.sum(kw, axis=0, keepdims=True)
        put_state(s, h, c_new, n_new, m_new)


def _own_layer(ref, layer, first):
    if not first:
        return ref
    for other in range(ref.shape[0]):
        if other != layer:
            ref[other] = jnp.zeros(ref.shape[1:], ref.dtype)
    return ref.at[layer]


def _mlstm_prompt_kernel(p_ref, gc_ref, gr_ref, x_ref, bc_ref, br_ref, gh_ref, wo_ref, *rest,
                         layer, first):
    xo_ref, co_ref, no_ref, mo_ref, c_sc, n_sc, m_sc, y_sc = rest[-8:]
    chunk = pl.program_id(1)

    @pl.when(chunk == 0)
    def _():
        c_sc[...] = jnp.zeros_like(c_sc)
        n_sc[...] = jnp.zeros_like(n_sc)
        m_sc[...] = jnp.zeros_like(m_sc)

    def get_state(_, h, which):
        return {"c": lambda: c_sc[h], "n": lambda: n_sc[h:h + 1, :],
                "m": lambda: m_sc[h:h + 1, 0:1]}[which]()

    def put_state(_, h, c_new, n_new, m_new):
        c_sc[h] = c_new
        n_sc[h:h + 1, :] = n_new
        m_sc[h:h + 1, :] = jnp.broadcast_to(m_new, (1, LANES))

    def put_y(_, h, y):
        y_sc[:, h * HEAD_DIM:(h + 1) * HEAD_DIM] = y.astype(BF16)

    _mlstm_chunk([p_ref], [(gc_ref[...], gr_ref[...])], bc_ref[...], br_ref[...], gh_ref, CHUNK,
                 get_state, put_state, put_y)
    xo_ref[...] = x_ref[...] + _dot(y_sc[...], wo_ref[...])

    @pl.when(chunk == pl.num_programs(1) - 1)
    def _():
        _own_layer(co_ref, layer, first)[...] = c_sc[...]
        no_ref[...] = n_sc[0:N_HEADS, :]
        mo_ref[...] = m_sc[...]


def _mlstm_prompt(x, p, gate, gate_t, bias_col, bias_row, g_head, wo_all, j, n_layers, c_prev):
    nc = SEQ // CHUNK
    first = c_prev is None
    state_block = (None, N_HEADS, HEAD_DIM, HEAD_DIM)
    if first:
        c_spec = pl.BlockSpec((n_layers,) + state_block, lambda b, c: (0, b, 0, 0, 0))
    else:
        c_spec = pl.BlockSpec((None,) + state_block, lambda b, c: (j, b, 0, 0, 0))
    tok = lambda n: pl.BlockSpec((CHUNK, n), lambda b, c: (b * nc + c, 0))
    const = lambda shape: pl.BlockSpec(shape, lambda b, c: (0,) * len(shape))
    in_specs = [tok(4 * D_MODEL), tok(LANES),
                pl.BlockSpec((GATE_ROWS, CHUNK), lambda b, c: (0, b * nc + c)),
                tok(D_MODEL), const((1, LANES)), const((GATE_ROWS, 1)), const((1, D_MODEL)),
                _resident((None, D_MODEL, D_MODEL), lambda b, c: (j, 0, 0))]
    args = [p, gate, gate_t, x, bias_col, bias_row, g_head, wo_all]
    aliases = {3: 0}
    if c_prev is not None:
        in_specs.append(pl.BlockSpec(memory_space=pl.ANY))
        args.append(c_prev)
        aliases[len(args) - 1] = 1
    return pl.pallas_call(
        functools.partial(_mlstm_prompt_kernel, layer=j, first=first),
        grid=(BATCH, nc),
        in_specs=in_specs,
        out_specs=[tok(D_MODEL), c_spec,
                   pl.BlockSpec((None, N_HEADS, HEAD_DIM), lambda b, c: (b, 0, 0)),
                   pl.BlockSpec((None, GATE_ROWS, LANES), lambda b, c: (b, 0, 0))],
        out_shape=[jax.ShapeDtypeStruct((N_ROWS, D_MODEL), F32),
                   jax.ShapeDtypeStruct((n_layers, BATCH, N_HEADS, HEAD_DIM, HEAD_DIM), F32),
                   jax.ShapeDtypeStruct((BATCH, N_HEADS, HEAD_DIM), F32),
                   jax.ShapeDtypeStruct((BATCH, GATE_ROWS, LANES), F32)],
        scratch_shapes=[pltpu.VMEM((N_HEADS, HEAD_DIM, HEAD_DIM), F32),
                        pltpu.VMEM((GATE_ROWS, HEAD_DIM), F32),
                        pltpu.VMEM((GATE_ROWS, LANES), F32),
                        pltpu.VMEM((CHUNK, D_MODEL), BF16)],
        input_output_aliases=aliases,
        compiler_params=_params("arbitrary", "arbitrary"),
        name="mlstm_prompt",
    )(*args)


MLSTM_SAMPLE_BATCH = 4


def _mlstm_sample_step(p_ref, gc_ref, gr_ref, bc_ref, br_ref, gh_ref, ci_ref, ni_ref, mi_ref,
                       y_ref, co_ref, no_ref, mo_ref, layer, first):
    co_ref = _own_layer(co_ref, layer, first)

    def get_state(b, h, which):
        return {"c": lambda: ci_ref[b, h], "n": lambda: ni_ref[b, h:h + 1, :],
                "m": lambda: mi_ref[b, h:h + 1, 0:1]}[which]()

    def put_state(b, h, c_new, n_new, m_new):
        co_ref[b, h] = c_new
        no_ref[b, h:h + 1, :] = n_new
        mo_ref[b, h:h + 1, :] = jnp.broadcast_to(m_new, (1, LANES))

    def put_y(b, h, y):
        y_ref[b, :, h * HEAD_DIM:(h + 1) * HEAD_DIM] = y

    mo_ref[...] = jnp.zeros(mo_ref.shape, F32)
    seqs = range(MLSTM_SAMPLE_BATCH)
    _mlstm_chunk([p_ref.at[b] for b in seqs], [(gc_ref[b], gr_ref[b]) for b in seqs],
                 bc_ref[...], br_ref[...], gh_ref, DEC_SEQ, get_state, put_state, put_y)


def _mlstm_sample_io(p3, gate3, gate_t3, bias_col, bias_row, g_head, c_all, n_in, m_in, j, first, block):
    bs = MLSTM_SAMPLE_BATCH
    const = lambda shape: pl.BlockSpec(shape, lambda *ids: (0,) * len(shape))
    tok = lambda n: pl.BlockSpec((bs, DEC_SEQ, n), lambda *ids: (block(*ids), 0, 0))
    state_block = (bs, N_HEADS, HEAD_DIM, HEAD_DIM)
    c_spec = pl.BlockSpec((None,) + state_block, lambda *ids: (j, block(*ids), 0, 0, 0))
    c_out = (pl.BlockSpec((c_all.shape[0],) + state_block, lambda *ids: (0, block(*ids), 0, 0, 0))
             if first else c_spec)
    n_spec = pl.BlockSpec((bs, N_HEADS, HEAD_DIM), lambda *ids: (block(*ids), 0, 0))
    m_spec = pl.BlockSpec((bs, GATE_ROWS, LANES), lambda *ids: (block(*ids), 0, 0))
    in_specs = [tok(4 * D_MODEL), tok(LANES),
                pl.BlockSpec((bs, GATE_ROWS, DEC_SEQ), lambda *ids: (block(*ids), 0, 0)),
                const((1, LANES)), const((GATE_ROWS, 1)), const((1, D_MODEL)),
                c_spec, n_spec, m_spec]
    args = [p3, gate3, gate_t3, bias_col, bias_row, g_head, c_all, n_in, m_in]
    out_shapes = [jax.ShapeDtypeStruct((DEC_BATCH, DEC_SEQ, D_MODEL), F32),
                  jax.ShapeDtypeStruct(c_all.shape, F32),
                  jax.ShapeDtypeStruct((DEC_BATCH, N_HEADS, HEAD_DIM), F32),
                  jax.ShapeDtypeStruct((DEC_BATCH, GATE_ROWS, LANES), F32)]
    return in_specs, args, [tok(D_MODEL), c_out, n_spec, m_spec], out_shapes


def _norm_matmul_kernel(x_ref, g_ref, w_ref, o_ref):
    o_ref[...] = _dot(_rms(x_ref[...], g_ref[...]).astype(BF16), w_ref[...])


def _norm_matmul_sample(x, g, w_all, l):
    first = N_PROMPT // ROW_TILE
    return pl.pallas_call(
        _norm_matmul_kernel,
        grid=(N_SAMPLE // ROW_TILE,),
        in_specs=[pl.BlockSpec((ROW_TILE, D_MODEL), lambda r: (first + r, 0)),
                  pl.BlockSpec((1, D_MODEL), lambda r: (0, 0)),
                  _resident((None, D_MODEL, D_MODEL), lambda r: (l, 0, 0))],
        out_specs=pl.BlockSpec((ROW_TILE, D_MODEL), lambda r: (r, 0)),
        out_shape=jax.ShapeDtypeStruct((N_SAMPLE, D_MODEL), F32),
        compiler_params=_params("parallel"),
        name="norm_matmul_sample",
    )(x, g, w_all)


def _matmul_residual_kernel(x_ref, y_ref, w_ref, o_ref):
    o_ref[...] = x_ref[...] + _dot(y_ref[...].astype(BF16), w_ref[...])


def _matmul_residual_sample(x, y, w_all, l):
    first = N_PROMPT // ROW_TILE
    xrow = pl.BlockSpec((ROW_TILE, D_MODEL), lambda r: (first + r, 0))
    return pl.pallas_call(
        _matmul_residual_kernel,
        grid=(N_SAMPLE // ROW_TILE,),
        in_specs=[xrow, pl.BlockSpec((ROW_TILE, D_MODEL), lambda r: (r, 0)),
                  _resident((None, D_MODEL, D_MODEL), lambda r: (l, 0, 0))],
        out_specs=xrow,
        out_shape=jax.ShapeDtypeStruct((N_ROWS, D_MODEL), F32),
        input_output_aliases={0: 0},
        compiler_params=_params("parallel"),
        name="matmul_residual_sample",
    )(x, y, w_all)


MEM_KV_BATCH = 2


def _mem_kv_kernel(m_ref, w_ref, k_ref, v_ref, kb_ref, vb_ref):
    kv = _dot(m_ref[...].astype(BF16), w_ref[...])
    kb_ref[...] = kv[:, :D_MODEL].astype(BF16)
    vb_ref[...] = kv[:, D_MODEL:].astype(BF16)
    for b in range(MEM_KV_BATCH):
        rows = slice(b * N_MEM, (b + 1) * N_MEM)
        k_ref[b] = kv[rows, :D_MODEL].reshape(N_MEM, N_XHEADS, XHEAD_DIM)
        v_ref[b] = kv[rows, D_MODEL:].reshape(N_MEM, N_XHEADS, XHEAD_DIM)


def _mem_kv(mem, wkv_all):
    bs = MEM_KV_BATCH
    out = pl.BlockSpec((None, bs, N_MEM, N_XHEADS, XHEAD_DIM), lambda l, r: (l, r, 0, 0, 0))
    shape = jax.ShapeDtypeStruct((DEPTH, BATCH, N_MEM, N_XHEADS, XHEAD_DIM), F32)
    out_b = pl.BlockSpec((None, bs * N_MEM, D_MODEL), lambda l, r: (l, r, 0))
    shape_b = jax.ShapeDtypeStruct((DEPTH, BATCH * N_MEM, D_MODEL), BF16)
    return pl.pallas_call(
        _mem_kv_kernel,
        grid=(DEPTH, BATCH // bs),
        in_specs=[pl.BlockSpec((bs * N_MEM, D_MODEL), lambda l, r: (r, 0)),
                  pl.BlockSpec((None, D_MODEL, 2 * D_MODEL), lambda l, r: (l, 0, 0))],
        out_specs=[out, out, out_b, out_b],
        out_shape=[shape, shape, shape_b, shape_b],
        compiler_params=_params("arbitrary", "arbitrary"),
        name="mem_kv",
    )(mem, wkv_all)


def _softmax_rows(s):
    e = jnp.exp(s - jnp.max(s, axis=1, keepdims=True))
    return e * (1.0 / jnp.sum(e, axis=1, keepdims=True))


N_MLSTM_SAMPLE_IN = 9


def _xattn_prompt_kernel(x_ref, g_ref, wq_ref, k_ref, v_ref, wo_ref, *rest, ride):
    a_sc = rest[-1]
    if ride is None:
        o_ref = rest[0]
    else:
        o_ref, y_ref, co_ref, no_ref, mo_ref = rest[-6:-1]
        _mlstm_sample_step(*rest[:N_MLSTM_SAMPLE_IN], y_ref, co_ref, no_ref, mo_ref, *ride)
    x = x_ref[...]
    q = _dot(_rms(x, g_ref[...]).astype(BF16), wq_ref[...])
    cols = [slice(h * XHEAD_DIM, (h + 1) * XHEAD_DIM) for h in range(N_XHEADS)]
    scores = [lax.dot_general(q[:, c].astype(BF16), k_ref[:, c], _NT,
                              preferred_element_type=F32) * (XHEAD_DIM ** -0.5) for c in cols]
    probs = [_softmax_rows(s).astype(BF16) for s in scores]
    for c, p in zip(cols, probs):
        a_sc[:, c] = _dot(p, v_ref[:, c]).astype(BF16)
    o_ref[...] = x + _dot(a_sc[...], wo_ref[...])


def _xattn_prompt(x, g, wq_all, k_all, v_all, wo_all, l, mlstm=None):
    tile = SEQ_TILE if mlstm is None else ROW_TILE
    nt = SEQ // tile
    xrow = pl.BlockSpec((tile, D_MODEL), lambda b, t: (b * nt + t, 0))
    kv = pl.BlockSpec((None, N_MEM, D_MODEL), lambda b, t: (l, b, 0))
    w = _resident((None, D_MODEL, D_MODEL), lambda b, t: (l, 0, 0))
    in_specs = [xrow, pl.BlockSpec((1, D_MODEL), lambda b, t: (0, 0)), w, kv, kv, w]
    args = [x, g, wq_all, k_all, v_all, wo_all]
    out_specs = [xrow]
    out_shape = [jax.ShapeDtypeStruct((N_ROWS, D_MODEL), F32)]
    aliases = {0: 0}
    ride = None
    if mlstm is not None:
        *step_args, j, c_prev = mlstm
        assert BATCH * nt * MLSTM_SAMPLE_BATCH == DEC_BATCH
        ride = (j, c_prev is None)
        s_in, s_args, s_out, s_shapes = _mlstm_sample_io(*step_args, *ride, lambda b, t: b * nt + t)
        assert len(s_in) == N_MLSTM_SAMPLE_IN
        in_specs += s_in
        args += s_args
        if c_prev is not None:
            in_specs.append(pl.BlockSpec(memory_space=pl.ANY))
            args.append(c_prev)
            aliases[len(args) - 1] = 2
        out_specs += s_out
        out_shape += s_shapes
    results = pl.pallas_call(
        functools.partial(_xattn_prompt_kernel, ride=ride),
        grid=(BATCH, nt),
        in_specs=in_specs,
        out_specs=out_specs,
        out_shape=out_shape,
        scratch_shapes=[pltpu.VMEM((tile, D_MODEL), BF16)],
        input_output_aliases=aliases,
        compiler_params=_params("arbitrary", "arbitrary"),
        name="xattn_prompt" + ("_mlstm" if ride else ""),
    )(*args)
    return results[0] if mlstm is None else results


def _sample_attention(qs, ks, vs):
    n_q, n_kv = N_XHEADS * DEC_SEQ, N_MEM * N_XHEADS
    q_head = lax.broadcasted_iota(jnp.int32, (n_q, n_kv), 0) // DEC_SEQ
    kv_head = lax.broadcasted_iota(jnp.int32, (n_q, n_kv), 1) % N_XHEADS
    same_head = q_head == kv_head
    scores = [lax.dot_general(q.astype(BF16), k.reshape(n_kv, XHEAD_DIM).astype(BF16), _NT,
                              preferred_element_type=F32) * (XHEAD_DIM ** -0.5)
              for q, k in zip(qs, ks)]
    probs = [_softmax_rows(jnp.where(same_head, s, -jnp.inf)).astype(BF16) for s in scores]
    return [_dot(p, v.reshape(n_kv, XHEAD_DIM).astype(BF16)) for p, v in zip(probs, vs)]


def _heads_major(a):
    a = a.reshape(DEC_BATCH, DEC_SEQ, N_XHEADS, XHEAD_DIM)
    return a.transpose(0, 2, 1, 3).reshape(DEC_BATCH, N_XHEADS * DEC_SEQ, XHEAD_DIM)


def _pool_prompt_kernel(x_ref, g_ref, win_ref, wgrp_ref, scale_ref, wout_ref,
                        o_ref, buf_ref, ext_sc, z_sc):
    tile = pl.program_id(1)

    @pl.when(tile == 0)
    def _():
        ext_sc[0:POOL_HALO, :] = jnp.zeros((POOL_HALO, D_MODEL), F32)

    x = x_ref[...]
    u = _dot(_rms(x, g_ref[...]).astype(BF16), win_ref[...])
    ext_sc[POOL_HALO:, :] = u
    pos = tile * SEQ_TILE + lax.broadcasted_iota(jnp.int32, (SEQ_TILE, 1), 0)
    for g, w in enumerate(POOL_WINDOWS):
        lo, hi = g * POOL_GROUP_DIM, (g + 1) * POOL_GROUP_DIM
        assert w & (w - 1) == 0 and w <= POOL_HALO
        win = ext_sc[:, lo:hi]
        span = 1
        while span < w:
            win = win + pltpu.roll(win, span, 0)
            span *= 2
        win = win[POOL_HALO:, :]
        cnt = jnp.minimum(pos + 1, w).astype(F32)
        d = win * (1.0 / cnt) - u[:, lo:hi]
        z_sc[:, lo:hi] = (_dot(d.astype(BF16), wgrp_ref[g]) * scale_ref[:, lo:hi]).astype(BF16)
    o_ref[...] = x + _dot(z_sc[...], wout_ref[...])
    tail = ext_sc[SEQ_TILE:SEQ_TILE + POOL_HALO, :]
    ext_sc[0:POOL_HALO, :] = tail

    @pl.when(tile == pl.num_programs(1) - 1)
    def _():
        buf_ref[...] = tail


def _pool_prompt(x, g, win_all, wgrp_all, scale, wout_all, j):
    nt = SEQ // SEQ_TILE
    xrow = pl.BlockSpec((SEQ_TILE, D_MODEL), lambda b, t: (b * nt + t, 0))
    vec = pl.BlockSpec((1, D_MODEL), lambda b, t: (0, 0))
    w = _resident((None, D_MODEL, D_MODEL), lambda b, t: (j, 0, 0))
    wgrp = _resident((None, len(POOL_WINDOWS), POOL_GROUP_DIM, POOL_GROUP_DIM),
                     lambda b, t: (j, 0, 0, 0))
    return pl.pallas_call(
        _pool_prompt_kernel,
        grid=(BATCH, nt),
        in_specs=[xrow, vec, w, wgrp, vec, w],
        out_specs=[xrow, pl.BlockSpec((None, POOL_HALO, D_MODEL), lambda b, t: (b, 0, 0))],
        out_shape=[jax.ShapeDtypeStruct((N_ROWS, D_MODEL), F32),
                   jax.ShapeDtypeStruct((BATCH, POOL_HALO, D_MODEL), F32)],
        scratch_shapes=[pltpu.VMEM((POOL_HALO + SEQ_TILE, D_MODEL), F32),
                        pltpu.VMEM((SEQ_TILE, D_MODEL), BF16)],
        input_output_aliases={0: 0},
        compiler_params=_params("arbitrary", "arbitrary"),
        name="pool_prompt",
    )(x, g, win_all, wgrp_all, scale, wout_all)


def _pool_sample_kernel(x_ref, buf_ref, g_ref, win_ref, wgrp_ref, scale_ref, *rest, layer, first):
    z_ref, nbuf_ref = rest[-2:]
    nbuf_ref = _own_layer(nbuf_ref, layer, first)
    u = _dot(_rms(x_ref[...], g_ref[...]).astype(BF16), win_ref[...])

    def ext(row, lo, hi):
        if row < POOL_BUF:
            return buf_ref[:, row, lo:hi]
        t = row - POOL_BUF
        return u[t * DEC_BATCH:(t + 1) * DEC_BATCH, lo:hi]

    for g, w in enumerate(POOL_WINDOWS):
        lo, hi = g * POOL_GROUP_DIM, (g + 1) * POOL_GROUP_DIM
        ds = []
        for t in range(DEC_SEQ):
            win = ext(POOL_BUF + t, lo, hi)
            for back in range(1, w):
                win = win + ext(POOL_BUF + t - back, lo, hi)
            cnt = float(min(PAST_LEN + 1 + t, w))
            ds.append(win / cnt - ext(POOL_BUF + t, lo, hi))
        d = jnp.concatenate(ds, axis=0).astype(BF16)
        z_ref[:, lo:hi] = _dot(d, wgrp_ref[g]) * scale_ref[:, lo:hi]
    for row in range(POOL_BUF):
        nbuf_ref[:, row, :] = ext(row + DEC_SEQ, 0, D_MODEL)


def _pool_sample(xs_t, buf_all, g, win_all, wgrp_all, scale, j, buf_prev):
    first = buf_prev is None
    buf_block = (DEC_BATCH, POOL_BUF, D_MODEL)
    once = lambda shape, imap: pl.BlockSpec(shape, imap, pipeline_mode=pl.Buffered(1))
    if first:
        nbuf = once((buf_all.shape[0],) + buf_block, lambda i: (0, 0, 0, 0))
    else:
        nbuf = once((None,) + buf_block, lambda i: (j, 0, 0, 0))
    vec = pl.BlockSpec((1, D_MODEL), lambda i: (0, 0))
    buf = once((None, DEC_BATCH, POOL_BUF, D_MODEL), lambda i: (j, 0, 0, 0))
    in_specs = [pl.BlockSpec((N_SAMPLE, D_MODEL), lambda i: (0, 0)), buf, vec,
                once((None, D_MODEL, D_MODEL), lambda i: (j, 0, 0)),
                once((None, len(POOL_WINDOWS), POOL_GROUP_DIM, POOL_GROUP_DIM), lambda i: (j, 0, 0, 0)),
                vec]
    args = [xs_t, buf_all, g, win_all, wgrp_all, scale]
    aliases = {}
    if buf_prev is not None:
        in_specs.append(pl.BlockSpec(memory_space=pl.ANY))
        args.append(buf_prev)
        aliases[len(args) - 1] = 1
    return pl.pallas_call(
        functools.partial(_pool_sample_kernel, layer=j, first=first),
        grid=(1,),
        in_specs=in_specs,
        out_specs=[pl.BlockSpec((N_SAMPLE, D_MODEL), lambda i: (0, 0)), nbuf],
        out_shape=[jax.ShapeDtypeStruct((N_SAMPLE, D_MODEL), F32),
                   jax.ShapeDtypeStruct(buf_all.shape, F32)],
        input_output_aliases=aliases,
        compiler_params=_params("arbitrary"),
        name="pool_sample",
    )(*args)


def _swap_token_seq(a, lead):
    n = a.shape[0] // lead
    return a.reshape(lead, n, a.shape[1]).transpose(1, 0, 2).reshape(a.shape)


def kernel(x_prompt, x_sample, mem_prompt, cache_mem_k, cache_mem_v, state_mlstm_C, state_mlstm_n,
           state_mlstm_m, state_pool_buf, norm_g, final_g, ffn_w_up, ffn_w_down, mlstm_w_in,
           mlstm_b_i, mlstm_b_f, mlstm_g_head, mlstm_w_out, pool_w_in, pool_w_grp, pool_scale,
           pool_w_out, xattn_w_q, xattn_w_kv, xattn_w_o):
    n_mlstm = mlstm_w_in.shape[0]
    ffn_w = {(0, 0): (ffn_w_up[0, 0].astype(BF16), ffn_w_down[0, 0].astype(BF16))}
    others = (mlstm_w_in, mlstm_w_out, pool_w_in, pool_w_grp, pool_w_out, xattn_w_q, xattn_w_kv,
              xattn_w_o)
    w_gate = jnp.pad(mlstm_w_in[:, :, 4 * D_MODEL:],
                     ((0, 0), (0, 0), (0, LANES - 2 * N_HEADS))).astype(BF16)
    gate_bias = jnp.concatenate([mlstm_b_i, mlstm_b_f], axis=1).astype(F32)

    x = [x_prompt.reshape(N_PROMPT, D_MODEL), x_sample.reshape(N_SAMPLE, D_MODEL)]
    c_p = c_s = buf_s = None
    n_p, m_p, n_s, m_s, buf_p = [], [], [], [], []
    for l in range(DEPTH):
        g = norm_g[l].reshape(4, 1, D_MODEL)
        j = l // 2
        later = [(l, 1)] + ([(l + 1, 0)] if l + 1 < DEPTH else [])
        jobs = []
        for cl, ci in later:
            part = dict(part=cl * ffn_w_up.shape[1] + ci, parts=DEPTH * ffn_w_up.shape[1])
            jobs += [_cast_job(ffn_w_up, **part),
                     _cast_job(ffn_w_down, slabs=D_FF // LANES, **part)]
        if l == 0:
            jobs += [_cast_job(w) for w in others]
        x, cast = _ffn(x if l == 0 else [x], g[0], *ffn_w[l, 0], l, cast=jobs)
        ffn_w.update((key, tuple(cast[2 * k:2 * k + 2])) for k, key in enumerate(later))
        if l == 0:
            w_in, w_mout, p_win, p_wgrp, p_wout, wq, wkv, wo = (
                c.reshape(w.shape) for c, w in zip(cast[2 * len(later):], others))
            mem_k, mem_v, mem_kb, mem_vb = _mem_kv(mem_prompt.reshape(BATCH * N_MEM, D_MODEL), wkv)
        if l % 2 == 0:
            p_p, gate_p, p_s, gate_s = _mlstm_proj(x, g[1], w_in, w_gate, j)
            bias_col = jnp.pad(gate_bias[j], (0, LANES - GATE_ROWS)).reshape(1, LANES)
            bias_row = gate_bias[j].reshape(GATE_ROWS, 1)
            g_head = mlstm_g_head[j].reshape(1, D_MODEL)
            x, c_p, n1, m1 = _mlstm_prompt(x, p_p, gate_p, gate_p[:, :GATE_ROWS].T, bias_col, bias_row,
                                           g_head, w_mout, j, n_mlstm, c_p)
            gate_s = gate_s.reshape(DEC_BATCH, DEC_SEQ, LANES)
            m_in = jnp.broadcast_to(
                jnp.pad(state_mlstm_m[j], ((0, 0), (0, GATE_ROWS - N_HEADS)))[:, :, None],
                (DEC_BATCH, GATE_ROWS, LANES))
            x, y_s, c_s, n2, m2 = _xattn_prompt(
                x, g[2], wq, mem_kb, mem_vb, wo, l,
                mlstm=(p_s.reshape(DEC_BATCH, DEC_SEQ, 4 * D_MODEL), gate_s,
                       gate_s[:, :, :GATE_ROWS].transpose(0, 2, 1),
                       bias_col, bias_row, g_head, state_mlstm_C, state_mlstm_n[j], m_in, j, c_s))
            x = _matmul_residual_sample(x, y_s.reshape(N_SAMPLE, D_MODEL), w_mout, j)
            n_p.append(n1); m_p.append(m1[:, :N_HEADS, 0])
            n_s.append(n2); m_s.append(m2[:, :N_HEADS, 0])
        else:
            scale = pool_scale[j].reshape(1, D_MODEL)
            x, b1 = _pool_prompt(x, g[1], p_win, p_wgrp, scale, p_wout, j)
            xs_t = _swap_token_seq(x[N_PROMPT:], DEC_BATCH)
            z_t, buf_s = _pool_sample(xs_t, state_pool_buf, g[1], p_win, p_wgrp, scale, j, buf_s)
            x = _matmul_residual_sample(x, _swap_token_seq(z_t, DEC_SEQ), p_wout, j)
            buf_p.append(b1[:, POOL_HALO - POOL_BUF:])
            x = _xattn_prompt(x, g[2], wq, mem_kb, mem_vb, wo, l)
        q_s = _norm_matmul_sample(x, g[2], wq, l)
        last = l == DEPTH - 1
        out = _ffn([x], g[3], *ffn_w[l, 1], l, final_g=final_g.reshape(1, D_MODEL) if last else None,
                   attend=(_heads_major(q_s), cache_mem_k, cache_mem_v, wo))
        x = out if last else out[0]

    y_prompt, y_sample = x
    return (y_prompt.reshape(BATCH, SEQ, D_MODEL), y_sample.reshape(DEC_BATCH, DEC_SEQ, D_MODEL),
            mem_k, mem_v, c_p, jnp.stack(n_p), jnp.stack(m_p),
            c_s, jnp.stack(n_s), jnp.stack(m_s), jnp.stack(buf_p), buf_s)
```
